```python
import math
import jax
import jax.numpy as jnp
from jax import lax
import numpy as np

D_MODEL = 1024
BATCH = 8
SEQ = 4096
DEPTH = 2

N_MIXERS = 4
D_MIX = D_MODEL
D_GROUP = D_MIX // N_MIXERS
HEAD_DIM = 64
N_HEADS = D_GROUP // HEAD_DIM
Q_BLOCK = 128
NEG_INF = -1e30
EPS = 1e-6

NSA_CMP_BLOCK = 32
NSA_CMP_STRIDE = 16
NSA_SEL_BLOCK = 64
NSA_TOP_N = 16
NSA_WINDOW = 512
NSA_CMP_HIDDEN = 128
NSA_FORCED_SCORE = 1e4

DIFF_QK_DIM = HEAD_DIM // 2

SSM_STATE = 64
SSM_GROUPS = 2
SSM_CHUNK = 128
SSM_XBC = D_GROUP + 2 * SSM_GROUPS * SSM_STATE
CONV_WIDTH = 4

MLSTM_CHUNK = 128

IN_SPLITS = (
    D_GROUP, HEAD_DIM, HEAD_DIM, HEAD_DIM, HEAD_DIM, HEAD_DIM, HEAD_DIM, 3 * N_HEADS, D_GROUP,
    D_GROUP, D_GROUP, D_GROUP, D_GROUP,
    D_GROUP, SSM_XBC, N_HEADS,
    2 * D_GROUP, D_GROUP, 2 * N_HEADS, D_GROUP, D_GROUP,
)
D_IN_PROJ = sum(IN_SPLITS)

kernel_name = "hybrid_parallel_nsa_diff_ssd_mlstm"


def rmsnorm(x, g):
    xf = x.astype(jnp.float32)
    y = xf * lax.rsqrt(jnp.mean(xf * xf, axis=-1, keepdims=True) + EPS)
    return (y * g).astype(x.dtype)


def causal_dwconv(x, w, b):
    width, ch = w.shape
    y = lax.conv_general_dilated(x, w[:, None, :], window_strides=(1,), padding=((width - 1, 0),),
                                 dimension_numbers=('NWC', 'WIO', 'NWC'), feature_group_count=ch)
    return y + b


def masked_softmax(s, valid):
    return jax.nn.softmax(jnp.where(valid, s, NEG_INF), axis=-1) * valid


def nsa_attention(q, kc, vc, ks, vs, kw, vw, gates, cmp_pos, ck_w1, ck_w2, cv_w1, cv_w2):
    bsz, seq, n_h, dh = q.shape
    scale = dh ** -0.5
    n_cmp = (seq - NSA_CMP_BLOCK) // NSA_CMP_STRIDE + 1
    n_sel = seq // NSA_SEL_BLOCK
    top_n = min(NSA_TOP_N, n_sel)
    n_qb = seq // Q_BLOCK
    span = NSA_WINDOW + Q_BLOCK

    cmp_idx = np.arange(n_cmp)[:, None] * NSA_CMP_STRIDE + np.arange(NSA_CMP_BLOCK)[None, :]
    cmp_end = jnp.asarray(cmp_idx[:, -1], jnp.int32)
    sel_start = np.arange(n_sel) * NSA_SEL_BLOCK
    overlap = np.clip(np.minimum(cmp_idx[:, -1:] + 1, sel_start[None, :] + NSA_SEL_BLOCK)
                      - np.maximum(cmp_idx[:, :1], sel_start[None, :]), 0, None)
    cmp_to_sel = jnp.asarray(overlap / NSA_CMP_BLOCK, jnp.float32)

    def compress(src, w1, w2):
        blk = src[:, cmp_idx] + cmp_pos
        hid = jax.nn.silu(blk.reshape(bsz, n_cmp, NSA_CMP_BLOCK * dh) @ w1)
        return hid @ w2

    k_cmp = compress(kc, ck_w1, ck_w2)
    v_cmp = compress(vc, cv_w1, cv_w2)
    ks_blk = ks.reshape(bsz, n_sel, NSA_SEL_BLOCK, dh)
    vs_blk = vs.reshape(bsz, n_sel, NSA_SEL_BLOCK, dh)
    kw_pad = jnp.pad(kw, ((0, 0), (NSA_WINDOW, 0), (0, 0)))
    vw_pad = jnp.pad(vw, ((0, 0), (NSA_WINDOW, 0), (0, 0)))
    b_idx = jnp.arange(bsz)[:, None, None]
    sel_ids = jnp.arange(n_sel)
    sel_off = jnp.arange(NSA_SEL_BLOCK)
    win_off = jnp.arange(span)

    def block(args):
        qi, q_blk, g_blk = args
        q0 = qi * Q_BLOCK
        t = q0 + jnp.arange(Q_BLOCK)
        s_c = jnp.einsum('bqhd,bnd->bhqn', q_blk, k_cmp).astype(jnp.float32) * scale
        p_c = masked_softmax(s_c, cmp_end[None, :] <= t[:, None])
        o_c = jnp.einsum('bhqn,bnd->bqhd', p_c.astype(v_cmp.dtype), v_cmp)
        imp = jnp.einsum('bhqn,ns->bqs', p_c, cmp_to_sel)
        cur = t // NSA_SEL_BLOCK
        forced = (sel_ids[None, :] == cur[:, None]) | (sel_ids[None, :] == 0)
        allowed = sel_ids[None, :] <= cur[:, None]
        imp = jnp.where(forced, NSA_FORCED_SCORE, jnp.where(allowed, imp, -1.0))
        _, top_idx = lax.top_k(imp, top_n)
        k_sel = ks_blk[b_idx, top_idx]
        v_sel = vs_blk[b_idx, top_idx]
        kpos = top_idx[..., None] * NSA_SEL_BLOCK + sel_off
        valid_s = (kpos <= t[None, :, None, None]).reshape(bsz, 1, Q_BLOCK, top_n * NSA_SEL_BLOCK)
        s_s = jnp.einsum('bqhd,bqnld->bhqnl', q_blk, k_sel).astype(jnp.float32) * scale
        p_s = masked_softmax(s_s.reshape(bsz, n_h, Q_BLOCK, top_n * NSA_SEL_BLOCK), valid_s)
        p_s = p_s.reshape(bsz, n_h, Q_BLOCK, top_n, NSA_SEL_BLOCK).astype(v_sel.dtype)
        o_s = jnp.einsum('bhqnl,bqnld->bqhd', p_s, v_sel)
        k_w = lax.dynamic_slice_in_dim(kw_pad, q0, span, axis=1)
        v_w = lax.dynamic_slice_in_dim(vw_pad, q0, span, axis=1)
        kpos_w = q0 - NSA_WINDOW + win_off
        valid_w = ((kpos_w[None, :] <= t[:, None]) & (kpos_w[None, :] > t[:, None] - NSA_WINDOW)
                   & (kpos_w[None, :] >= 0))
        s_w = jnp.einsum('bqhd,bkd->bhqk', q_blk, k_w).astype(jnp.float32) * scale
        p_w = masked_softmax(s_w, valid_w)
        o_w = jnp.einsum('bhqk,bkd->bqhd', p_w.astype(v_w.dtype), v_w)
        return g_blk[..., 0:1] * o_c + g_blk[..., 1:2] * o_s + g_blk[..., 2:3] * o_w

    q_blocks = jnp.moveaxis(q.reshape(bsz, n_qb, Q_BLOCK, n_h, dh), 1, 0)
    g_blocks = jnp.moveaxis(gates.reshape(bsz, n_qb, Q_BLOCK, n_h, 3), 1, 0)
    out = lax.map(block, (jnp.arange(n_qb), q_blocks, g_blocks))
    return jnp.moveaxis(out, 0, 1).reshape(bsz, seq, n_h, dh)


def diff_attention(q, k, v, lam, norm_g, layer_idx):
    bsz, seq, n_h = q.shape[:3]
    n_qb = seq // Q_BLOCK
    lambda_init = 0.8 - 0.6 * math.exp(-0.3 * layer_idx)
    lam = lam.astype(jnp.float32)
    lam_full = jnp.exp(jnp.sum(lam[0] * lam[1])) - jnp.exp(jnp.sum(lam[2] * lam[3])) + lambda_init
    scale = DIFF_QK_DIM ** -0.5
    kpos = jnp.arange(seq)

    def block(args):
        qi, q_blk = args
        t = qi * Q_BLOCK + jnp.arange(Q_BLOCK)
        s = jnp.einsum('bqhcd,bkhcd->bhcqk', q_blk, k).astype(jnp.float32) * scale
        p = jax.nn.softmax(jnp.where(kpos[None, :] <= t[:, None], s, NEG_INF), axis=-1)
        a = p[:, :, 0] - lam_full * p[:, :, 1]
        return jnp.einsum('bhqk,bkhd->bqhd', a.astype(v.dtype), v)

    q_blocks = jnp.moveaxis(q.reshape(bsz, n_qb, Q_BLOCK, n_h, 2, DIFF_QK_DIM), 1, 0)
    out = lax.map(block, (jnp.arange(n_qb), q_blocks))
    out = jnp.moveaxis(out, 0, 1).reshape(bsz, seq, n_h, -1)
    return rmsnorm(out, norm_g) * (1.0 - lambda_init)


def ssd_chunked(x, dt, a, bm, cm):
    bsz, seq, n_h, p = x.shape
    L = SSM_CHUNK
    nc = seq // L
    xdt = (x * dt[..., None]).reshape(bsz, nc, L, n_h, p)
    bc = bm.reshape(bsz, nc, L, n_h, SSM_STATE)
    cc = cm.reshape(bsz, nc, L, n_h, SSM_STATE)
    a_cs = jnp.cumsum((dt * a).reshape(bsz, nc, L, n_h), axis=2)
    causal = jnp.tril(jnp.ones((L, L), bool))[None, None, :, :, None]
    decay = jnp.exp(jnp.where(causal, a_cs[:, :, :, None, :] - a_cs[:, :, None, :, :], -jnp.inf))
    scores = jnp.einsum('bcthn,bcshn->bctsh', cc, bc) * decay
    y_diag = jnp.einsum('bctsh,bcshp->bcthp', scores, xdt)
    decay_end = jnp.exp(a_cs[:, :, -1:, :] - a_cs)
    states = jnp.einsum('bcshn,bcsh,bcshp->bchpn', bc, decay_end, xdt)
    chunk_decay = jnp.exp(a_cs[:, :, -1, :])

    def step(h, inp):
        st, dec = inp
        return dec[..., None, None] * h + st, h

    _, prev = lax.scan(step, jnp.zeros((bsz, n_h, p, SSM_STATE), jnp.float32),
                       (jnp.moveaxis(states, 1, 0), jnp.moveaxis(chunk_decay, 1, 0)))
    prev = jnp.moveaxis(prev, 0, 1)
    y_off = jnp.einsum('bcthn,bchpn,bcth->bcthp', cc, prev, jnp.exp(a_cs))
    return (y_diag + y_off).reshape(bsz, seq, n_h, p)


def ssd_mixer(xbc, dt_raw, z, conv_w, conv_b, dt_bias, a_log, d_skip, norm_g):
    bsz, seq, _ = xbc.shape
    xbc = jax.nn.silu(causal_dwconv(xbc, conv_w, conv_b))
    xs, bm, cm = jnp.split(xbc, [D_GROUP, D_GROUP + SSM_GROUPS * SSM_STATE], axis=-1)
    rep = N_HEADS // SSM_GROUPS
    xs = xs.reshape(bsz, seq, N_HEADS, HEAD_DIM).astype(jnp.float32)
    bm = jnp.repeat(bm.reshape(bsz, seq, SSM_GROUPS, SSM_STATE), rep, axis=2).astype(jnp.float32)
    cm = jnp.repeat(cm.reshape(bsz, seq, SSM_GROUPS, SSM_STATE), rep, axis=2).astype(jnp.float32)
    dt = jax.nn.softplus(dt_raw.astype(jnp.float32) + dt_bias.astype(jnp.float32))
    a = -jnp.exp(a_log.astype(jnp.float32))
    y = ssd_chunked(xs, dt, a, bm, cm) + d_skip.astype(jnp.float32)[:, None] * xs
    y = y.reshape(bsz, seq, D_GROUP).astype(z.dtype) * jax.nn.silu(z)
    y = rmsnorm(y.reshape(bsz, seq, SSM_GROUPS, -1), norm_g.reshape(SSM_GROUPS, -1))
    return y.reshape(bsz, seq, D_GROUP)


def mlstm_chunked(q, k, v, i_pre, f_pre):
    bsz, seq, n_h, dh = q.shape
    L = MLSTM_CHUNK
    nc = seq // L
    k = k * dh ** -0.5
    q = q.reshape(bsz, nc, L, n_h, dh)
    k = k.reshape(bsz, nc, L, n_h, dh)
    v = v.reshape(bsz, nc, L, n_h, dh)
    ig = i_pre.reshape(bsz, nc, L, n_h)
    b = jnp.cumsum(jax.nn.log_sigmoid(f_pre).reshape(bsz, nc, L, n_h), axis=2)
    causal = jnp.tril(jnp.ones((L, L), bool))[None, None, :, :, None]
    d_log = jnp.where(causal, b[:, :, :, None, :] - b[:, :, None, :, :] + ig[:, :, None, :, :], -jnp.inf)
    b_last = b[:, :, -1, :]
    g_end = b_last[:, :, None, :] - b + ig
    m_loc = jnp.max(g_end, axis=2)
    w_end = jnp.exp(g_end - m_loc[:, :, None, :])
    c_loc = jnp.einsum('bcsh,bcshd,bcshe->bchde', w_end, v, k)
    n_loc = jnp.einsum('bcsh,bcshe->bche', w_end, k)

    def step(carry, inp):
        c_st, n_st, m_st = carry
        c_l, n_l, m_l, b_l = inp
        m_new = jnp.maximum(b_l + m_st, m_l)
        a_prev = jnp.exp(b_l + m_st - m_new)
        a_loc = jnp.exp(m_l - m_new)
        c_new = a_prev[..., None, None] * c_st + a_loc[..., None, None] * c_l
        n_new = a_prev[..., None] * n_st + a_loc[..., None] * n_l
        return (c_new, n_new, m_new), (c_st, n_st, m_st)

    init = (jnp.zeros((bsz, n_h, dh, dh), jnp.float32), jnp.zeros((bsz, n_h, dh), jnp.float32),
            jnp.zeros((bsz, n_h), jnp.float32))
    xs = (jnp.moveaxis(c_loc, 1, 0), jnp.moveaxis(n_loc, 1, 0), jnp.moveaxis(m_loc, 1, 0),
          jnp.moveaxis(b_last, 1, 0))
    _, (c_prev, n_prev, m_prev) = lax.scan(step, init, xs)
    c_prev = jnp.moveaxis(c_prev, 0, 1)
    n_prev = jnp.moveaxis(n_prev, 0, 1)
    m_prev = jnp.moveaxis(m_prev, 0, 1)
    a_inter = b + m_prev[:, :, None, :]
    m_t = jnp.maximum(a_inter, jnp.max(d_log, axis=3))
    w_intra = jnp.exp(d_log - m_t[:, :, :, None, :])
    s_qk = jnp.einsum('bcthd,bcshd->bctsh', q, k) * w_intra
    w_inter = jnp.exp(a_inter - m_t)
    num = (jnp.einsum('bctsh,bcshd->bcthd', s_qk, v)
           + w_inter[..., None] * jnp.einsum('bcthe,bchde->bcthd', q, c_prev))
    den = jnp.sum(s_qk, axis=3) + w_inter * jnp.einsum('bcthe,bche->bcth', q, n_prev)
    h = num / jnp.maximum(jnp.abs(den), jnp.exp(-m_t))[..., None]
    return h.reshape(bsz, seq, n_h, dh)


def hybrid_layer(x, c_act, layer_idx, norm_g, ada_w, ada_b, w_in, w_out,
                 nsa_cmp_pos, nsa_ck_w1, nsa_ck_w2, nsa_cv_w1, nsa_cv_w2, nsa_norm_g,
                 diff_lam, diff_norm_g,
                 ssm_conv_w, ssm_conv_b, ssm_dt_bias, ssm_a_log, ssm_d, ssm_norm_g,
                 ml_conv_w, ml_conv_b, ml_if_b, ml_norm_g):
    bsz, seq, _ = x.shape
    shift, scale, gate = jnp.split(c_act @ ada_w + ada_b, 3, axis=-1)
    h = rmsnorm(x, norm_g) * (1.0 + scale[:, None, :]) + shift[:, None, :]
    offsets = [int(o) for o in np.cumsum(IN_SPLITS)[:-1]]
    (a_q, a_kc, a_vc, a_ks, a_vs, a_kw, a_vw, a_g, a_z,
     b_q, b_k, b_v, b_z,
     c_z, c_xbc, c_dt,
     d_qk, d_v, d_if, d_o, d_z) = jnp.split(h @ w_in, offsets, axis=-1)

    def heads(t):
        return t.reshape(bsz, seq, N_HEADS, -1)

    y_a = nsa_attention(heads(a_q), a_kc, a_vc, a_ks, a_vs, a_kw, a_vw,
                        jax.nn.sigmoid(a_g.reshape(bsz, seq, N_HEADS, 3)),
                        nsa_cmp_pos, nsa_ck_w1, nsa_ck_w2, nsa_cv_w1, nsa_cv_w2)
    y_a = rmsnorm(y_a, nsa_norm_g).reshape(bsz, seq, D_GROUP) * jax.nn.silu(a_z)
    y_b = diff_attention(b_q.reshape(bsz, seq, N_HEADS, 2, DIFF_QK_DIM),
                         b_k.reshape(bsz, seq, N_HEADS, 2, DIFF_QK_DIM),
                         heads(b_v), diff_lam, diff_norm_g, layer_idx)
    y_b = y_b.reshape(bsz, seq, D_GROUP) * jax.nn.silu(b_z)
    y_c = ssd_mixer(c_xbc, c_dt, c_z, ssm_conv_w, ssm_conv_b, ssm_dt_bias, ssm_a_log, ssm_d, ssm_norm_g)
    qk = jax.nn.silu(causal_dwconv(d_qk, ml_conv_w, ml_conv_b))
    m_q, m_k = jnp.split(qk, 2, axis=-1)
    i_pre, f_pre = jnp.split((d_if + ml_if_b).astype(jnp.float32), 2, axis=-1)
    h_m = mlstm_chunked(heads(m_q).astype(jnp.float32), heads(m_k).astype(jnp.float32),
                        heads(d_v).astype(jnp.float32), i_pre, f_pre).astype(x.dtype)
    h_m = jax.nn.sigmoid(heads(d_o)) * h_m
    y_d = rmsnorm(h_m, ml_norm_g).reshape(bsz, seq, D_GROUP) * jax.nn.silu(d_z)
    y = jnp.concatenate([y_a, y_b, y_c, y_d], axis=-1) @ w_out
    return x + gate[:, None, :] * y


def setup_inputs(seed: int = 0) -> dict:
    key = jax.random.key(seed)
    keys = iter(jax.random.split(key, 40))

    def nrm(shape, s):
        return s * jax.random.normal(next(keys), shape, jnp.float32)

    def unif(shape, lo, hi):
        return jax.random.uniform(next(keys), shape, jnp.float32, lo, hi)

    x = nrm((BATCH, SEQ, D_MODEL), 1.0)
    c = nrm((BATCH, D_MODEL), 1.0)
    norm_g = 1.0 + nrm((DEPTH, D_MODEL), 0.05)
    ada_w = nrm((DEPTH, D_MODEL, 3 * D_MODEL), D_MODEL ** -0.5)
    ada_b = nrm((DEPTH, 3 * D_MODEL), 0.02)
    w_in = nrm((DEPTH, D_MODEL, D_IN_PROJ), D_MODEL ** -0.5)
    w_out = nrm((DEPTH, D_MIX, D_MODEL), D_MIX ** -0.5)
    nsa_cmp_pos = nrm((DEPTH, NSA_CMP_BLOCK, HEAD_DIM), 0.1)
    cmp_in = NSA_CMP_BLOCK * HEAD_DIM
    nsa_ck_w1 = nrm((DEPTH, cmp_in, NSA_CMP_HIDDEN), cmp_in ** -0.5)
    nsa_ck_w2 = nrm((DEPTH, NSA_CMP_HIDDEN, HEAD_DIM), NSA_CMP_HIDDEN ** -0.5)
    nsa_cv_w1 = nrm((DEPTH, cmp_in, NSA_CMP_HIDDEN), cmp_in ** -0.5)
    nsa_cv_w2 = nrm((DEPTH, NSA_CMP_HIDDEN, HEAD_DIM), NSA_CMP_HIDDEN ** -0.5)
    nsa_norm_g = 1.0 + nrm((DEPTH, HEAD_DIM), 0.05)
    diff_lam = nrm((DEPTH, 4, DIFF_QK_DIM), 0.1)
    diff_norm_g = 1.0 + nrm((DEPTH, HEAD_DIM), 0.05)
    ssm_conv_w = nrm((DEPTH, CONV_WIDTH, SSM_XBC), CONV_WIDTH ** -0.5)
    ssm_conv_b = nrm((DEPTH, SSM_XBC), 0.02)
    dt0 = jnp.exp(unif((DEPTH, N_HEADS), math.log(1e-3), math.log(1e-1)))
    ssm_dt_bias = dt0 + jnp.log(-jnp.expm1(-dt0))
    ssm_a_log = jnp.log(unif((DEPTH, N_HEADS), 1.0, 16.0))
    ssm_d = 1.0 + nrm((DEPTH, N_HEADS), 0.1)
    ssm_norm_g = 1.0 + nrm((DEPTH, D_GROUP), 0.05)
    ml_conv_w = nrm((DEPTH, CONV_WIDTH, 2 * D_GROUP), CONV_WIDTH ** -0.5)
    ml_conv_b = nrm((DEPTH, 2 * D_GROUP), 0.02)
    f_bias = jnp.broadcast_to(jnp.linspace(3.0, 6.0, N_HEADS), (DEPTH, N_HEADS)) + nrm((DEPTH, N_HEADS), 0.1)
    ml_if_b = jnp.concatenate([nrm((DEPTH, N_HEADS), 0.1), f_bias], axis=-1)
    ml_norm_g = 1.0 + nrm((DEPTH, HEAD_DIM), 0.05)
    final_g = 1.0 + nrm((D_MODEL,), 0.05)
    return {'x': x, 'c': c, 'norm_g': norm_g, 'ada_w': ada_w, 'ada_b': ada_b, 'w_in': w_in, 'w_out': w_out,
            'nsa_cmp_pos': nsa_cmp_pos, 'nsa_ck_w1': nsa_ck_w1, 'nsa_ck_w2': nsa_ck_w2,
            'nsa_cv_w1': nsa_cv_w1, 'nsa_cv_w2': nsa_cv_w2, 'nsa_norm_g': nsa_norm_g,
            'diff_lam': diff_lam, 'diff_norm_g': diff_norm_g,
            'ssm_conv_w': ssm_conv_w, 'ssm_conv_b': ssm_conv_b, 'ssm_dt_bias': ssm_dt_bias,
            'ssm_a_log': ssm_a_log, 'ssm_d': ssm_d, 'ssm_norm_g': ssm_norm_g,
            'ml_conv_w': ml_conv_w, 'ml_conv_b': ml_conv_b, 'ml_if_b': ml_if_b, 'ml_norm_g': ml_norm_g,
            'final_g': final_g}


def reference(x, c, norm_g, ada_w, ada_b, w_in, w_out,
              nsa_cmp_pos, nsa_ck_w1, nsa_ck_w2, nsa_cv_w1, nsa_cv_w2, nsa_norm_g,
              diff_lam, diff_norm_g,
              ssm_conv_w, ssm_conv_b, ssm_dt_bias, ssm_a_log, ssm_d, ssm_norm_g,
              ml_conv_w, ml_conv_b, ml_if_b, ml_norm_g, final_g):
    c_act = jax.nn.silu(c)
    for l in range(DEPTH):
        x = hybrid_layer(x, c_act, l, norm_g[l], ada_w[l], ada_b[l], w_in[l], w_out[l],
                         nsa_cmp_pos[l], nsa_ck_w1[l], nsa_ck_w2[l], nsa_cv_w1[l], nsa_cv_w2[l], nsa_norm_g[l],
                         diff_lam[l], diff_norm_g[l],
                         ssm_conv_w[l], ssm_conv_b[l], ssm_dt_bias[l], ssm_a_log[l], ssm_d[l], ssm_norm_g[l],
                         ml_conv_w[l], ml_conv_b[l], ml_if_b[l], ml_norm_g[l])
    return rmsnorm(x, final_g)
```

```python
import functools
import math

import numpy as np
import jax
import jax.numpy as jnp
from jax import lax
from jax.experimental import pallas as pl
from jax.experimental.pallas import tpu as pltpu

F32 = jnp.float32
BF16 = jnp.bfloat16

D_MODEL = 1024
N_HEADS = 4
HEAD_DIM = 64
D_GROUP = N_HEADS * HEAD_DIM
NEG_INF = -1e30
EPS = 1e-6

NSA_CMP_BLOCK = 32
NSA_CMP_STRIDE = 16
NSA_SEL_BLOCK = 64
NSA_TOP_N = 16
NSA_WINDOW = 512
NSA_CMP_HIDDEN = 128
NSA_FORCED_SCORE = 1e4

DIFF_QK_DIM = HEAD_DIM // 2
SSM_STATE = 64
SSM_GROUPS = 2
SSM_XBC = D_GROUP + 2 * SSM_GROUPS * SSM_STATE
CONV_WIDTH = 4
CHUNK = 128
CONV_PAD = 8

LANES = 128
D_PROJ = 4096

_SPLITS = (
    D_GROUP, HEAD_DIM, HEAD_DIM, HEAD_DIM, HEAD_DIM, HEAD_DIM, HEAD_DIM, 3 * N_HEADS, D_GROUP,
    D_GROUP, D_GROUP, D_GROUP, D_GROUP,
    D_GROUP, SSM_XBC, N_HEADS,
    2 * D_GROUP, D_GROUP, 2 * N_HEADS, D_GROUP, D_GROUP,
)
_OFFS = [0] + [int(o) for o in np.cumsum(_SPLITS)]
(_A_Q, _A_KC, _A_VC, _A_KS, _A_VS, _A_KW, _A_VW, _A_G, _A_Z,
 _B_Q, _B_K, _B_V, _B_Z, _C_Z, _C_XBC, _C_DT, _D_QK, _D_V, _D_IF, _D_O, _D_Z) = range(21)

SM_GATE = 0
SM_DT = 12
SM_I = 16
SM_F = 20

PB_A_Q = 0
PB_A_KCVC = 2
PB_A_KSVS = 3
PB_A_KWVW = 4
PB_SMALL = 5
PB_A_Z = 6
PB_B_Q = 8
PB_B_K = 10
PB_B_V = 12
PB_B_Z = 14
PB_C_XBC = 16
PB_C_Z = 20
PB_D_V = 22
PB_D_QK = 24
PB_D_O = 28
PB_D_Z = 30

VMEM_LIMIT = 56 * 1024 * 1024


def _cparams(sem):
    return pltpu.CompilerParams(dimension_semantics=sem, vmem_limit_bytes=VMEM_LIMIT)


def _relayout_w_in(w_in):
    def col(i):
        return w_in[:, _OFFS[i]:_OFFS[i + 1]]
    d = w_in.shape[0]
    small = jnp.concatenate(
        [col(_A_G), col(_C_DT), col(_D_IF), jnp.zeros((d, LANES - 24), w_in.dtype)], axis=1)
    parts = [col(_A_Q), col(_A_KC), col(_A_VC), col(_A_KS), col(_A_VS), col(_A_KW), col(_A_VW),
             small, col(_A_Z),
             col(_B_Q), col(_B_K), col(_B_V), col(_B_Z),
             col(_C_XBC), col(_C_Z),
             col(_D_V), col(_D_QK), col(_D_O), col(_D_Z)]
    out = jnp.concatenate(parts, axis=1)
    assert out.shape[1] == D_PROJ
    return out


def _dot(a, b):
    return jnp.dot(a, b, preferred_element_type=F32)


def _nt(a, b):
    return lax.dot_general(a, b, (((1,), (1,)), ((), ())), preferred_element_type=F32)


def _split3(a):
    hi = a.astype(BF16)
    r1 = a - hi.astype(F32)
    mid = r1.astype(BF16)
    lo = (r1 - mid.astype(F32)).astype(BF16)
    return hi, mid, lo


def _dot_lhs_f32(a, b_exact):
    hi, mid, lo = _split3(a)
    return _dot(hi, b_exact) + _dot(mid, b_exact) + _dot(lo, b_exact)


def _dot_rhs_f32(a_exact, b):
    hi, mid, lo = _split3(b)
    return _dot(a_exact, hi) + _dot(a_exact, mid) + _dot(a_exact, lo)


def _nt_rhs_f32(a_exact, b):
    hi, mid, lo = _split3(b)
    return _nt(a_exact, hi) + _nt(a_exact, mid) + _nt(a_exact, lo)


def _sigmoid(x):
    return 1.0 / (1.0 + jnp.exp(-x))


def _silu(x):
    return x * _sigmoid(x)


def _softplus(x):
    return jnp.maximum(x, 0.0) + jnp.log(1.0 + jnp.exp(-jnp.abs(x)))


def _iota(shape, dim):
    return lax.broadcasted_iota(jnp.int32, shape, dim)


def _ada_kernel(c_ref, w_ref, b_ref, o_ref):
    ca = _silu(c_ref[...])
    w = w_ref[0]
    c_hi, c_mid, _ = _split3(ca)
    w_hi, w_mid, _ = _split3(w)
    acc = _dot(c_hi, w_hi) + _dot(c_hi, w_mid) + _dot(c_mid, w_hi)
    o_ref[0] = acc + b_ref[0]


def _ada_modulation(c, ada_w, ada_b):
    depth, d, d3 = ada_w.shape
    bsz = c.shape[0]
    nb = d3 // d
    return pl.pallas_call(
        _ada_kernel,
        grid=(depth, nb),
        in_specs=[pl.BlockSpec((bsz, d), lambda l, j: (0, 0)),
                  pl.BlockSpec((1, d, d), lambda l, j: (l, 0, j)),
                  pl.BlockSpec((1, 1, d), lambda l, j: (l, 0, j))],
        out_specs=pl.BlockSpec((1, bsz, d), lambda l, j: (l, 0, j)),
        out_shape=jax.ShapeDtypeStruct((depth, bsz, d3), F32),
        compiler_params=_cparams(("arbitrary", "arbitrary")),
        name="ada_modulation",
    )(c, ada_w, ada_b.reshape(depth, 1, d3))


def _inproj_kernel(x_ref, g_ref, sc_ref, sh_ref, w_ref, o_ref, h_scr):
    @pl.when(pl.program_id(2) == 0)
    def _():
        x = x_ref[0]
        ms = jnp.mean(x * x, axis=-1, keepdims=True)
        y = x * lax.rsqrt(ms + EPS) * g_ref[...]
        h = y * (1.0 + sc_ref[0]) + sh_ref[0]
        h_scr[...] = h.astype(BF16)
    o_ref[0] = _dot(h_scr[...], w_ref[...])


def _in_projection(x, norm_g, scale, shift, w_bf):
    bsz, seq, d = x.shape
    tm = min(1024, seq)
    tn = 1024
    return pl.pallas_call(
        _inproj_kernel,
        grid=(bsz, seq // tm, D_PROJ // tn),
        in_specs=[pl.BlockSpec((1, tm, d), lambda b, i, j: (b, i, 0)),
                  pl.BlockSpec((1, d), lambda b, i, j: (0, 0)),
                  pl.BlockSpec((1, 1, d), lambda b, i, j: (b, 0, 0)),
                  pl.BlockSpec((1, 1, d), lambda b, i, j: (b, 0, 0)),
                  pl.BlockSpec((d, tn), lambda b, i, j: (0, j))],
        out_specs=pl.BlockSpec((1, tm, tn), lambda b, i, j: (b, i, j)),
        out_shape=jax.ShapeDtypeStruct((bsz, seq, D_PROJ), F32),
        scratch_shapes=[pltpu.VMEM((tm, d), BF16)],
        compiler_params=_cparams(("arbitrary", "arbitrary", "arbitrary")),
        name="in_projection",
    )(x, norm_g.reshape(1, d), scale, shift, w_bf)


def _compress_kernel(rk_ref, rv_ref, pos_ref, kw1_ref, kw2_ref, vw1_ref, vw2_ref, kv_ref, vt_ref):
    half = NSA_CMP_STRIDE * HEAD_DIM
    pos = jnp.broadcast_to(pos_ref[...], (8, 2 * half)).astype(BF16)

    def one(r_ref, w1_ref, w2_ref):
        r = r_ref[0].astype(BF16)
        w1 = w1_ref[...].astype(BF16)
        first = _dot(r, w1[:half])
        second = _dot(r, w1[half:])
        n = first.shape[0]
        second = pltpu.roll(second, n - 1, 0)
        bias = _dot(pos, w1)[0:1]
        hid = _silu(first + second + bias)
        return _dot(hid.astype(BF16), w2_ref[...].astype(BF16))

    kc = one(rk_ref, kw1_ref, kw2_ref)
    vc = one(rv_ref, vw1_ref, vw2_ref)
    kv = jnp.concatenate([kc, vc], axis=1)
    kv_ref[0] = kv.astype(BF16)
    vt_ref[0] = kv.T[HEAD_DIM:].astype(BF16)


def _nsa_compress(rk, rv, pos, kw1, kw2, vw1, vw2):
    bsz, n, width = rk.shape
    full2 = lambda b: (0, 0)
    return pl.pallas_call(
        _compress_kernel,
        grid=(bsz,),
        in_specs=[pl.BlockSpec((1, n, width), lambda b: (b, 0, 0)),
                  pl.BlockSpec((1, n, width), lambda b: (b, 0, 0)),
                  pl.BlockSpec(pos.shape, full2),
                  pl.BlockSpec(kw1.shape, full2), pl.BlockSpec(kw2.shape, full2),
                  pl.BlockSpec(vw1.shape, full2), pl.BlockSpec(vw2.shape, full2)],
        out_specs=[pl.BlockSpec((1, n, LANES), lambda b: (b, 0, 0)),
                   pl.BlockSpec((1, HEAD_DIM, n), lambda b: (b, 0, 0))],
        out_shape=[jax.ShapeDtypeStruct((bsz, n, LANES), BF16),
                   jax.ShapeDtypeStruct((bsz, HEAD_DIM, n), BF16)],
        compiler_params=_cparams(("arbitrary",)),
        name="nsa_compress",
    )(rk, rv, pos, kw1, kw2, vw1, vw2)


def _softmax_cols_update(s, carry, vt, valid=None):
    m, l, acc = carry
    if valid is not None:
        s = jnp.where(valid, s, NEG_INF)
    m_new = jnp.maximum(m, jnp.max(s, axis=0, keepdims=True))
    p = jnp.exp(s - m_new)
    alpha = jnp.exp(m - m_new)
    l_new = alpha * l + jnp.sum(p, axis=0, keepdims=True)
    acc_new = alpha * acc + _dot(vt, p.astype(BF16))
    return m_new, l_new, acc_new


def _nsa_kernel(q_ref, ks_ref, kw_ref, sm_ref, z_ref, ckv_ref, cvt_ref, c2s_ref, ng_ref, o_ref,
                ksb, kwb, vst, vwt, sel_scr, *, seq, tq, tk, top_n):
    i = pl.program_id(1)
    n_cmp = seq // NSA_CMP_STRIDE
    n_sel = seq // NSA_SEL_BLOCK
    cols = N_HEADS * tq
    span = NSA_WINDOW + tq

    @pl.when(i == 0)
    def _():
        for blk in range(seq // LANES):
            rows = slice(blk * LANES, (blk + 1) * LANES)
            ks = ks_ref[0, rows, :]
            kw = kw_ref[0, rows, :]
            ksb[rows, :] = ks.astype(BF16)
            kwb[rows, :] = kw.astype(BF16)
            vst[:, rows] = ks.T[HEAD_DIM:].astype(BF16)
            vwt[:, rows] = kw.T[HEAD_DIM:].astype(BF16)

    q0 = pl.multiple_of(i * tq, tq)
    q = q_ref[0] * (HEAD_DIM ** -0.5)
    low = _iota((tq, LANES), 1) < HEAD_DIM
    heads = []
    for pair in range(N_HEADS // 2):
        qp = q[:, pair * LANES:(pair + 1) * LANES]
        heads.append(jnp.where(low, qp, 0.0))
        heads.append(jnp.where(low, pltpu.roll(qp, HEAD_DIM, 1), 0.0))
    q4 = jnp.concatenate(heads, axis=0).astype(BF16)

    t_row = q0 + _iota((1, tq), 1)
    t_cols = jnp.concatenate([t_row] * N_HEADS, axis=1)

    ckv = ckv_ref[0]
    s_c = _nt(ckv, q4)
    cmp_end = _iota((n_cmp, 1), 0) * NSA_CMP_STRIDE + (NSA_CMP_BLOCK - 1)
    valid_c = cmp_end <= t_cols
    s_c = jnp.where(valid_c, s_c, NEG_INF)
    m_c = jnp.max(s_c, axis=0, keepdims=True)
    p_c = jnp.where(valid_c, jnp.exp(s_c - m_c), 0.0)
    l_c = jnp.sum(p_c, axis=0, keepdims=True)
    p_c = p_c * (1.0 / jnp.where(l_c > 0.0, l_c, 1.0))
    o_c = _dot(cvt_ref[0], p_c.astype(BF16))

    p_sum = p_c[:, 0:tq]
    for h in range(1, N_HEADS):
        p_sum = p_sum + p_c[:, h * tq:(h + 1) * tq]
    p_hi = p_sum.astype(BF16)
    p_lo = (p_sum - p_hi.astype(F32)).astype(BF16)
    c2s = c2s_ref[...]
    imp = _dot(c2s, p_hi) + _dot(c2s, p_lo)
    sid = _iota((n_sel, tq), 0)
    cur = t_row // NSA_SEL_BLOCK
    forced = (sid == cur) | (sid == 0)
    imp = jnp.where(forced, NSA_FORCED_SCORE, jnp.where(sid <= cur, imp, -1.0))
    rank = jnp.zeros((n_sel, tq), F32)
    for j in range(n_sel):
        vj = imp[j:j + 1, :]
        ge = jnp.where(vj >= imp, 1.0, 0.0)
        gt = jnp.where(vj > imp, 1.0, 0.0)
        rank = rank + jnp.where(sid > j, ge, gt)
    sel_scr[...] = jnp.where(rank < float(top_n), 1.0, 0.0)

    blocks_per_chunk = tk // NSA_SEL_BLOCK

    def sel_step(kstart, carry, causal):
        kstart = pl.multiple_of(kstart, tk)
        s = _nt(ksb[pl.ds(kstart, tk), :], q4)
        sb = pl.multiple_of(kstart // NSA_SEL_BLOCK, blocks_per_chunk)
        selc = sel_scr[pl.ds(sb, blocks_per_chunk), :]
        rows = [jnp.broadcast_to(selc[r:r + 1, :], (NSA_SEL_BLOCK, tq)) for r in range(blocks_per_chunk)]
        selx = jnp.concatenate(rows, axis=0)
        valid = jnp.concatenate([selx] * N_HEADS, axis=1) > 0.5
        if causal:
            kpos = kstart + _iota((tk, 1), 0)
            valid = valid & (kpos <= t_cols)
        return _softmax_cols_update(s, carry, vst[:, pl.ds(kstart, tk)], valid)

    init = (jnp.full((1, cols), NEG_INF, F32), jnp.zeros((1, cols), F32), jnp.zeros((HEAD_DIM, cols), F32))
    n_full = q0 // tk
    carry = lax.fori_loop(0, n_full, lambda c, cr: sel_step(c * tk, cr, False), init)
    _, l_s, acc_s = sel_step(n_full * tk, carry, True)
    o_s = acc_s * (1.0 / l_s)

    wstart = pl.multiple_of(jnp.maximum(q0 - NSA_WINDOW, 0), tq)
    s_w = _nt(kwb[pl.ds(wstart, span), :], q4)
    kpos_w = wstart + _iota((span, 1), 0)
    valid_w = (kpos_w <= t_cols) & (kpos_w > t_cols - NSA_WINDOW)
    _, l_w, acc_w = _softmax_cols_update(s_w, init, vwt[:, pl.ds(wstart, span)], valid_w)
    o_w = acc_w * (1.0 / l_w)

    gates = _sigmoid(sm_ref[0].T[0:16])
    ng = ng_ref[...]
    outs = []
    for h in range(N_HEADS):
        c0 = h * tq
        g_c = gates[3 * h:3 * h + 1]
        g_s = gates[3 * h + 1:3 * h + 2]
        g_w = gates[3 * h + 2:3 * h + 3]
        oh = g_c * o_c[:, c0:c0 + tq] + g_s * o_s[:, c0:c0 + tq] + g_w * o_w[:, c0:c0 + tq]
        ms = jnp.mean(oh * oh, axis=0, keepdims=True)
        outs.append(oh * lax.rsqrt(ms + EPS) * ng)
    y = jnp.concatenate(
        [jnp.concatenate(outs[0:2], axis=0).T, jnp.concatenate(outs[2:4], axis=0).T], axis=1)
    o_ref[0] = (y * _silu(z_ref[0])).astype(o_ref.dtype)


def _nsa_attention(proj, ckv, cvt, c2s_t, norm_g, *, tq=128, tk=512):
    bsz, seq, _ = proj.shape
    tk = min(tk, seq)
    n_cmp = seq // NSA_CMP_STRIDE
    n_sel = seq // NSA_SEL_BLOCK
    top_n = min(NSA_TOP_N, n_sel)
    ng = jnp.broadcast_to(norm_g.reshape(HEAD_DIM, 1), (HEAD_DIM, tq)).astype(F32)
    kern = functools.partial(_nsa_kernel, seq=seq, tq=tq, tk=tk, top_n=top_n)
    return pl.pallas_call(
        kern,
        grid=(bsz, seq // tq),
        in_specs=[pl.BlockSpec((1, tq, 2 * LANES), lambda b, i: (b, i, PB_A_Q // 2)),
                  pl.BlockSpec((1, seq, LANES), lambda b, i: (b, 0, PB_A_KSVS)),
                  pl.BlockSpec((1, seq, LANES), lambda b, i: (b, 0, PB_A_KWVW)),
                  pl.BlockSpec((1, tq, LANES), lambda b, i: (b, i, PB_SMALL)),
                  pl.BlockSpec((1, tq, 2 * LANES), lambda b, i: (b, i, PB_A_Z // 2)),
                  pl.BlockSpec((1, n_cmp, LANES), lambda b, i: (b, 0, 0)),
                  pl.BlockSpec((1, HEAD_DIM, n_cmp), lambda b, i: (b, 0, 0)),
                  pl.BlockSpec((n_sel, n_cmp), lambda b, i: (0, 0)),
                  pl.BlockSpec((HEAD_DIM, tq), lambda b, i: (0, 0))],
        out_specs=pl.BlockSpec((1, tq, D_GROUP), lambda b, i: (b, i, 0)),
        out_shape=jax.ShapeDtypeStruct((bsz, seq, D_GROUP), BF16),
        scratch_shapes=[pltpu.VMEM((seq, LANES), BF16), pltpu.VMEM((seq, LANES), BF16),
                        pltpu.VMEM((HEAD_DIM, seq), BF16), pltpu.VMEM((HEAD_DIM, seq), BF16),
                        pltpu.VMEM((n_sel, tq), F32)],
        compiler_params=_cparams(("arbitrary", "arbitrary")),
        name="nsa_attention",
    )(proj, proj, proj, proj, proj, ckv, cvt, c2s_t, ng)


def _diff_kernel(q_ref, k_ref, v_ref, z_ref, lam_ref, ng_ref, o_ref, kb, vt, *, seq, tq, tk, lambda_init):
    i = pl.program_id(2)
    cols = 4 * tq

    @pl.when(i == 0)
    def _():
        for blk in range(seq // LANES):
            rows = slice(blk * LANES, (blk + 1) * LANES)
            kb[rows, :] = k_ref[0, rows, :].astype(BF16)
            vt[:, rows] = v_ref[0, rows, :].T.astype(BF16)

    lam = lam_ref[...]
    lam_full = (jnp.exp(jnp.sum(lam[0:1] * lam[1:2], axis=1, keepdims=True))
                - jnp.exp(jnp.sum(lam[2:3] * lam[3:4], axis=1, keepdims=True)) + lambda_init)

    q0 = pl.multiple_of(i * tq, tq)
    q = q_ref[0] * (DIFF_QK_DIM ** -0.5)
    part = _iota((tq, LANES), 1) // DIFF_QK_DIM
    q4 = jnp.concatenate([jnp.where(part == m, q, 0.0) for m in range(4)], axis=0).astype(BF16)
    t_cols = q0 + jnp.concatenate([_iota((1, tq), 1)] * 4, axis=1)

    def step(kstart, carry, causal):
        kstart = pl.multiple_of(kstart, tk)
        m, l, acc_a, acc_b = carry
        s = _nt(kb[pl.ds(kstart, tk), :], q4)
        if causal:
            kpos = kstart + _iota((tk, 1), 0)
            s = jnp.where(kpos <= t_cols, s, NEG_INF)
        m_new = jnp.maximum(m, jnp.max(s, axis=0, keepdims=True))
        p = jnp.exp(s - m_new)
        alpha = jnp.exp(m - m_new)
        l_new = alpha * l + jnp.sum(p, axis=0, keepdims=True)
        pb = p.astype(BF16)
        vt_c = vt[:, pl.ds(kstart, tk)]
        half = 2 * tq
        acc_a = alpha[:, :half] * acc_a + _dot(vt_c[:HEAD_DIM], pb[:, :half])
        acc_b = alpha[:, half:] * acc_b + _dot(vt_c[HEAD_DIM:], pb[:, half:])
        return m_new, l_new, acc_a, acc_b

    init = (jnp.full((1, cols), NEG_INF, F32), jnp.zeros((1, cols), F32),
            jnp.zeros((HEAD_DIM, 2 * tq), F32), jnp.zeros((HEAD_DIM, 2 * tq), F32))
    n_full = q0 // tk
    carry = lax.fori_loop(0, n_full, lambda c, cr: step(c * tk, cr, False), init)
    _, l, acc_a, acc_b = step(n_full * tk, carry, True)
    inv = 1.0 / l
    ng = ng_ref[...]
    outs = []
    for acc, off in ((acc_a, 0), (acc_b, 2 * tq)):
        o1 = acc[:, 0:tq] * inv[:, off:off + tq]
        o2 = acc[:, tq:2 * tq] * inv[:, off + tq:off + 2 * tq]
        o = o1 - lam_full * o2
        ms = jnp.mean(o * o, axis=0, keepdims=True)
        outs.append(o * lax.rsqrt(ms + EPS) * ng * (1.0 - lambda_init))
    y = jnp.concatenate(outs, axis=0).T
    o_ref[0] = (y * _silu(z_ref[0])).astype(o_ref.dtype)


def _diff_attention(proj, lam, norm_g, layer_idx, *, tq=128, tk=512):
    bsz, seq, _ = proj.shape
    tk = min(tk, seq)
    lambda_init = 0.8 - 0.6 * math.exp(-0.3 * layer_idx)
    ng = jnp.broadcast_to(norm_g.reshape(HEAD_DIM, 1), (HEAD_DIM, tq)).astype(F32)
    kern = functools.partial(_diff_kernel, seq=seq, tq=tq, tk=tk, lambda_init=lambda_init)
    return pl.pallas_call(
        kern,
        grid=(bsz, N_HEADS // 2, seq // tq),
        in_specs=[pl.BlockSpec((1, tq, LANES), lambda b, j, i: (b, i, PB_B_Q + j)),
                  pl.BlockSpec((1, seq, LANES), lambda b, j, i: (b, 0, PB_B_K + j)),
                  pl.BlockSpec((1, seq, LANES), lambda b, j, i: (b, 0, PB_B_V + j)),
                  pl.BlockSpec((1, tq, LANES), lambda b, j, i: (b, i, PB_B_Z + j)),
                  pl.BlockSpec((4, DIFF_QK_DIM), lambda b, j, i: (0, 0)),
                  pl.BlockSpec((HEAD_DIM, tq), lambda b, j, i: (0, 0))],
        out_specs=pl.BlockSpec((1, tq, LANES), lambda b, j, i: (b, i, j)),
        out_shape=jax.ShapeDtypeStruct((bsz, seq, D_GROUP), BF16),
        scratch_shapes=[pltpu.VMEM((seq, LANES), BF16), pltpu.VMEM((LANES, seq), BF16)],
        compiler_params=_cparams(("arbitrary", "arbitrary", "arbitrary")),
        name="diff_attention",
    )(proj, proj, proj, proj, lam, ng)


def _causal_conv_silu(x_raw, xext, cw_ref, cb_ref, first):
    n = x_raw.shape[0]

    @pl.when(first)
    def _():
        xext[0:CONV_PAD, :] = jnp.zeros((CONV_PAD, x_raw.shape[1]), F32)

    xext[CONV_PAD:CONV_PAD + n, :] = x_raw
    cw = cw_ref[...]
    acc = cb_ref[...] + cw[0:1] * xext[pl.ds(CONV_PAD - 3, n), :]
    for k in range(1, CONV_WIDTH):
        acc = acc + cw[k:k + 1] * xext[pl.ds(CONV_PAD - 3 + k, n), :]
    xext[0:CONV_PAD, :] = x_raw[n - CONV_PAD:, :]
    return _silu(acc)


def _tri(n):
    return jnp.where(_iota((n, n), 0) >= _iota((n, n), 1), 1.0, 0.0).astype(BF16)


def _expand_mat(first_lane, width):
    r = _iota((LANES, width), 0)
    c = _iota((LANES, width), 1)
    return jnp.where(r - first_lane == c // HEAD_DIM, 1.0, 0.0).astype(BF16)


def _row_select(width):
    r = _iota((8, width), 0)
    c = _iota((8, width), 1)
    return jnp.where(c == r * HEAD_DIM, 1.0, 0.0).astype(BF16)


def _group_sum_mat(width, group):
    r = _iota((width, width), 0)
    c = _iota((width, width), 1)
    return jnp.where(r // group == c // group, 1.0, 0.0).astype(BF16)


def _ssd_kernel(z_ref, xbc_ref, sm_ref, cw_ref, cb_ref, dtb_ref, alog_ref, dsk_ref, ng_ref, o_ref,
                xext, st):
    ci = pl.program_id(1)
    n = CHUNK

    @pl.when(ci == 0)
    def _():
        st[...] = jnp.zeros(st.shape, F32)

    xc = _causal_conv_silu(xbc_ref[0], xext, cw_ref, cb_ref, ci == 0)
    xs = xc[:, 0:D_GROUP]
    bm = xc[:, D_GROUP:D_GROUP + LANES]
    cm = xc[:, D_GROUP + LANES:D_GROUP + 2 * LANES]

    dt_c = _softplus(sm_ref[0] + dtb_ref[...])
    dtx = _dot_lhs_f32(dt_c, _expand_mat(SM_DT, D_GROUP))
    a_x = -jnp.exp(alog_ref[...])
    tri = _tri(n)
    acs = _dot_rhs_f32(tri, dtx * a_x)
    acs_rows = _nt_rhs_f32(_row_select(D_GROUP), acs)
    acs_last = acs[n - 1:n, :]

    xdt = xs * dtx
    xdt_b = xdt.astype(BF16)
    bm_b = bm.astype(BF16)
    cm_b = cm.astype(BF16)
    lane = _iota((n, LANES), 1)
    causal = _iota((n, n), 0) >= _iota((n, n), 1)

    y_pairs = []
    for g in range(SSM_GROUPS):
        in_group = (lane // SSM_STATE) == g
        gmat = _nt(jnp.where(in_group, cm, 0.0).astype(BF16), bm_b)
        pair = []
        for h in (2 * g, 2 * g + 1):
            col = acs[:, HEAD_DIM * h:HEAD_DIM * h + 1]
            row = acs_rows[h:h + 1, :]
            decay = jnp.exp(jnp.where(causal, col - row, NEG_INF))
            pair.append(_dot((gmat * decay).astype(BF16), xdt_b[:, g * LANES:(g + 1) * LANES]))
        y_pairs.append(jnp.where(lane < HEAD_DIM, pair[0], pair[1]))
    y_diag = jnp.concatenate(y_pairs, axis=1)

    state = st[...]
    y_off = jnp.exp(acs) * _dot(cm_b, state.astype(BF16))
    decay_end = jnp.exp(acs_last - acs)
    upd = _dot(bm.T.astype(BF16), (decay_end * xdt).astype(BF16))
    own = (_iota(st.shape, 0) // SSM_STATE) == (_iota(st.shape, 1) // LANES)
    st[...] = jnp.where(own, jnp.exp(acs_last) * state + upd, 0.0)

    y = (y_diag + y_off + dsk_ref[...] * xs) * _silu(z_ref[0])
    ms = _dot_lhs_f32(y * y, _group_sum_mat(D_GROUP, LANES)) * (1.0 / LANES)
    o_ref[0] = (y * lax.rsqrt(ms + EPS) * ng_ref[...]).astype(o_ref.dtype)


def _ssd_mixer(proj, conv_w, conv_b, dt_bias, a_log, d_skip, norm_g):
    bsz, seq, _ = proj.shape
    n = CHUNK
    dtb = jnp.zeros((1, LANES), F32).at[0, SM_DT:SM_DT + N_HEADS].set(dt_bias)
    alog_x = jnp.repeat(a_log, HEAD_DIM).reshape(1, D_GROUP)
    dsk_x = jnp.repeat(d_skip, HEAD_DIM).reshape(1, D_GROUP)
    full = lambda b, c: (0, 0)
    return pl.pallas_call(
        _ssd_kernel,
        grid=(bsz, seq // n),
        in_specs=[pl.BlockSpec((1, n, D_GROUP), lambda b, c: (b, c, PB_C_Z // 2)),
                  pl.BlockSpec((1, n, SSM_XBC), lambda b, c: (b, c, PB_C_XBC * LANES // SSM_XBC)),
                  pl.BlockSpec((1, n, LANES), lambda b, c: (b, c, PB_SMALL)),
                  pl.BlockSpec((CONV_WIDTH, SSM_XBC), full),
                  pl.BlockSpec((1, SSM_XBC), full),
                  pl.BlockSpec((1, LANES), full),
                  pl.BlockSpec((1, D_GROUP), full),
                  pl.BlockSpec((1, D_GROUP), full),
                  pl.BlockSpec((1, D_GROUP), full)],
        out_specs=pl.BlockSpec((1, n, D_GROUP), lambda b, c: (b, c, 0)),
        out_shape=jax.ShapeDtypeStruct((bsz, seq, D_GROUP), BF16),
        scratch_shapes=[pltpu.VMEM((CONV_PAD + n, SSM_XBC), F32), pltpu.VMEM((LANES, D_GROUP), F32)],
        compiler_params=_cparams(("arbitrary", "arbitrary")),
        name="ssd_mixer",
    )(proj, proj, proj, conv_w, conv_b.reshape(1, SSM_XBC), dtb, alog_x, dsk_x, norm_g.reshape(1, D_GROUP))


def _mlstm_kernel(qk_ref, v_ref, sm_ref, og_ref, z_ref, cw_ref, cb_ref, ifb_ref, ng_ref, o_ref,
                  xext, c_st, n_st, m_st):
    ci = pl.program_id(1)
    n = CHUNK

    @pl.when(ci == 0)
    def _():
        c_st[...] = jnp.zeros(c_st.shape, F32)
        n_st[...] = jnp.zeros(n_st.shape, F32)
        m_st[...] = jnp.zeros(m_st.shape, F32)

    qk = _causal_conv_silu(qk_ref[0], xext, cw_ref, cb_ref, ci == 0)
    q = qk[:, 0:D_GROUP]
    k = qk[:, D_GROUP:] * (HEAD_DIM ** -0.5)
    v = v_ref[0]
    q_b = q.astype(BF16)
    k_b = k.astype(BF16)
    v_b = v.astype(BF16)

    pre = sm_ref[0] + ifb_ref[...]
    logf = -_softplus(-pre)
    ig = _dot_lhs_f32(pre, _expand_mat(SM_I, D_GROUP))
    lf = _dot_lhs_f32(logf, _expand_mat(SM_F, D_GROUP))
    tri = _tri(n)
    b = _dot_rhs_f32(tri, lf)
    b_last = b[n - 1:n, :]
    sel = _row_select(D_GROUP)
    u_rows = _nt_rhs_f32(sel, ig - b)

    m_prev = m_st[...]
    c_prev = c_st[...]
    n_prev = n_st[...]

    lane = _iota((n, LANES), 1)
    causal = _iota((n, n), 0) >= _iota((n, n), 1)
    ones_b = jnp.ones((n, LANES), BF16)
    num_pairs, den_pairs, mt_pairs, wi_pairs = [], [], [], []
    for pr in range(N_HEADS // 2):
        lanes = slice(pr * LANES, (pr + 1) * LANES)
        qp = q[:, lanes]
        kp_b = k_b[:, lanes]
        rhs = jnp.concatenate([v_b[:, lanes], ones_b], axis=1)
        res, mts, wis = [], [], []
        for hh in range(2):
            h = 2 * pr + hh
            in_head = (lane // HEAD_DIM) == hh
            bcol = b[:, HEAD_DIM * h:HEAD_DIM * h + 1]
            dlog = jnp.where(causal, bcol + u_rows[h:h + 1, :], NEG_INF)
            inter = bcol + m_prev[:, HEAD_DIM * h:HEAD_DIM * h + 1]
            m_t = jnp.maximum(inter, jnp.max(dlog, axis=1, keepdims=True))
            s_qk = _nt(jnp.where(in_head, qp, 0.0).astype(BF16), kp_b) * jnp.exp(dlog - m_t)
            res.append(_dot(s_qk.astype(BF16), rhs))
            mts.append(jnp.broadcast_to(m_t, (n, LANES)))
            wis.append(jnp.broadcast_to(jnp.exp(inter - m_t), (n, LANES)))
        first = lane < HEAD_DIM
        num_pairs.append(jnp.where(first, res[0][:, :LANES], res[1][:, :LANES]))
        den_pairs.append(jnp.where(first, res[0][:, LANES:], res[1][:, LANES:]))
        mt_pairs.append(jnp.where(first, mts[0], mts[1]))
        wi_pairs.append(jnp.where(first, wis[0], wis[1]))
    num_intra = jnp.concatenate(num_pairs, axis=1)
    den_intra = jnp.concatenate(den_pairs, axis=1)
    m_t = jnp.concatenate(mt_pairs, axis=1)
    w_inter = jnp.concatenate(wi_pairs, axis=1)

    head_sum = _group_sum_mat(D_GROUP, HEAD_DIM)
    num = num_intra + w_inter * _dot(q_b, c_prev.astype(BF16))
    den = den_intra + w_inter * _dot_lhs_f32(q * n_prev, head_sum)
    hid = num / jnp.maximum(jnp.abs(den), jnp.exp(-m_t))

    g_end = b_last - b + ig
    m_loc = jnp.max(g_end, axis=0, keepdims=True)
    m_new = jnp.maximum(b_last + m_prev, m_loc)
    a_prev = jnp.exp(b_last + m_prev - m_new)
    a_loc = jnp.exp(m_loc - m_new)
    kw = k * (jnp.exp(g_end - m_loc) * a_loc)
    kw_t = jnp.concatenate([kw[:, :LANES].T, kw[:, LANES:].T], axis=0)
    upd = _dot(kw_t.astype(BF16), v_b)
    own = (_iota(c_st.shape, 0) // HEAD_DIM) == (_iota(c_st.shape, 1) // HEAD_DIM)
    c_st[...] = jnp.where(own, a_prev * c_prev + upd, 0.0)
    n_st[...] = a_prev * n_prev + jnp.sum(kw, axis=0, keepdims=True)
    m_st[...] = m_new

    hm = _sigmoid(og_ref[0]) * hid
    ms = _dot_lhs_f32(hm * hm, head_sum) * (1.0 / HEAD_DIM)
    o_ref[0] = (hm * lax.rsqrt(ms + EPS) * ng_ref[...] * _silu(z_ref[0])).astype(o_ref.dtype)


def _mlstm_mixer(proj, conv_w, conv_b, if_b, norm_g):
    bsz, seq, _ = proj.shape
    n = CHUNK
    ifb = jnp.zeros((1, LANES), F32).at[0, SM_I:SM_I + 2 * N_HEADS].set(if_b)
    ng = jnp.tile(norm_g, N_HEADS).reshape(1, D_GROUP)
    full = lambda b, c: (0, 0)
    return pl.pallas_call(
        _mlstm_kernel,
        grid=(bsz, seq // n),
        in_specs=[pl.BlockSpec((1, n, 2 * D_GROUP), lambda b, c: (b, c, PB_D_QK * LANES // (2 * D_GROUP))),
                  pl.BlockSpec((1, n, D_GROUP), lambda b, c: (b, c, PB_D_V // 2)),
                  pl.BlockSpec((1, n, LANES), lambda b, c: (b, c, PB_SMALL)),
                  pl.BlockSpec((1, n, D_GROUP), lambda b, c: (b, c, PB_D_O // 2)),
                  pl.BlockSpec((1, n, D_GROUP), lambda b, c: (b, c, PB_D_Z // 2)),
                  pl.BlockSpec((CONV_WIDTH, 2 * D_GROUP), full),
                  pl.BlockSpec((1, 2 * D_GROUP), full),
                  pl.BlockSpec((1, LANES), full),
                  pl.BlockSpec((1, D_GROUP), full)],
        out_specs=pl.BlockSpec((1, n, D_GROUP), lambda b, c: (b, c, 0)),
        out_shape=jax.ShapeDtypeStruct((bsz, seq, D_GROUP), BF16),
        scratch_shapes=[pltpu.VMEM((CONV_PAD + n, 2 * D_GROUP), F32),
                        pltpu.VMEM((D_GROUP, D_GROUP), F32),
                        pltpu.VMEM((1, D_GROUP), F32),
                        pltpu.VMEM((1, D_GROUP), F32)],
        compiler_params=_cparams(("arbitrary", "arbitrary")),
        name="mlstm_mixer",
    )(proj, proj, proj, proj, proj, conv_w, conv_b.reshape(1, 2 * D_GROUP), ifb, ng)


def _outproj_kernel(ya_ref, yb_ref, yc_ref, yd_ref, x_ref, gate_ref, w_ref, fg_ref, o_ref, *, final):
    acc = _dot(ya_ref[0], w_ref[0:D_GROUP, :])
    acc = acc + _dot(yb_ref[0], w_ref[D_GROUP:2 * D_GROUP, :])
    acc = acc + _dot(yc_ref[0], w_ref[2 * D_GROUP:3 * D_GROUP, :])
    acc = acc + _dot(yd_ref[0], w_ref[3 * D_GROUP:4 * D_GROUP, :])
    out = x_ref[0] + gate_ref[0] * acc
    if final:
        ms = jnp.mean(out * out, axis=-1, keepdims=True)
        out = out * lax.rsqrt(ms + EPS) * fg_ref[...]
    o_ref[0] = out


def _out_projection(ys, x, gate, w_bf, final_g, final):
    bsz, seq, d = x.shape
    tm = min(512, seq)
    yspec = pl.BlockSpec((1, tm, D_GROUP), lambda b, i: (b, i, 0))
    return pl.pallas_call(
        functools.partial(_outproj_kernel, final=final),
        grid=(bsz, seq // tm),
        in_specs=[yspec, yspec, yspec, yspec,
                  pl.BlockSpec((1, tm, d), lambda b, i: (b, i, 0)),
                  pl.BlockSpec((1, 1, d), lambda b, i: (b, 0, 0)),
                  pl.BlockSpec((N_HEADS * D_GROUP, d), lambda b, i: (0, 0)),
                  pl.BlockSpec((1, d), lambda b, i: (0, 0))],
        out_specs=pl.BlockSpec((1, tm, d), lambda b, i: (b, i, 0)),
        out_shape=jax.ShapeDtypeStruct((bsz, seq, d), F32),
        compiler_params=_cparams(("arbitrary", "arbitrary")),
        name="out_projection",
    )(*ys, x, gate, w_bf, final_g.reshape(1, d))


def _cmp_to_sel_t(seq):
    n_cmp = (seq - NSA_CMP_BLOCK) // NSA_CMP_STRIDE + 1
    n_sel = seq // NSA_SEL_BLOCK
    start = np.arange(n_cmp)[:, None] * NSA_CMP_STRIDE
    sel_start = np.arange(n_sel)[None, :] * NSA_SEL_BLOCK
    overlap = np.clip(np.minimum(start + NSA_CMP_BLOCK, sel_start + NSA_SEL_BLOCK)
                      - np.maximum(start, sel_start), 0, None)
    out = np.zeros((n_sel, seq // NSA_CMP_STRIDE), np.float32)
    out[:, :n_cmp] = (overlap / NSA_CMP_BLOCK).T
    return jnp.asarray(out, BF16)


def _mixers(proj, layer_idx, p):
    bsz, seq, _ = proj.shape
    n_rows = seq // NSA_CMP_STRIDE
    width = NSA_CMP_STRIDE * HEAD_DIM
    c0 = PB_A_KCVC * LANES
    rk = proj[:, :, c0:c0 + HEAD_DIM].reshape(bsz, n_rows, width)
    rv = proj[:, :, c0 + HEAD_DIM:c0 + 2 * HEAD_DIM].reshape(bsz, n_rows, width)
    pos = p['nsa_cmp_pos'].reshape(1, NSA_CMP_BLOCK * HEAD_DIM)
    ckv, cvt = _nsa_compress(rk, rv, pos, p['nsa_ck_w1'], p['nsa_ck_w2'], p['nsa_cv_w1'], p['nsa_cv_w2'])
    y_a = _nsa_attention(proj, ckv, cvt, _cmp_to_sel_t(seq), p['nsa_norm_g'])
    y_b = _diff_attention(proj, p['diff_lam'], p['diff_norm_g'], layer_idx)
    y_c = _ssd_mixer(proj, p['ssm_conv_w'], p['ssm_conv_b'], p['ssm_dt_bias'], p['ssm_a_log'],
                     p['ssm_d'], p['ssm_norm_g'])
    y_d = _mlstm_mixer(proj, p['ml_conv_w'], p['ml_conv_b'], p['ml_if_b'], p['ml_norm_g'])
    return y_a, y_b, y_c, y_d


_LAYER_PARAMS = ('nsa_cmp_pos', 'nsa_ck_w1', 'nsa_ck_w2', 'nsa_cv_w1', 'nsa_cv_w2', 'nsa_norm_g',
                 'diff_lam', 'diff_norm_g', 'ssm_conv_w', 'ssm_conv_b', 'ssm_dt_bias', 'ssm_a_log',
                 'ssm_d', 'ssm_norm_g', 'ml_conv_w', 'ml_conv_b', 'ml_if_b', 'ml_norm_g')


def kernel(x, c, norm_g, ada_w, ada_b, w_in, w_out, nsa_cmp_pos, nsa_ck_w1, nsa_ck_w2, nsa_cv_w1, nsa_cv_w2, nsa_norm_g, diff_lam, diff_norm_g, ssm_conv_w, ssm_conv_b, ssm_dt_bias, ssm_a_log, ssm_d, ssm_norm_g, ml_conv_w, ml_conv_b, ml_if_b, ml_norm_g, final_g):
    stacked = dict(nsa_cmp_pos=nsa_cmp_pos, nsa_ck_w1=nsa_ck_w1, nsa_ck_w2=nsa_ck_w2, nsa_cv_w1=nsa_cv_w1,
                   nsa_cv_w2=nsa_cv_w2, nsa_norm_g=nsa_norm_g, diff_lam=diff_lam, diff_norm_g=diff_norm_g,
                   ssm_conv_w=ssm_conv_w, ssm_conv_b=ssm_conv_b, ssm_dt_bias=ssm_dt_bias,
                   ssm_a_log=ssm_a_log, ssm_d=ssm_d, ssm_norm_g=ssm_norm_g, ml_conv_w=ml_conv_w,
                   ml_conv_b=ml_conv_b, ml_if_b=ml_if_b, ml_norm_g=ml_norm_g)
    depth = w_in.shape[0]
    bsz, seq, d = x.shape
    mod = _ada_modulation(c, ada_w, ada_b)
    for l in range(depth):
        p = {name: stacked[name][l] for name in _LAYER_PARAMS}
        shift = mod[l, :, 0:d].reshape(bsz, 1, d)
        scale = mod[l, :, d:2 * d].reshape(bsz, 1, d)
        gate = mod[l, :, 2 * d:3 * d].reshape(bsz, 1, d)
        w_bf = _relayout_w_in(w_in[l]).astype(BF16)
        proj = _in_projection(x, norm_g[l], scale, shift, w_bf)
        ys = _mixers(proj, l, p)
        x = _out_projection(ys, x, gate, w_out[l].astype(BF16), final_g, final=(l == depth - 1))
    return x
```

```python
import functools
import math

import numpy as np
import jax
import jax.numpy as jnp
from jax import lax
from jax.experimental import pallas as pl
from jax.experimental.pallas import tpu as pltpu

F32 = jnp.float32
BF16 = jnp.bfloat16

D_MODEL = 1024
N_HEADS = 4
HEAD_DIM = 64
D_GROUP = N_HEADS * HEAD_DIM
NEG_INF = -1e30
EPS = 1e-6

NSA_CMP_BLOCK = 32
NSA_CMP_STRIDE = 16
NSA_SEL_BLOCK = 64
NSA_TOP_N = 16
NSA_WINDOW = 512
NSA_CMP_HIDDEN = 128
NSA_FORCED_SCORE = 1e4

DIFF_QK_DIM = HEAD_DIM // 2
SSM_STATE = 64
SSM_GROUPS = 2
SSM_XBC = D_GROUP + 2 * SSM_GROUPS * SSM_STATE
CONV_WIDTH = 4
CHUNK = 128
CONV_PAD = 8

LANES = 128
D_PROJ = 4096

_SPLITS = (
    D_GROUP, HEAD_DIM, HEAD_DIM, HEAD_DIM, HEAD_DIM, HEAD_DIM, HEAD_DIM, 3 * N_HEADS, D_GROUP,
    D_GROUP, D_GROUP, D_GROUP, D_GROUP,
    D_GROUP, SSM_XBC, N_HEADS,
    2 * D_GROUP, D_GROUP, 2 * N_HEADS, D_GROUP, D_GROUP,
)
_OFFS = [0] + [int(o) for o in np.cumsum(_SPLITS)]
(_A_Q, _A_KC, _A_VC, _A_KS, _A_VS, _A_KW, _A_VW, _A_G, _A_Z,
 _B_Q, _B_K, _B_V, _B_Z, _C_Z, _C_XBC, _C_DT, _D_QK, _D_V, _D_IF, _D_O, _D_Z) = range(21)

SM_GATE = 0
SM_DT = 12
SM_I = 16
SM_F = 20

PB_A_Q = 0
PB_A_KCVC = 2
PB_A_KSVS = 3
PB_A_KWVW = 4
PB_SMALL = 5
PB_A_Z = 6
PB_B_Q = 8
PB_B_K = 10
PB_B_V = 12
PB_B_Z = 14
PB_C_XBC = 16
PB_C_Z = 20
PB_D_V = 22
PB_D_QK = 24
PB_D_O = 28
PB_D_Z = 30

VMEM_LIMIT = 56 * 1024 * 1024
LOG2E = 1.4426950408889634
SUM_ROWS = 16


def _cparams(sem):
    return pltpu.CompilerParams(dimension_semantics=sem, vmem_limit_bytes=VMEM_LIMIT)


def _relayout_w_in(w_in):
    def col(i):
        return w_in[:, _OFFS[i]:_OFFS[i + 1]]
    d = w_in.shape[0]
    small = jnp.concatenate(
        [col(_A_G), col(_C_DT), col(_D_IF), jnp.zeros((d, LANES - 24), w_in.dtype)], axis=1)
    parts = [col(_A_Q), col(_A_KC), col(_A_VC), col(_A_KS), col(_A_VS), col(_A_KW), col(_A_VW),
             small, col(_A_Z),
             col(_B_Q), col(_B_K), col(_B_V), col(_B_Z),
             col(_C_XBC), col(_C_Z),
             col(_D_V), col(_D_QK), col(_D_O), col(_D_Z)]
    out = jnp.concatenate(parts, axis=1)
    assert out.shape[1] == D_PROJ
    return out


def _dot(a, b):
    return jnp.dot(a, b, preferred_element_type=F32)


def _nt(a, b):
    return lax.dot_general(a, b, (((1,), (1,)), ((), ())), preferred_element_type=F32)


def _split3(a):
    hi = a.astype(BF16)
    r1 = a - hi.astype(F32)
    mid = r1.astype(BF16)
    lo = (r1 - mid.astype(F32)).astype(BF16)
    return hi, mid, lo


def _dot_lhs_f32(a, b_exact):
    hi, mid, lo = _split3(a)
    return _dot(hi, b_exact) + _dot(mid, b_exact) + _dot(lo, b_exact)


def _dot_rhs_f32(a_exact, b):
    hi, mid, lo = _split3(b)
    return _dot(a_exact, hi) + _dot(a_exact, mid) + _dot(a_exact, lo)


def _nt_rhs_f32(a_exact, b):
    hi, mid, lo = _split3(b)
    return _nt(a_exact, hi) + _nt(a_exact, mid) + _nt(a_exact, lo)


def _sigmoid(x):
    return 1.0 / (1.0 + jnp.exp(-x))


def _silu(x):
    return x * _sigmoid(x)


def _softplus(x):
    return jnp.maximum(x, 0.0) + jnp.log(1.0 + jnp.exp(-jnp.abs(x)))


def _iota(shape, dim):
    return lax.broadcasted_iota(jnp.int32, shape, dim)


def _ada_kernel(c_ref, w_ref, b_ref, o_ref):
    ca = _silu(c_ref[...])
    w = w_ref[0]
    c_hi, c_mid, _ = _split3(ca)
    w_hi, w_mid, _ = _split3(w)
    acc = _dot(c_hi, w_hi) + _dot(c_hi, w_mid) + _dot(c_mid, w_hi)
    o_ref[0] = acc + b_ref[0]


def _ada_modulation(c, ada_w, ada_b):
    depth, d, d3 = ada_w.shape
    bsz = c.shape[0]
    nb = d3 // d
    return pl.pallas_call(
        _ada_kernel,
        grid=(depth, nb),
        in_specs=[pl.BlockSpec((bsz, d), lambda l, j: (0, 0)),
                  pl.BlockSpec((1, d, d), lambda l, j: (l, 0, j)),
                  pl.BlockSpec((1, 1, d), lambda l, j: (l, 0, j))],
        out_specs=pl.BlockSpec((1, bsz, d), lambda l, j: (l, 0, j)),
        out_shape=jax.ShapeDtypeStruct((depth, bsz, d3), F32),
        compiler_params=_cparams(("arbitrary", "arbitrary")),
        name="ada_modulation",
    )(c, ada_w, ada_b.reshape(depth, 1, d3))


def _inproj_kernel(x_ref, g_ref, sc_ref, sh_ref, w_ref, o_ref, h_scr):
    @pl.when(pl.program_id(2) == 0)
    def _():
        x = x_ref[0]
        ms = jnp.mean(x * x, axis=-1, keepdims=True)
        y = x * lax.rsqrt(ms + EPS) * g_ref[...]
        h = y * (1.0 + sc_ref[0]) + sh_ref[0]
        h_scr[...] = h.astype(BF16)
    o_ref[0] = _dot(h_scr[...], w_ref[...])


def _in_projection(x, norm_g, scale, shift, w_bf):
    bsz, seq, d = x.shape
    tm = min(1024, seq)
    tn = 1024
    return pl.pallas_call(
        _inproj_kernel,
        grid=(bsz, seq // tm, D_PROJ // tn),
        in_specs=[pl.BlockSpec((1, tm, d), lambda b, i, j: (b, i, 0)),
                  pl.BlockSpec((1, d), lambda b, i, j: (0, 0)),
                  pl.BlockSpec((1, 1, d), lambda b, i, j: (b, 0, 0)),
                  pl.BlockSpec((1, 1, d), lambda b, i, j: (b, 0, 0)),
                  pl.BlockSpec((d, tn), lambda b, i, j: (0, j))],
        out_specs=pl.BlockSpec((1, tm, tn), lambda b, i, j: (b, i, j)),
        out_shape=jax.ShapeDtypeStruct((bsz, seq, D_PROJ), F32),
        scratch_shapes=[pltpu.VMEM((tm, d), BF16)],
        compiler_params=_cparams(("arbitrary", "arbitrary", "arbitrary")),
        name="in_projection",
    )(x, norm_g.reshape(1, d), scale, shift, w_bf)


def _compress_kernel(rk_ref, rv_ref, pos_ref, kw1_ref, kw2_ref, vw1_ref, vw2_ref, kv_ref, vt_ref):
    half = NSA_CMP_STRIDE * HEAD_DIM
    pos = jnp.broadcast_to(pos_ref[...], (8, 2 * half)).astype(BF16)

    def one(r_ref, w1_ref, w2_ref):
        r = r_ref[0].astype(BF16)
        w1 = w1_ref[...].astype(BF16)
        first = _dot(r, w1[:half])
        second = _dot(r, w1[half:])
        n = first.shape[0]
        second = pltpu.roll(second, n - 1, 0)
        bias = _dot(pos, w1)[0:1]
        hid = _silu(first + second + bias)
        return _dot(hid.astype(BF16), w2_ref[...].astype(BF16))

    kc = one(rk_ref, kw1_ref, kw2_ref)
    vc = one(rv_ref, vw1_ref, vw2_ref)
    kv = jnp.concatenate([kc, vc], axis=1)
    kv_ref[0] = kv.astype(BF16)
    vt_ref[0] = kv.T[HEAD_DIM:].astype(BF16)


def _nsa_compress(rk, rv, pos, kw1, kw2, vw1, vw2):
    bsz, n, width = rk.shape
    full2 = lambda b: (0, 0)
    return pl.pallas_call(
        _compress_kernel,
        grid=(bsz,),
        in_specs=[pl.BlockSpec((1, n, width), lambda b: (b, 0, 0)),
                  pl.BlockSpec((1, n, width), lambda b: (b, 0, 0)),
                  pl.BlockSpec(pos.shape, full2),
                  pl.BlockSpec(kw1.shape, full2), pl.BlockSpec(kw2.shape, full2),
                  pl.BlockSpec(vw1.shape, full2), pl.BlockSpec(vw2.shape, full2)],
        out_specs=[pl.BlockSpec((1, n, LANES), lambda b: (b, 0, 0)),
                   pl.BlockSpec((1, HEAD_DIM, n), lambda b: (b, 0, 0))],
        out_shape=[jax.ShapeDtypeStruct((bsz, n, LANES), BF16),
                   jax.ShapeDtypeStruct((bsz, HEAD_DIM, n), BF16)],
        compiler_params=_cparams(("arbitrary",)),
        name="nsa_compress",
    )(rk, rv, pos, kw1, kw2, vw1, vw2)


def _online_softmax(s, m):
    m_new = jnp.maximum(m, jnp.max(s, axis=0, keepdims=True))
    return m_new, jnp.exp2(s - m_new).astype(BF16), jnp.exp2(m - m_new)


def _nsa_kernel(q_ref, ks_ref, kw_ref, sm_ref, z_ref, ckv_ref, cvt_ref, c2s_ref, ng_ref, o_ref,
                ksb, kwb, vst, vwt, sel_scr, *, seq, tq, tk, top_n):
    i = pl.program_id(1)
    n_cmp = seq // NSA_CMP_STRIDE
    n_sel = seq // NSA_SEL_BLOCK
    span = NSA_WINDOW + tq
    vrows = HEAD_DIM + SUM_ROWS

    @pl.when(i == 0)
    def _():
        ones = jnp.ones((SUM_ROWS, LANES), BF16)
        for blk in range(seq // LANES):
            rows = slice(blk * LANES, (blk + 1) * LANES)
            ks = ks_ref[0, rows, :]
            kw = kw_ref[0, rows, :]
            ksb[rows, :] = ks.astype(BF16)
            kwb[rows, :] = kw.astype(BF16)
            vst[:, rows] = jnp.concatenate([ks.T[HEAD_DIM:].astype(BF16), ones], axis=0)
            vwt[:, rows] = jnp.concatenate([kw.T[HEAD_DIM:].astype(BF16), ones], axis=0)

    q0 = pl.multiple_of(i * tq, tq)
    q = q_ref[0] * (HEAD_DIM ** -0.5 * LOG2E)
    low = _iota((tq, LANES), 1) < HEAD_DIM
    qh = []
    for pair in range(N_HEADS // 2):
        qp = q[:, pair * LANES:(pair + 1) * LANES]
        qh.append(jnp.where(low, qp, 0.0).astype(BF16))
        qh.append(jnp.where(low, pltpu.roll(qp, HEAD_DIM, 1), 0.0).astype(BF16))

    t_row = q0 + _iota((1, tq), 1)

    ckv = ckv_ref[0]
    cmp_end = _iota((n_cmp, 1), 0) * NSA_CMP_STRIDE + (NSA_CMP_BLOCK - 1)
    valid_c = cmp_end <= t_row
    o_c = []
    p_sum = None
    for h in range(N_HEADS):
        s_c = jnp.where(valid_c, _nt(ckv, qh[h]), NEG_INF)
        m_c = jnp.max(s_c, axis=0, keepdims=True)
        p_c = jnp.where(valid_c, jnp.exp2(s_c - m_c), 0.0)
        l_c = jnp.sum(p_c, axis=0, keepdims=True)
        p_c = p_c * (1.0 / jnp.where(l_c > 0.0, l_c, 1.0))
        o_c.append(_dot(cvt_ref[0], p_c.astype(BF16)))
        p_sum = p_c if p_sum is None else p_sum + p_c

    p_hi = p_sum.astype(BF16)
    p_lo = (p_sum - p_hi.astype(F32)).astype(BF16)
    c2s = c2s_ref[...]
    imp = _dot(c2s, p_hi) + _dot(c2s, p_lo)
    sid = _iota((n_sel, tq), 0)
    cur = t_row // NSA_SEL_BLOCK
    forced = (sid == cur) | (sid == 0)
    imp = jnp.where(forced, NSA_FORCED_SCORE, jnp.where(sid <= cur, imp, -1.0))
    rank = jnp.zeros((n_sel, tq), F32)
    for j in range(n_sel):
        vj = imp[j:j + 1, :]
        ge = jnp.where(vj >= imp, 1.0, 0.0)
        gt = jnp.where(vj > imp, 1.0, 0.0)
        rank = rank + jnp.where(sid > j, ge, gt)
    sel_scr[...] = jnp.where(rank < float(top_n), 1.0, 0.0)

    blocks_per_chunk = tk // NSA_SEL_BLOCK

    def stream_step(scores_fn, vt_c, valid, ms, accs):
        ms, accs = list(ms), list(accs)
        s, p, alpha = [None] * N_HEADS, [None] * N_HEADS, [None] * N_HEADS
        s[0] = scores_fn(0)
        for h in range(N_HEADS):
            if h + 1 < N_HEADS:
                s[h + 1] = scores_fn(h + 1)
            ms[h], p[h], alpha[h] = _online_softmax(jnp.where(valid, s[h], NEG_INF), ms[h])
            if h >= 1:
                accs[h - 1] = alpha[h - 1] * accs[h - 1] + _dot(vt_c, p[h - 1])
        accs[-1] = alpha[-1] * accs[-1] + _dot(vt_c, p[-1])
        return tuple(ms), tuple(accs)

    def sel_step(kstart, carry, causal):
        kstart = pl.multiple_of(kstart, tk)
        k_c = ksb[pl.ds(kstart, tk), :]
        sb = pl.multiple_of(kstart // NSA_SEL_BLOCK, blocks_per_chunk)
        selc = sel_scr[pl.ds(sb, blocks_per_chunk), :]
        rows = [jnp.broadcast_to(selc[r:r + 1, :], (NSA_SEL_BLOCK, tq)) for r in range(blocks_per_chunk)]
        valid = jnp.concatenate(rows, axis=0) > 0.5
        if causal:
            valid = valid & (kstart + _iota((tk, 1), 0) <= t_row)
        return stream_step(lambda h: _nt(k_c, qh[h]), vst[:, pl.ds(kstart, tk)], valid, *carry)

    init = (tuple(jnp.full((1, tq), NEG_INF, F32) for _ in range(N_HEADS)),
            tuple(jnp.zeros((vrows, tq), F32) for _ in range(N_HEADS)))
    n_full = q0 // tk
    carry = lax.fori_loop(0, n_full, lambda c, cr: sel_step(c * tk, cr, False), init)
    _, acc_s = sel_step(n_full * tk, carry, True)

    wstart = pl.multiple_of(jnp.maximum(q0 - NSA_WINDOW, 0), tq)
    k_w = kwb[pl.ds(wstart, span), :]
    kpos_w = wstart + _iota((span, 1), 0)
    valid_w = (kpos_w <= t_row) & (kpos_w > t_row - NSA_WINDOW)
    _, acc_w = stream_step(lambda h: _nt(k_w, qh[h]), vwt[:, pl.ds(wstart, span)], valid_w, *init)

    gates = _sigmoid(sm_ref[0].T[0:16])
    ng = ng_ref[...]
    outs = []
    for h in range(N_HEADS):
        g_c = gates[3 * h:3 * h + 1]
        g_s = gates[3 * h + 1:3 * h + 2]
        g_w = gates[3 * h + 2:3 * h + 3]
        o_s = acc_s[h][:HEAD_DIM] * (1.0 / acc_s[h][HEAD_DIM:HEAD_DIM + 1])
        o_w = acc_w[h][:HEAD_DIM] * (1.0 / acc_w[h][HEAD_DIM:HEAD_DIM + 1])
        oh = g_c * o_c[h] + g_s * o_s + g_w * o_w
        ms = jnp.mean(oh * oh, axis=0, keepdims=True)
        outs.append(oh * lax.rsqrt(ms + EPS) * ng)
    y = jnp.concatenate(
        [jnp.concatenate(outs[0:2], axis=0).T, jnp.concatenate(outs[2:4], axis=0).T], axis=1)
    o_ref[0] = (y * _silu(z_ref[0])).astype(o_ref.dtype)


def _nsa_attention(proj, ckv, cvt, c2s_t, norm_g, *, tq=256, tk=512):
    bsz, seq, _ = proj.shape
    tk = min(tk, seq)
    n_cmp = seq // NSA_CMP_STRIDE
    n_sel = seq // NSA_SEL_BLOCK
    top_n = min(NSA_TOP_N, n_sel)
    ng = jnp.broadcast_to(norm_g.reshape(HEAD_DIM, 1), (HEAD_DIM, tq)).astype(F32)
    kern = functools.partial(_nsa_kernel, seq=seq, tq=tq, tk=tk, top_n=top_n)
    return pl.pallas_call(
        kern,
        grid=(bsz, seq // tq),
        in_specs=[pl.BlockSpec((1, tq, 2 * LANES), lambda b, i: (b, i, PB_A_Q // 2)),
                  pl.BlockSpec((1, seq, LANES), lambda b, i: (b, 0, PB_A_KSVS)),
                  pl.BlockSpec((1, seq, LANES), lambda b, i: (b, 0, PB_A_KWVW)),
                  pl.BlockSpec((1, tq, LANES), lambda b, i: (b, i, PB_SMALL)),
                  pl.BlockSpec((1, tq, 2 * LANES), lambda b, i: (b, i, PB_A_Z // 2)),
                  pl.BlockSpec((1, n_cmp, LANES), lambda b, i: (b, 0, 0)),
                  pl.BlockSpec((1, HEAD_DIM, n_cmp), lambda b, i: (b, 0, 0)),
                  pl.BlockSpec((n_sel, n_cmp), lambda b, i: (0, 0)),
                  pl.BlockSpec((HEAD_DIM, tq), lambda b, i: (0, 0))],
        out_specs=pl.BlockSpec((1, tq, D_GROUP), lambda b, i: (b, i, 0)),
        out_shape=jax.ShapeDtypeStruct((bsz, seq, D_GROUP), BF16),
        scratch_shapes=[pltpu.VMEM((seq, LANES), BF16), pltpu.VMEM((seq, LANES), BF16),
                        pltpu.VMEM((HEAD_DIM + SUM_ROWS, seq), BF16),
                        pltpu.VMEM((HEAD_DIM + SUM_ROWS, seq), BF16),
                        pltpu.VMEM((n_sel, tq), F32)],
        compiler_params=_cparams(("arbitrary", "arbitrary")),
        name="nsa_attention",
    )(proj, proj, proj, proj, proj, ckv, cvt, c2s_t, ng)


def _diff_kernel(q_ref, k_ref, v_ref, z_ref, lam_ref, ng_ref, o_ref, kb, vt, *, seq, tq, tk, lambda_init):
    i = pl.program_id(1)
    cols = 2 * tq
    vrows = HEAD_DIM + SUM_ROWS

    @pl.when(i == 0)
    def _():
        ones = jnp.ones((SUM_ROWS, LANES), BF16)
        for blk in range(seq // LANES):
            rows = slice(blk * LANES, (blk + 1) * LANES)
            kb[rows, :] = k_ref[0, rows, :].astype(BF16)
            parts = []
            for pair in range(N_HEADS // 2):
                v_t = v_ref[0, rows, pair * LANES:(pair + 1) * LANES].T.astype(BF16)
                parts += [v_t[:HEAD_DIM], ones, v_t[HEAD_DIM:], ones]
            vt[:, rows] = jnp.concatenate(parts, axis=0)

    lam = lam_ref[...]
    lam_full = (jnp.exp(jnp.sum(lam[0:1] * lam[1:2], axis=1, keepdims=True))
                - jnp.exp(jnp.sum(lam[2:3] * lam[3:4], axis=1, keepdims=True)) + lambda_init)

    q0 = pl.multiple_of(i * tq, tq)
    q = q_ref[0] * (DIFF_QK_DIM ** -0.5 * LOG2E)
    part = _iota((tq, LANES), 1) // DIFF_QK_DIM
    q2 = []
    for h in range(N_HEADS):
        qp = q[:, (h // 2) * LANES:(h // 2 + 1) * LANES]
        q2.append(jnp.concatenate([jnp.where(part == 2 * (h % 2) + c, qp, 0.0) for c in range(2)],
                                  axis=0).astype(BF16))
    t_cols = q0 + jnp.concatenate([_iota((1, tq), 1)] * 2, axis=1)

    def scores(h, kstart):
        lanes = slice((h // 2) * LANES, (h // 2 + 1) * LANES)
        return _nt(kb[pl.ds(kstart, tk), lanes], q2[h])

    def softmax(s, m):
        m_new = jnp.maximum(m, jnp.max(s, axis=0, keepdims=True))
        return m_new, jnp.exp2(s - m_new).astype(BF16), jnp.exp2(m - m_new)

    def accumulate(h, kstart, p, alpha, acc):
        return alpha * acc + _dot(vt[h * vrows:(h + 1) * vrows, pl.ds(kstart, tk)], p)

    def step(kstart, carry, causal):
        kstart = pl.multiple_of(kstart, tk)
        ms, accs = list(carry[0]), list(carry[1])
        s, p, alpha = [None] * N_HEADS, [None] * N_HEADS, [None] * N_HEADS
        kpos = kstart + _iota((tk, 1), 0)
        s[0] = scores(0, kstart)
        for h in range(N_HEADS):
            if h + 1 < N_HEADS:
                s[h + 1] = scores(h + 1, kstart)
            sh = jnp.where(kpos <= t_cols, s[h], NEG_INF) if causal else s[h]
            ms[h], p[h], alpha[h] = softmax(sh, ms[h])
            if h >= 1:
                accs[h - 1] = accumulate(h - 1, kstart, p[h - 1], alpha[h - 1], accs[h - 1])
        accs[-1] = accumulate(N_HEADS - 1, kstart, p[-1], alpha[-1], accs[-1])
        return tuple(ms), tuple(accs)

    n_full = q0 // tk
    init = (tuple(jnp.full((1, cols), NEG_INF, F32) for _ in range(N_HEADS)),
            tuple(jnp.zeros((vrows, cols), F32) for _ in range(N_HEADS)))
    carry = lax.fori_loop(0, n_full, lambda c, cr: step(c * tk, cr, False), init)
    _, accs = step(n_full * tk, carry, True)

    ng = ng_ref[...]
    outs = []
    for acc in accs:
        inv = 1.0 / acc[HEAD_DIM:HEAD_DIM + 1, :]
        o1 = acc[:HEAD_DIM, 0:tq] * inv[:, 0:tq]
        o2 = acc[:HEAD_DIM, tq:2 * tq] * inv[:, tq:2 * tq]
        o = o1 - lam_full * o2
        ms = jnp.mean(o * o, axis=0, keepdims=True)
        outs.append(o * lax.rsqrt(ms + EPS) * ng * (1.0 - lambda_init))
    y = jnp.concatenate(
        [jnp.concatenate(outs[0:2], axis=0).T, jnp.concatenate(outs[2:4], axis=0).T], axis=1)
    o_ref[0] = (y * _silu(z_ref[0])).astype(o_ref.dtype)


def _diff_attention(proj, lam, norm_g, layer_idx, *, tq=256, tk=512):
    bsz, seq, _ = proj.shape
    tk = min(tk, seq)
    lambda_init = 0.8 - 0.6 * math.exp(-0.3 * layer_idx)
    ng = jnp.broadcast_to(norm_g.reshape(HEAD_DIM, 1), (HEAD_DIM, tq)).astype(F32)
    kern = functools.partial(_diff_kernel, seq=seq, tq=tq, tk=tk, lambda_init=lambda_init)
    return pl.pallas_call(
        kern,
        grid=(bsz, seq // tq),
        in_specs=[pl.BlockSpec((1, tq, D_GROUP), lambda b, i: (b, i, PB_B_Q // 2)),
                  pl.BlockSpec((1, seq, D_GROUP), lambda b, i: (b, 0, PB_B_K // 2)),
                  pl.BlockSpec((1, seq, D_GROUP), lambda b, i: (b, 0, PB_B_V // 2)),
                  pl.BlockSpec((1, tq, D_GROUP), lambda b, i: (b, i, PB_B_Z // 2)),
                  pl.BlockSpec((4, DIFF_QK_DIM), lambda b, i: (0, 0)),
                  pl.BlockSpec((HEAD_DIM, tq), lambda b, i: (0, 0))],
        out_specs=pl.BlockSpec((1, tq, D_GROUP), lambda b, i: (b, i, 0)),
        out_shape=jax.ShapeDtypeStruct((bsz, seq, D_GROUP), BF16),
        scratch_shapes=[pltpu.VMEM((seq, D_GROUP), BF16),
                        pltpu.VMEM((N_HEADS * (HEAD_DIM + SUM_ROWS), seq), BF16)],
        compiler_params=_cparams(("arbitrary", "arbitrary")),
        name="diff_attention",
    )(proj, proj, proj, proj, lam, ng)


def _causal_conv_silu(x_raw, xext, cw_ref, cb_ref, first):
    n = x_raw.shape[0]

    @pl.when(first)
    def _():
        xext[0:CONV_PAD, :] = jnp.zeros((CONV_PAD, x_raw.shape[1]), F32)

    xext[CONV_PAD:CONV_PAD + n, :] = x_raw
    cw = cw_ref[...]
    acc = cb_ref[...] + cw[0:1] * xext[pl.ds(CONV_PAD - 3, n), :]
    for k in range(1, CONV_WIDTH):
        acc = acc + cw[k:k + 1] * xext[pl.ds(CONV_PAD - 3 + k, n), :]
    xext[0:CONV_PAD, :] = x_raw[n - CONV_PAD:, :]
    return _silu(acc)


def _tri(n):
    return jnp.where(_iota((n, n), 0) >= _iota((n, n), 1), 1.0, 0.0).astype(BF16)


def _expand_mat(first_lane, width):
    r = _iota((LANES, width), 0)
    c = _iota((LANES, width), 1)
    return jnp.where(r - first_lane == c // HEAD_DIM, 1.0, 0.0).astype(BF16)


def _row_select(width):
    r = _iota((8, width), 0)
    c = _iota((8, width), 1)
    return jnp.where(c == r * HEAD_DIM, 1.0, 0.0).astype(BF16)


def _group_sum_mat(width, group):
    r = _iota((width, width), 0)
    c = _iota((width, width), 1)
    return jnp.where(r // group == c // group, 1.0, 0.0).astype(BF16)


def _ssd_kernel(z_ref, xbc_ref, sm_ref, cw_ref, cb_ref, dtb_ref, alog_ref, dsk_ref, ng_ref, o_ref,
                xext, st):
    ci = pl.program_id(1)
    n = CHUNK

    @pl.when(ci == 0)
    def _():
        st[...] = jnp.zeros(st.shape, F32)

    xc = _causal_conv_silu(xbc_ref[0], xext, cw_ref, cb_ref, ci == 0)
    xs = xc[:, 0:D_GROUP]
    bm = xc[:, D_GROUP:D_GROUP + LANES]
    cm = xc[:, D_GROUP + LANES:D_GROUP + 2 * LANES]

    dt_c = _softplus(sm_ref[0] + dtb_ref[...])
    dtx = _dot_lhs_f32(dt_c, _expand_mat(SM_DT, D_GROUP))
    a_x = -jnp.exp(alog_ref[...])
    tri = _tri(n)
    acs = _dot_rhs_f32(tri, dtx * a_x)
    acs_rows = _nt_rhs_f32(_row_select(D_GROUP), acs)
    acs_last = acs[n - 1:n, :]

    xdt = xs * dtx
    xdt_b = xdt.astype(BF16)
    bm_b = bm.astype(BF16)
    cm_b = cm.astype(BF16)
    lane = _iota((n, LANES), 1)
    causal = _iota((n, n), 0) >= _iota((n, n), 1)

    y_pairs = []
    for g in range(SSM_GROUPS):
        in_group = (lane // SSM_STATE) == g
        gmat = _nt(jnp.where(in_group, cm, 0.0).astype(BF16), bm_b)
        pair = []
        for h in (2 * g, 2 * g + 1):
            col = acs[:, HEAD_DIM * h:HEAD_DIM * h + 1]
            row = acs_rows[h:h + 1, :]
            decay = jnp.exp(jnp.where(causal, col - row, NEG_INF))
            pair.append(_dot((gmat * decay).astype(BF16), xdt_b[:, g * LANES:(g + 1) * LANES]))
        y_pairs.append(jnp.where(lane < HEAD_DIM, pair[0], pair[1]))
    y_diag = jnp.concatenate(y_pairs, axis=1)

    state = st[...]
    y_off = jnp.exp(acs) * _dot(cm_b, state.astype(BF16))
    decay_end = jnp.exp(acs_last - acs)
    upd = _dot(bm.T.astype(BF16), (decay_end * xdt).astype(BF16))
    own = (_iota(st.shape, 0) // SSM_STATE) == (_iota(st.shape, 1) // LANES)
    st[...] = jnp.where(own, jnp.exp(acs_last) * state + upd, 0.0)

    y = (y_diag + y_off + dsk_ref[...] * xs) * _silu(z_ref[0])
    ms = _dot_lhs_f32(y * y, _group_sum_mat(D_GROUP, LANES)) * (1.0 / LANES)
    o_ref[0] = (y * lax.rsqrt(ms + EPS) * ng_ref[...]).astype(o_ref.dtype)


def _ssd_mixer(proj, conv_w, conv_b, dt_bias, a_log, d_skip, norm_g):
    bsz, seq, _ = proj.shape
    n = CHUNK
    dtb = jnp.zeros((1, LANES), F32).at[0, SM_DT:SM_DT + N_HEADS].set(dt_bias)
    alog_x = jnp.repeat(a_log, HEAD_DIM).reshape(1, D_GROUP)
    dsk_x = jnp.repeat(d_skip, HEAD_DIM).reshape(1, D_GROUP)
    full = lambda b, c: (0, 0)
    return pl.pallas_call(
        _ssd_kernel,
        grid=(bsz, seq // n),
        in_specs=[pl.BlockSpec((1, n, D_GROUP), lambda b, c: (b, c, PB_C_Z // 2)),
                  pl.BlockSpec((1, n, SSM_XBC), lambda b, c: (b, c, PB_C_XBC * LANES // SSM_XBC)),
                  pl.BlockSpec((1, n, LANES), lambda b, c: (b, c, PB_SMALL)),
                  pl.BlockSpec((CONV_WIDTH, SSM_XBC), full),
                  pl.BlockSpec((1, SSM_XBC), full),
                  pl.BlockSpec((1, LANES), full),
                  pl.BlockSpec((1, D_GROUP), full),
                  pl.BlockSpec((1, D_GROUP), full),
                  pl.BlockSpec((1, D_GROUP), full)],
        out_specs=pl.BlockSpec((1, n, D_GROUP), lambda b, c: (b, c, 0)),
        out_shape=jax.ShapeDtypeStruct((bsz, seq, D_GROUP), BF16),
        scratch_shapes=[pltpu.VMEM((CONV_PAD + n, SSM_XBC), F32), pltpu.VMEM((LANES, D_GROUP), F32)],
        compiler_params=_cparams(("arbitrary", "arbitrary")),
        name="ssd_mixer",
    )(proj, proj, proj, conv_w, conv_b.reshape(1, SSM_XBC), dtb, alog_x, dsk_x, norm_g.reshape(1, D_GROUP))


def _mlstm_kernel(qk_ref, v_ref, sm_ref, og_ref, z_ref, cw_ref, cb_ref, ifb_ref, ng_ref, o_ref,
                  xext, c_st, n_st, m_st):
    ci = pl.program_id(1)
    n = CHUNK

    @pl.when(ci == 0)
    def _():
        c_st[...] = jnp.zeros(c_st.shape, F32)
        n_st[...] = jnp.zeros(n_st.shape, F32)
        m_st[...] = jnp.zeros(m_st.shape, F32)

    qk = _causal_conv_silu(qk_ref[0], xext, cw_ref, cb_ref, ci == 0)
    q = qk[:, 0:D_GROUP]
    k = qk[:, D_GROUP:] * (HEAD_DIM ** -0.5)
    v = v_ref[0]
    q_b = q.astype(BF16)
    k_b = k.astype(BF16)
    v_b = v.astype(BF16)

    pre = sm_ref[0] + ifb_ref[...]
    logf = -_softplus(-pre)
    ig = _dot_lhs_f32(pre, _expand_mat(SM_I, D_GROUP))
    lf = _dot_lhs_f32(logf, _expand_mat(SM_F, D_GROUP))
    tri = _tri(n)
    b = _dot_rhs_f32(tri, lf)
    b_last = b[n - 1:n, :]
    sel = _row_select(D_GROUP)
    u_rows = _nt_rhs_f32(sel, ig - b)

    m_prev = m_st[...]
    c_prev = c_st[...]
    n_prev = n_st[...]

    lane = _iota((n, LANES), 1)
    causal = _iota((n, n), 0) >= _iota((n, n), 1)
    ones_b = jnp.ones((n, LANES), BF16)
    num_pairs, den_pairs, mt_pairs, wi_pairs = [], [], [], []
    for pr in range(N_HEADS // 2):
        lanes = slice(pr * LANES, (pr + 1) * LANES)
        qp = q[:, lanes]
        kp_b = k_b[:, lanes]
        rhs = jnp.concatenate([v_b[:, lanes], ones_b], axis=1)
        res, mts, wis = [], [], []
        for hh in range(2):
            h = 2 * pr + hh
            in_head = (lane // HEAD_DIM) == hh
            bcol = b[:, HEAD_DIM * h:HEAD_DIM * h + 1]
            dlog = jnp.where(causal, bcol + u_rows[h:h + 1, :], NEG_INF)
            inter = bcol + m_prev[:, HEAD_DIM * h:HEAD_DIM * h + 1]
            m_t = jnp.maximum(inter, jnp.max(dlog, axis=1, keepdims=True))
            s_qk = _nt(jnp.where(in_head, qp, 0.0).astype(BF16), kp_b) * jnp.exp(dlog - m_t)
            res.append(_dot(s_qk.astype(BF16), rhs))
            mts.append(jnp.broadcast_to(m_t, (n, LANES)))
            wis.append(jnp.broadcast_to(jnp.exp(inter - m_t), (n, LANES)))
        first = lane < HEAD_DIM
        num_pairs.append(jnp.where(first, res[0][:, :LANES], res[1][:, :LANES]))
        den_pairs.append(jnp.where(first, res[0][:, LANES:], res[1][:, LANES:]))
        mt_pairs.append(jnp.where(first, mts[0], mts[1]))
        wi_pairs.append(jnp.where(first, wis[0], wis[1]))
    num_intra = jnp.concatenate(num_pairs, axis=1)
    den_intra = jnp.concatenate(den_pairs, axis=1)
    m_t = jnp.concatenate(mt_pairs, axis=1)
    w_inter = jnp.concatenate(wi_pairs, axis=1)

    head_sum = _group_sum_mat(D_GROUP, HEAD_DIM)
    num = num_intra + w_inter * _dot(q_b, c_prev.astype(BF16))
    den = den_intra + w_inter * _dot_lhs_f32(q * n_prev, head_sum)
    hid = num / jnp.maximum(jnp.abs(den), jnp.exp(-m_t))

    g_end = b_last - b + ig
    m_loc = jnp.max(g_end, axis=0, keepdims=True)
    m_new = jnp.maximum(b_last + m_prev, m_loc)
    a_prev = jnp.exp(b_last + m_prev - m_new)
    a_loc = jnp.exp(m_loc - m_new)
    kw = k * (jnp.exp(g_end - m_loc) * a_loc)
    kw_t = jnp.concatenate([kw[:, :LANES].T, kw[:, LANES:].T], axis=0)
    upd = _dot(kw_t.astype(BF16), v_b)
    own = (_iota(c_st.shape, 0) // HEAD_DIM) == (_iota(c_st.shape, 1) // HEAD_DIM)
    c_st[...] = jnp.where(own, a_prev * c_prev + upd, 0.0)
    n_st[...] = a_prev * n_prev + jnp.sum(kw, axis=0, keepdims=True)
    m_st[...] = m_new

    hm = _sigmoid(og_ref[0]) * hid
    ms = _dot_lhs_f32(hm * hm, head_sum) * (1.0 / HEAD_DIM)
    o_ref[0] = (hm * lax.rsqrt(ms + EPS) * ng_ref[...] * _silu(z_ref[0])).astype(o_ref.dtype)


def _mlstm_mixer(proj, conv_w, conv_b, if_b, norm_g):
    bsz, seq, _ = proj.shape
    n = CHUNK
    ifb = jnp.zeros((1, LANES), F32).at[0, SM_I:SM_I + 2 * N_HEADS].set(if_b)
    ng = jnp.tile(norm_g, N_HEADS).reshape(1, D_GROUP)
    full = lambda b, c: (0, 0)
    return pl.pallas_call(
        _mlstm_kernel,
        grid=(bsz, seq // n),
        in_specs=[pl.BlockSpec((1, n, 2 * D_GROUP), lambda b, c: (b, c, PB_D_QK * LANES // (2 * D_GROUP))),
                  pl.BlockSpec((1, n, D_GROUP), lambda b, c: (b, c, PB_D_V // 2)),
                  pl.BlockSpec((1, n, LANES), lambda b, c: (b, c, PB_SMALL)),
                  pl.BlockSpec((1, n, D_GROUP), lambda b, c: (b, c, PB_D_O // 2)),
                  pl.BlockSpec((1, n, D_GROUP), lambda b, c: (b, c, PB_D_Z // 2)),
                  pl.BlockSpec((CONV_WIDTH, 2 * D_GROUP), full),
                  pl.BlockSpec((1, 2 * D_GROUP), full),
                  pl.BlockSpec((1, LANES), full),
                  pl.BlockSpec((1, D_GROUP), full)],
        out_specs=pl.BlockSpec((1, n, D_GROUP), lambda b, c: (b, c, 0)),
        out_shape=jax.ShapeDtypeStruct((bsz, seq, D_GROUP), BF16),
        scratch_shapes=[pltpu.VMEM((CONV_PAD + n, 2 * D_GROUP), F32),
                        pltpu.VMEM((D_GROUP, D_GROUP), F32),
                        pltpu.VMEM((1, D_GROUP), F32),
                        pltpu.VMEM((1, D_GROUP), F32)],
        compiler_params=_cparams(("arbitrary", "arbitrary")),
        name="mlstm_mixer",
    )(proj, proj, proj, proj, proj, conv_w, conv_b.reshape(1, 2 * D_GROUP), ifb, ng)


def _outproj_kernel(ya_ref, yb_ref, yc_ref, yd_ref, x_ref, gate_ref, w_ref, fg_ref, o_ref, *, final):
    acc = _dot(ya_ref[0], w_ref[0:D_GROUP, :])
    acc = acc + _dot(yb_ref[0], w_ref[D_GROUP:2 * D_GROUP, :])
    acc = acc + _dot(yc_ref[0], w_ref[2 * D_GROUP:3 * D_GROUP, :])
    acc = acc + _dot(yd_ref[0], w_ref[3 * D_GROUP:4 * D_GROUP, :])
    out = x_ref[0] + gate_ref[0] * acc
    if final:
        ms = jnp.mean(out * out, axis=-1, keepdims=True)
        out = out * lax.rsqrt(ms + EPS) * fg_ref[...]
    o_ref[0] = out


def _out_projection(ys, x, gate, w_bf, final_g, final):
    bsz, seq, d = x.shape
    tm = min(512, seq)
    yspec = pl.BlockSpec((1, tm, D_GROUP), lambda b, i: (b, i, 0))
    return pl.pallas_call(
        functools.partial(_outproj_kernel, final=final),
        grid=(bsz, seq // tm),
        in_specs=[yspec, yspec, yspec, yspec,
                  pl.BlockSpec((1, tm, d), lambda b, i: (b, i, 0)),
                  pl.BlockSpec((1, 1, d), lambda b, i: (b, 0, 0)),
                  pl.BlockSpec((N_HEADS * D_GROUP, d), lambda b, i: (0, 0)),
                  pl.BlockSpec((1, d), lambda b, i: (0, 0))],
        out_specs=pl.BlockSpec((1, tm, d), lambda b, i: (b, i, 0)),
        out_shape=jax.ShapeDtypeStruct((bsz, seq, d), F32),
        compiler_params=_cparams(("arbitrary", "arbitrary")),
        name="out_projection",
    )(*ys, x, gate, w_bf, final_g.reshape(1, d))


def _cmp_to_sel_t(seq):
    n_cmp = (seq - NSA_CMP_BLOCK) // NSA_CMP_STRIDE + 1
    n_sel = seq // NSA_SEL_BLOCK
    start = np.arange(n_cmp)[:, None] * NSA_CMP_STRIDE
    sel_start = np.arange(n_sel)[None, :] * NSA_SEL_BLOCK
    overlap = np.clip(np.minimum(start + NSA_CMP_BLOCK, sel_start + NSA_SEL_BLOCK)
                      - np.maximum(start, sel_start), 0, None)
    out = np.zeros((n_sel, seq // NSA_CMP_STRIDE), np.float32)
    out[:, :n_cmp] = (overlap / NSA_CMP_BLOCK).T
    return jnp.asarray(out, BF16)


def _mixers(proj, layer_idx, p):
    bsz, seq, _ = proj.shape
    n_rows = seq // NSA_CMP_STRIDE
    width = NSA_CMP_STRIDE * HEAD_DIM
    c0 = PB_A_KCVC * LANES
    rk = proj[:, :, c0:c0 + HEAD_DIM].reshape(bsz, n_rows, width)
    rv = proj[:, :, c0 + HEAD_DIM:c0 + 2 * HEAD_DIM].reshape(bsz, n_rows, width)
    pos = p['nsa_cmp_pos'].reshape(1, NSA_CMP_BLOCK * HEAD_DIM)
    ckv, cvt = _nsa_compress(rk, rv, pos, p['nsa_ck_w1'], p['nsa_ck_w2'], p['nsa_cv_w1'], p['nsa_cv_w2'])
    y_a = _nsa_attention(proj, ckv, cvt, _cmp_to_sel_t(seq), p['nsa_norm_g'])
    y_b = _diff_attention(proj, p['diff_lam'], p['diff_norm_g'], layer_idx)
    y_c = _ssd_mixer(proj, p['ssm_conv_w'], p['ssm_conv_b'], p['ssm_dt_bias'], p['ssm_a_log'],
                     p['ssm_d'], p['ssm_norm_g'])
    y_d = _mlstm_mixer(proj, p['ml_conv_w'], p['ml_conv_b'], p['ml_if_b'], p['ml_norm_g'])
    return y_a, y_b, y_c, y_d


_LAYER_PARAMS = ('nsa_cmp_pos', 'nsa_ck_w1', 'nsa_ck_w2', 'nsa_cv_w1', 'nsa_cv_w2', 'nsa_norm_g',
                 'diff_lam', 'diff_norm_g', 'ssm_conv_w', 'ssm_conv_b', 'ssm_dt_bias', 'ssm_a_log',
                 'ssm_d', 'ssm_norm_g', 'ml_conv_w', 'ml_conv_b', 'ml_if_b', 'ml_norm_g')


def kernel(x, c, norm_g, ada_w, ada_b, w_in, w_out, nsa_cmp_pos, nsa_ck_w1, nsa_ck_w2, nsa_cv_w1, nsa_cv_w2, nsa_norm_g, diff_lam, diff_norm_g, ssm_conv_w, ssm_conv_b, ssm_dt_bias, ssm_a_log, ssm_d, ssm_norm_g, ml_conv_w, ml_conv_b, ml_if_b, ml_norm_g, final_g):
    stacked = dict(nsa_cmp_pos=nsa_cmp_pos, nsa_ck_w1=nsa_ck_w1, nsa_ck_w2=nsa_ck_w2, nsa_cv_w1=nsa_cv_w1,
                   nsa_cv_w2=nsa_cv_w2, nsa_norm_g=nsa_norm_g, diff_lam=diff_lam, diff_norm_g=diff_norm_g,
                   ssm_conv_w=ssm_conv_w, ssm_conv_b=ssm_conv_b, ssm_dt_bias=ssm_dt_bias,
                   ssm_a_log=ssm_a_log, ssm_d=ssm_d, ssm_norm_g=ssm_norm_g, ml_conv_w=ml_conv_w,
                   ml_conv_b=ml_conv_b, ml_if_b=ml_if_b, ml_norm_g=ml_norm_g)
    depth = w_in.shape[0]
    bsz, seq, d = x.shape
    mod = _ada_modulation(c, ada_w, ada_b)
    for l in range(depth):
        p = {name: stacked[name][l] for name in _LAYER_PARAMS}
        shift = mod[l, :, 0:d].reshape(bsz, 1, d)
        scale = mod[l, :, d:2 * d].reshape(bsz, 1, d)
        gate = mod[l, :, 2 * d:3 * d].reshape(bsz, 1, d)
        w_bf = _relayout_w_in(w_in[l]).astype(BF16)
        proj = _in_projection(x, norm_g[l], scale, shift, w_bf)
        ys = _mixers(proj, l, p)
        x = _out_projection(ys, x, gate, w_out[l].astype(BF16), final_g, final=(l == depth - 1))
    return x
```

```python
import functools
import math

import numpy as np
import jax
import jax.numpy as jnp
from jax import lax
from jax.experimental import pallas as pl
from jax.experimental.pallas import tpu as pltpu

F32 = jnp.float32
BF16 = jnp.bfloat16

D_MODEL = 1024
N_HEADS = 4
HEAD_DIM = 64
D_GROUP = N_HEADS * HEAD_DIM
NEG_INF = -1e30
EPS = 1e-6

NSA_CMP_BLOCK = 32
NSA_CMP_STRIDE = 16
NSA_SEL_BLOCK = 64
NSA_TOP_N = 16
NSA_WINDOW = 512
NSA_CMP_HIDDEN = 128
NSA_FORCED_SCORE = 1e4

DIFF_QK_DIM = HEAD_DIM // 2
SSM_STATE = 64
SSM_GROUPS = 2
SSM_XBC = D_GROUP + 2 * SSM_GROUPS * SSM_STATE
CONV_WIDTH = 4
CHUNK = 128
CONV_PAD = 8

LANES = 128
D_PROJ = 4096

_SPLITS = (
    D_GROUP, HEAD_DIM, HEAD_DIM, HEAD_DIM, HEAD_DIM, HEAD_DIM, HEAD_DIM, 3 * N_HEADS, D_GROUP,
    D_GROUP, D_GROUP, D_GROUP, D_GROUP,
    D_GROUP, SSM_XBC, N_HEADS,
    2 * D_GROUP, D_GROUP, 2 * N_HEADS, D_GROUP, D_GROUP,
)
_OFFS = [0] + [int(o) for o in np.cumsum(_SPLITS)]
(_A_Q, _A_KC, _A_VC, _A_KS, _A_VS, _A_KW, _A_VW, _A_G, _A_Z,
 _B_Q, _B_K, _B_V, _B_Z, _C_Z, _C_XBC, _C_DT, _D_QK, _D_V, _D_IF, _D_O, _D_Z) = range(21)

SM_GATE = 0
SM_DT = 12
SM_I = 16
SM_F = 20

PB_A_Q = 0
PB_A_KCVC = 2
PB_A_KSVS = 3
PB_A_KWVW = 4
PB_SMALL = 5
PB_A_Z = 6
PB_B_Q = 8
PB_B_K = 10
PB_B_V = 12
PB_B_Z = 14
PB_C_XBC = 16
PB_C_Z = 20
PB_D_V = 22
PB_D_QK = 24
PB_D_O = 28
PB_D_Z = 30

VMEM_LIMIT = 56 * 1024 * 1024
LOG2E = 1.4426950408889634
SUM_ROWS = 16


def _cparams(sem):
    return pltpu.CompilerParams(dimension_semantics=sem, vmem_limit_bytes=VMEM_LIMIT)


def _relayout_w_in(w_in):
    def col(i):
        return w_in[:, _OFFS[i]:_OFFS[i + 1]]
    d = w_in.shape[0]
    small = jnp.concatenate(
        [col(_A_G), col(_C_DT), col(_D_IF), jnp.zeros((d, LANES - 24), w_in.dtype)], axis=1)
    parts = [col(_A_Q), col(_A_KC), col(_A_VC), col(_A_KS), col(_A_VS), col(_A_KW), col(_A_VW),
             small, col(_A_Z),
             col(_B_Q), col(_B_K), col(_B_V), col(_B_Z),
             col(_C_XBC), col(_C_Z),
             col(_D_V), col(_D_QK), col(_D_O), col(_D_Z)]
    out = jnp.concatenate(parts, axis=1)
    assert out.shape[1] == D_PROJ
    return out


def _dot(a, b):
    return jnp.dot(a, b, preferred_element_type=F32)


def _nt(a, b):
    return lax.dot_general(a, b, (((1,), (1,)), ((), ())), preferred_element_type=F32)


def _split3(a):
    hi = a.astype(BF16)
    r1 = a - hi.astype(F32)
    mid = r1.astype(BF16)
    lo = (r1 - mid.astype(F32)).astype(BF16)
    return hi, mid, lo


def _dot_lhs_f32(a, b_exact, terms=3):
    hi, mid, lo = _split3(a)
    out = _dot(hi, b_exact) + _dot(mid, b_exact)
    return out + _dot(lo, b_exact) if terms == 3 else out


def _dot_rhs_f32(a_exact, b):
    hi, mid, lo = _split3(b)
    return _dot(a_exact, hi) + _dot(a_exact, mid) + _dot(a_exact, lo)


def _nt_rhs_f32(a_exact, b):
    hi, mid, lo = _split3(b)
    return _nt(a_exact, hi) + _nt(a_exact, mid) + _nt(a_exact, lo)


def _seqs_per_step(bsz):
    return math.gcd(bsz, 4)


def _sigmoid(x):
    return 1.0 / (1.0 + jnp.exp(-x))


def _silu(x):
    return x * _sigmoid(x)


def _softplus(x):
    return jnp.maximum(x, 0.0) + jnp.log(1.0 + jnp.exp(-jnp.abs(x)))


def _iota(shape, dim):
    return lax.broadcasted_iota(jnp.int32, shape, dim)


def _ada_kernel(c_ref, w_ref, b_ref, o_ref):
    ca = _silu(c_ref[...])
    w = w_ref[0]
    c_hi, c_mid, _ = _split3(ca)
    w_hi, w_mid, _ = _split3(w)
    acc = _dot(c_hi, w_hi) + _dot(c_hi, w_mid) + _dot(c_mid, w_hi)
    o_ref[0] = acc + b_ref[0]


def _ada_modulation(c, ada_w, ada_b):
    depth, d, d3 = ada_w.shape
    bsz = c.shape[0]
    nb = d3 // d
    return pl.pallas_call(
        _ada_kernel,
        grid=(depth, nb),
        in_specs=[pl.BlockSpec((bsz, d), lambda l, j: (0, 0)),
                  pl.BlockSpec((1, d, d), lambda l, j: (l, 0, j)),
                  pl.BlockSpec((1, 1, d), lambda l, j: (l, 0, j))],
        out_specs=pl.BlockSpec((1, bsz, d), lambda l, j: (l, 0, j)),
        out_shape=jax.ShapeDtypeStruct((depth, bsz, d3), F32),
        compiler_params=_cparams(("arbitrary", "arbitrary")),
        name="ada_modulation",
    )(c, ada_w, ada_b.reshape(depth, 1, d3))


def _inproj_kernel(x_ref, g_ref, sc_ref, sh_ref, w_ref, o_ref, small_ref, h_scr):
    first = pl.program_id(2) == 0

    @pl.when(first)
    def _():
        x = x_ref[0]
        ms = jnp.mean(x * x, axis=-1, keepdims=True)
        y = x * lax.rsqrt(ms + EPS) * g_ref[...]
        h = y * (1.0 + sc_ref[0]) + sh_ref[0]
        h_scr[...] = h.astype(BF16)

    acc = _dot(h_scr[...], w_ref[...])
    o_ref[0] = acc.astype(o_ref.dtype)

    @pl.when(first)
    def _():
        small_ref[0] = acc[:, PB_SMALL * LANES:(PB_SMALL + 1) * LANES]


def _in_projection(x, norm_g, scale, shift, w_bf):
    bsz, seq, d = x.shape
    tm = min(1024, seq)
    tn = 1024
    assert (PB_SMALL + 1) * LANES <= tn
    return pl.pallas_call(
        _inproj_kernel,
        grid=(bsz, seq // tm, D_PROJ // tn),
        in_specs=[pl.BlockSpec((1, tm, d), lambda b, i, j: (b, i, 0)),
                  pl.BlockSpec((1, d), lambda b, i, j: (0, 0)),
                  pl.BlockSpec((1, 1, d), lambda b, i, j: (b, 0, 0)),
                  pl.BlockSpec((1, 1, d), lambda b, i, j: (b, 0, 0)),
                  pl.BlockSpec((d, tn), lambda b, i, j: (0, j))],
        out_specs=[pl.BlockSpec((1, tm, tn), lambda b, i, j: (b, i, j)),
                   pl.BlockSpec((1, tm, LANES), lambda b, i, j: (b, i, 0))],
        out_shape=[jax.ShapeDtypeStruct((bsz, seq, D_PROJ), BF16),
                   jax.ShapeDtypeStruct((bsz, seq, LANES), F32)],
        scratch_shapes=[pltpu.VMEM((tm, d), BF16)],
        compiler_params=_cparams(("arbitrary", "arbitrary", "arbitrary")),
        name="in_projection",
    )(x, norm_g.reshape(1, d), scale, shift, w_bf)


def _compress_kernel(rk_ref, rv_ref, pos_ref, kw1_ref, kw2_ref, vw1_ref, vw2_ref, kv_ref, vt_ref):
    half = NSA_CMP_STRIDE * HEAD_DIM
    pos = jnp.broadcast_to(pos_ref[...], (8, 2 * half)).astype(BF16)

    def one(r_ref, w1_ref, w2_ref):
        r = r_ref[0].astype(BF16)
        w1 = w1_ref[...].astype(BF16)
        first = _dot(r, w1[:half])
        second = _dot(r, w1[half:])
        n = first.shape[0]
        second = pltpu.roll(second, n - 1, 0)
        bias = _dot(pos, w1)[0:1]
        hid = _silu(first + second + bias)
        return _dot(hid.astype(BF16), w2_ref[...].astype(BF16))

    kc = one(rk_ref, kw1_ref, kw2_ref)
    vc = one(rv_ref, vw1_ref, vw2_ref)
    kv = jnp.concatenate([kc, vc], axis=1)
    kv_ref[0] = kv.astype(BF16)
    ones = jnp.ones((SUM_ROWS, kv.shape[0]), BF16)
    vt_ref[0] = jnp.concatenate([kv.T[HEAD_DIM:].astype(BF16), ones], axis=0)


def _nsa_compress(rk, rv, pos, kw1, kw2, vw1, vw2):
    bsz, n, width = rk.shape
    full2 = lambda b: (0, 0)
    return pl.pallas_call(
        _compress_kernel,
        grid=(bsz,),
        in_specs=[pl.BlockSpec((1, n, width), lambda b: (b, 0, 0)),
                  pl.BlockSpec((1, n, width), lambda b: (b, 0, 0)),
                  pl.BlockSpec(pos.shape, full2),
                  pl.BlockSpec(kw1.shape, full2), pl.BlockSpec(kw2.shape, full2),
                  pl.BlockSpec(vw1.shape, full2), pl.BlockSpec(vw2.shape, full2)],
        out_specs=[pl.BlockSpec((1, n, LANES), lambda b: (b, 0, 0)),
                   pl.BlockSpec((1, HEAD_DIM + SUM_ROWS, n), lambda b: (b, 0, 0))],
        out_shape=[jax.ShapeDtypeStruct((bsz, n, LANES), BF16),
                   jax.ShapeDtypeStruct((bsz, HEAD_DIM + SUM_ROWS, n), BF16)],
        compiler_params=_cparams(("arbitrary",)),
        name="nsa_compress",
    )(rk, rv, pos, kw1, kw2, vw1, vw2)


def _online_softmax(s, m):
    m_new = jnp.maximum(m, jnp.max(s, axis=0, keepdims=True))
    return m_new, jnp.exp2(s - m_new).astype(BF16), jnp.exp2(m - m_new)


def _nsa_kernel(q_ref, ks_ref, kw_ref, sm_ref, z_ref, ckv_ref, cvt_ref, c2s_ref, ng_ref, o_ref,
                vst, vwt, sel_scr, imp_scr, ms_scr, as_scr, mw_scr, aw_scr, *, seq, tq, tk, top_n):
    i = pl.program_id(1)
    n_cmp = seq // NSA_CMP_STRIDE
    n_sel = seq // NSA_SEL_BLOCK
    span = NSA_WINDOW + tq

    @pl.when(i == 0)
    def _():
        ones = jnp.ones((SUM_ROWS, LANES), BF16)
        for blk in range(seq // LANES):
            rows = slice(blk * LANES, (blk + 1) * LANES)
            ks = ks_ref[0, rows, :].astype(F32)
            kw = kw_ref[0, rows, :].astype(F32)
            vst[:, rows] = jnp.concatenate([ks.T[HEAD_DIM:].astype(BF16), ones], axis=0)
            vwt[:, rows] = jnp.concatenate([kw.T[HEAD_DIM:].astype(BF16), ones], axis=0)

    q0 = pl.multiple_of(i * tq, tq)
    q = q_ref[0].astype(F32) * (HEAD_DIM ** -0.5 * LOG2E)
    low = _iota((tq, LANES), 1) < HEAD_DIM
    qh = []
    for pair in range(N_HEADS // 2):
        qp = q[:, pair * LANES:(pair + 1) * LANES]
        qh.append(jnp.where(low, qp, 0.0).astype(BF16))
        qh.append(jnp.where(low, pltpu.roll(qp, HEAD_DIM, 1), 0.0).astype(BF16))

    t_row = q0 + _iota((1, tq), 1)

    ckv = ckv_ref[0]
    cvt = cvt_ref[0]
    cmp_end = _iota((n_cmp, 1), 0) * NSA_CMP_STRIDE + (NSA_CMP_BLOCK - 1)
    valid_c = cmp_end <= t_row
    any_valid = t_row >= NSA_CMP_BLOCK - 1
    heads = range(N_HEADS)
    s_c = [jnp.where(valid_c, _nt(ckv, qh[h]), NEG_INF) for h in heads]
    m_c = [jnp.max(s_c[h], axis=0, keepdims=True) for h in heads]
    p_c = [jnp.exp2(s_c[h] - m_c[h]) for h in heads]
    pv_c = [_dot(cvt, p_c[h].astype(BF16)) for h in heads]
    inv_c = [jnp.where(any_valid, 1.0 / pv_c[h][HEAD_DIM:HEAD_DIM + 1], 0.0) for h in heads]
    o_c = [pv_c[h][:HEAD_DIM] * inv_c[h] for h in heads]
    p_sum = p_c[0] * inv_c[0]
    for h in range(1, N_HEADS):
        p_sum = p_sum + p_c[h] * inv_c[h]

    p_hi = p_sum.astype(BF16)
    p_lo = (p_sum - p_hi.astype(F32)).astype(BF16)
    c2s = c2s_ref[...]
    imp = _dot(c2s, p_hi) + _dot(c2s, p_lo)
    sid = _iota((n_sel, tq), 0)
    cur = t_row // NSA_SEL_BLOCK
    forced = (sid == cur) | (sid == 0)
    imp = jnp.where(forced, NSA_FORCED_SCORE, jnp.where(sid <= cur, imp, -1.0))
    imp_scr[...] = imp
    per_step = tq // NSA_SEL_BLOCK

    def rank_body(jj, rank):
        for u in range(per_step):
            j = jj * per_step + u
            vj = imp_scr[pl.ds(j, 1), :]
            ge = jnp.where(vj >= imp, 1.0, 0.0)
            gt = jnp.where(vj > imp, 1.0, 0.0)
            rank = rank + jnp.where(sid > j, ge, gt)
        return rank

    rank = lax.fori_loop(0, i + 1, rank_body, jnp.zeros((n_sel, tq), F32))
    sel_scr[...] = jnp.where(rank < float(top_n), 1.0, 0.0)

    blocks_per_chunk = tk // NSA_SEL_BLOCK

    def stream_step(scores_fn, vt_c, valid, m_scr, acc_scr):
        s, p, alpha = [None] * N_HEADS, [None] * N_HEADS, [None] * N_HEADS

        def update(h):
            acc_scr[h] = alpha[h] * acc_scr[h] + _dot(vt_c, p[h])

        s[0] = scores_fn(0)
        for h in range(N_HEADS):
            if h + 1 < N_HEADS:
                s[h + 1] = scores_fn(h + 1)
            m_scr[h], p[h], alpha[h] = _online_softmax(jnp.where(valid, s[h], NEG_INF), m_scr[h])
            if h >= 1:
                update(h - 1)
        update(N_HEADS - 1)

    def sel_step(kstart, causal):
        kstart = pl.multiple_of(kstart, tk)
        k_c = ks_ref[0, pl.ds(kstart, tk), :]
        sb = pl.multiple_of(kstart // NSA_SEL_BLOCK, blocks_per_chunk)
        selc = sel_scr[pl.ds(sb, blocks_per_chunk), :]
        rows = [jnp.broadcast_to(selc[r:r + 1, :], (NSA_SEL_BLOCK, tq)) for r in range(blocks_per_chunk)]
        valid = jnp.concatenate(rows, axis=0) > 0.5
        if causal:
            valid = valid & (kstart + _iota((tk, 1), 0) <= t_row)
        stream_step(lambda h: _nt(k_c, qh[h]), vst[:, pl.ds(kstart, tk)], valid, ms_scr, as_scr)

    for m_scr, acc_scr in ((ms_scr, as_scr), (mw_scr, aw_scr)):
        m_scr[...] = jnp.full(m_scr.shape, NEG_INF, F32)
        acc_scr[...] = jnp.zeros(acc_scr.shape, F32)
    n_full = q0 // tk

    def full_chunk(c, carry):
        sel_step(c * tk, False)
        return carry

    lax.fori_loop(0, n_full, full_chunk, 0)
    sel_step(n_full * tk, True)

    wstart = pl.multiple_of(jnp.maximum(q0 - NSA_WINDOW, 0), tq)
    k_w = kw_ref[0, pl.ds(wstart, span), :]
    kpos_w = wstart + _iota((span, 1), 0)
    valid_w = (kpos_w <= t_row) & (kpos_w > t_row - NSA_WINDOW)
    stream_step(lambda h: _nt(k_w, qh[h]), vwt[:, pl.ds(wstart, span)], valid_w, mw_scr, aw_scr)

    gates = _sigmoid(sm_ref[0].T[0:16])
    ng = ng_ref[...]
    outs = []
    for h in range(N_HEADS):
        g_c = gates[3 * h:3 * h + 1]
        g_s = gates[3 * h + 1:3 * h + 2]
        g_w = gates[3 * h + 2:3 * h + 3]
        acc_s, acc_w = as_scr[h], aw_scr[h]
        o_s = acc_s[:HEAD_DIM] * (1.0 / acc_s[HEAD_DIM:HEAD_DIM + 1])
        o_w = acc_w[:HEAD_DIM] * (1.0 / acc_w[HEAD_DIM:HEAD_DIM + 1])
        oh = g_c * o_c[h] + g_s * o_s + g_w * o_w
        ms = jnp.mean(oh * oh, axis=0, keepdims=True)
        outs.append(oh * lax.rsqrt(ms + EPS) * ng)
    y = jnp.concatenate(
        [jnp.concatenate(outs[0:2], axis=0).T, jnp.concatenate(outs[2:4], axis=0).T], axis=1)
    o_ref[0] = (y * _silu(z_ref[0].astype(F32))).astype(o_ref.dtype)


def _nsa_attention(proj, small, ckv, cvt, c2s_t, norm_g, *, tq=512, tk=512):
    bsz, seq, _ = proj.shape
    tk = min(tk, seq)
    n_cmp = seq // NSA_CMP_STRIDE
    n_sel = seq // NSA_SEL_BLOCK
    top_n = min(NSA_TOP_N, n_sel)
    ng = jnp.broadcast_to(norm_g.reshape(HEAD_DIM, 1), (HEAD_DIM, tq)).astype(F32)
    kern = functools.partial(_nsa_kernel, seq=seq, tq=tq, tk=tk, top_n=top_n)
    return pl.pallas_call(
        kern,
        grid=(bsz, seq // tq),
        in_specs=[pl.BlockSpec((1, tq, 2 * LANES), lambda b, i: (b, i, PB_A_Q // 2)),
                  pl.BlockSpec((1, seq, LANES), lambda b, i: (b, 0, PB_A_KSVS)),
                  pl.BlockSpec((1, seq, LANES), lambda b, i: (b, 0, PB_A_KWVW)),
                  pl.BlockSpec((1, tq, LANES), lambda b, i: (b, i, 0)),
                  pl.BlockSpec((1, tq, 2 * LANES), lambda b, i: (b, i, PB_A_Z // 2)),
                  pl.BlockSpec((1, n_cmp, LANES), lambda b, i: (b, 0, 0)),
                  pl.BlockSpec((1, HEAD_DIM + SUM_ROWS, n_cmp), lambda b, i: (b, 0, 0)),
                  pl.BlockSpec((n_sel, n_cmp), lambda b, i: (0, 0)),
                  pl.BlockSpec((HEAD_DIM, tq), lambda b, i: (0, 0))],
        out_specs=pl.BlockSpec((1, tq, D_GROUP), lambda b, i: (b, i, 0)),
        out_shape=jax.ShapeDtypeStruct((bsz, seq, D_GROUP), BF16),
        scratch_shapes=[pltpu.VMEM((HEAD_DIM + SUM_ROWS, seq), BF16),
                        pltpu.VMEM((HEAD_DIM + SUM_ROWS, seq), BF16),
                        pltpu.VMEM((n_sel, tq), F32), pltpu.VMEM((n_sel, tq), F32),
                        pltpu.VMEM((N_HEADS, 1, tq), F32),
                        pltpu.VMEM((N_HEADS, HEAD_DIM + SUM_ROWS, tq), F32),
                        pltpu.VMEM((N_HEADS, 1, tq), F32),
                        pltpu.VMEM((N_HEADS, HEAD_DIM + SUM_ROWS, tq), F32)],
        compiler_params=_cparams(("arbitrary", "arbitrary")),
        name="nsa_attention",
    )(proj, proj, proj, small, proj, ckv, cvt, c2s_t, ng)


def _diff_kernel(q_ref, k_ref, v_ref, z_ref, lam_ref, ng_ref, o_ref, vt, m_scr, acc_scr, *, seq, tq, tk,
                 lambda_init):
    i = pl.program_id(1)
    cols = 2 * tq
    vrows = HEAD_DIM + SUM_ROWS

    @pl.when(i == 0)
    def _():
        ones = jnp.ones((SUM_ROWS, LANES), BF16)
        for blk in range(seq // LANES):
            rows = slice(blk * LANES, (blk + 1) * LANES)
            parts = []
            for pair in range(N_HEADS // 2):
                v_t = v_ref[0, rows, pair * LANES:(pair + 1) * LANES].astype(F32).T.astype(BF16)
                parts += [v_t[:HEAD_DIM], ones, v_t[HEAD_DIM:], ones]
            vt[:, rows] = jnp.concatenate(parts, axis=0)

    lam = lam_ref[...]
    lam_full = (jnp.exp(jnp.sum(lam[0:1] * lam[1:2], axis=1, keepdims=True))
                - jnp.exp(jnp.sum(lam[2:3] * lam[3:4], axis=1, keepdims=True)) + lambda_init)

    q0 = pl.multiple_of(i * tq, tq)
    q = q_ref[0].astype(F32) * (DIFF_QK_DIM ** -0.5 * LOG2E)
    part = _iota((tq, LANES), 1) // DIFF_QK_DIM
    q2 = []
    for h in range(N_HEADS):
        qp = q[:, (h // 2) * LANES:(h // 2 + 1) * LANES]
        q2.append(jnp.concatenate([jnp.where(part == 2 * (h % 2) + c, qp, 0.0) for c in range(2)],
                                  axis=0).astype(BF16))
    t_cols = q0 + jnp.concatenate([_iota((1, tq), 1)] * 2, axis=1)

    def scores(h, kstart):
        lanes = slice((h // 2) * LANES, (h // 2 + 1) * LANES)
        return _nt(k_ref[0, pl.ds(kstart, tk), lanes], q2[h])

    def softmax(s, m):
        m_new = jnp.maximum(m, jnp.max(s, axis=0, keepdims=True))
        return m_new, jnp.exp2(s - m_new).astype(BF16), jnp.exp2(m - m_new)

    def accumulate(h, kstart, p, alpha, acc):
        return alpha * acc + _dot(vt[h * vrows:(h + 1) * vrows, pl.ds(kstart, tk)], p)

    def step(kstart, causal):
        kstart = pl.multiple_of(kstart, tk)
        s, p, alpha = [None] * N_HEADS, [None] * N_HEADS, [None] * N_HEADS
        kpos = kstart + _iota((tk, 1), 0)

        def update(h):
            acc_scr[h] = accumulate(h, kstart, p[h], alpha[h], acc_scr[h])

        s[0] = scores(0, kstart)
        for h in range(N_HEADS):
            if h + 1 < N_HEADS:
                s[h + 1] = scores(h + 1, kstart)
            sh = jnp.where(kpos <= t_cols, s[h], NEG_INF) if causal else s[h]
            m_scr[h], p[h], alpha[h] = softmax(sh, m_scr[h])
            if h >= 1:
                update(h - 1)
        update(N_HEADS - 1)

    m_scr[...] = jnp.full(m_scr.shape, NEG_INF, F32)
    acc_scr[...] = jnp.zeros(acc_scr.shape, F32)
    n_full = q0 // tk

    def full_chunk(c, carry):
        step(c * tk, False)
        return carry

    lax.fori_loop(0, n_full, full_chunk, 0)
    step(n_full * tk, True)

    ng = ng_ref[...]
    outs = []
    for h in range(N_HEADS):
        acc = acc_scr[h]
        inv = 1.0 / acc[HEAD_DIM:HEAD_DIM + 1, :]
        o1 = acc[:HEAD_DIM, 0:tq] * inv[:, 0:tq]
        o2 = acc[:HEAD_DIM, tq:2 * tq] * inv[:, tq:2 * tq]
        o = o1 - lam_full * o2
        ms = jnp.mean(o * o, axis=0, keepdims=True)
        outs.append(o * lax.rsqrt(ms + EPS) * ng * (1.0 - lambda_init))
    y = jnp.concatenate(
        [jnp.concatenate(outs[0:2], axis=0).T, jnp.concatenate(outs[2:4], axis=0).T], axis=1)
    o_ref[0] = (y * _silu(z_ref[0].astype(F32))).astype(o_ref.dtype)


def _diff_attention(proj, lam, norm_g, layer_idx, *, tq=512, tk=512):
    bsz, seq, _ = proj.shape
    tk = min(tk, seq)
    lambda_init = 0.8 - 0.6 * math.exp(-0.3 * layer_idx)
    ng = jnp.broadcast_to(norm_g.reshape(HEAD_DIM, 1), (HEAD_DIM, tq)).astype(F32)
    kern = functools.partial(_diff_kernel, seq=seq, tq=tq, tk=tk, lambda_init=lambda_init)
    return pl.pallas_call(
        kern,
        grid=(bsz, seq // tq),
        in_specs=[pl.BlockSpec((1, tq, D_GROUP), lambda b, i: (b, i, PB_B_Q // 2)),
                  pl.BlockSpec((1, seq, D_GROUP), lambda b, i: (b, 0, PB_B_K // 2)),
                  pl.BlockSpec((1, seq, D_GROUP), lambda b, i: (b, 0, PB_B_V // 2)),
                  pl.BlockSpec((1, tq, D_GROUP), lambda b, i: (b, i, PB_B_Z // 2)),
                  pl.BlockSpec((4, DIFF_QK_DIM), lambda b, i: (0, 0)),
                  pl.BlockSpec((HEAD_DIM, tq), lambda b, i: (0, 0))],
        out_specs=pl.BlockSpec((1, tq, D_GROUP), lambda b, i: (b, i, 0)),
        out_shape=jax.ShapeDtypeStruct((bsz, seq, D_GROUP), BF16),
        scratch_shapes=[pltpu.VMEM((N_HEADS * (HEAD_DIM + SUM_ROWS), seq), BF16),
                        pltpu.VMEM((N_HEADS, 1, 2 * tq), F32),
                        pltpu.VMEM((N_HEADS, HEAD_DIM + SUM_ROWS, 2 * tq), F32)],
        compiler_params=_cparams(("arbitrary", "arbitrary")),
        name="diff_attention",
    )(proj, proj, proj, proj, lam, ng)


def _causal_conv_silu(x_raw, xext, cw_ref, cb_ref):
    n = x_raw.shape[0]
    xext[CONV_PAD:CONV_PAD + n, :] = x_raw
    cw = cw_ref[...]
    acc = cb_ref[...] + cw[0:1] * xext[pl.ds(CONV_PAD - 3, n), :]
    for k in range(1, CONV_WIDTH):
        acc = acc + cw[k:k + 1] * xext[pl.ds(CONV_PAD - 3 + k, n), :]
    xext[0:CONV_PAD, :] = x_raw[n - CONV_PAD:, :]
    return _silu(acc)


def _interleave(chunks):
    live = list(chunks)
    while live:
        nxt = []
        for g in live:
            try:
                next(g)
                nxt.append(g)
            except StopIteration:
                pass
        live = nxt


def _tri(n):
    return jnp.where(_iota((n, n), 0) >= _iota((n, n), 1), 1.0, 0.0).astype(BF16)


def _expand_mat(first_lane, width):
    r = _iota((LANES, width), 0)
    c = _iota((LANES, width), 1)
    return jnp.where(r - first_lane == c // HEAD_DIM, 1.0, 0.0).astype(BF16)


def _row_select(width):
    r = _iota((8, width), 0)
    c = _iota((8, width), 1)
    return jnp.where(c == r * HEAD_DIM, 1.0, 0.0).astype(BF16)


def _group_sum_mat(width, group):
    r = _iota((width, width), 0)
    c = _iota((width, width), 1)
    return jnp.where(r // group == c // group, 1.0, 0.0).astype(BF16)


def _ssd_kernel(z_ref, xbc_ref, sm_ref, cw_ref, cb_ref, dtb_ref, alog_ref, dsk_ref, ng_ref, o_ref,
                xext, st, *, nb):
    @pl.when(pl.program_id(1) == 0)
    def _():
        st[...] = jnp.zeros(st.shape, F32)
        xext[:, 0:CONV_PAD, :] = jnp.zeros((nb, CONV_PAD, SSM_XBC), F32)

    _interleave(_ssd_chunk(z_ref.at[bb], xbc_ref.at[bb], sm_ref.at[bb], cw_ref, cb_ref, dtb_ref, alog_ref,
                           dsk_ref, ng_ref, o_ref.at[bb], xext.at[bb], st.at[bb]) for bb in range(nb))


def _ssd_chunk(z_ref, xbc_ref, sm_ref, cw_ref, cb_ref, dtb_ref, alog_ref, dsk_ref, ng_ref, o_ref, xext, st):
    n = CHUNK
    xc = _causal_conv_silu(xbc_ref[...].astype(F32), xext, cw_ref, cb_ref)
    xs = xc[:, 0:D_GROUP]
    bm = xc[:, D_GROUP:D_GROUP + LANES]
    cm = xc[:, D_GROUP + LANES:D_GROUP + 2 * LANES]
    yield

    dt_c = _softplus(sm_ref[...] + dtb_ref[...])
    dtx = _dot_lhs_f32(dt_c, _expand_mat(SM_DT, D_GROUP))
    a_x = -jnp.exp(alog_ref[...])
    tri = _tri(n)
    acs = _dot_rhs_f32(tri, dtx * a_x)
    acs_rows = _nt_rhs_f32(_row_select(D_GROUP), acs)
    acs_last = acs[n - 1:n, :]
    yield

    xdt = xs * dtx
    xdt_b = xdt.astype(BF16)
    bm_b = bm.astype(BF16)
    cm_b = cm.astype(BF16)
    lane = _iota((n, LANES), 1)
    causal = _iota((n, n), 0) >= _iota((n, n), 1)

    y_pairs = []
    for g in range(SSM_GROUPS):
        in_group = (lane // SSM_STATE) == g
        gmat = _nt(jnp.where(in_group, cm, 0.0).astype(BF16), bm_b)
        pair = []
        for h in (2 * g, 2 * g + 1):
            col = acs[:, HEAD_DIM * h:HEAD_DIM * h + 1]
            row = acs_rows[h:h + 1, :]
            decay = jnp.exp(jnp.where(causal, col - row, NEG_INF))
            pair.append(_dot((gmat * decay).astype(BF16), xdt_b[:, g * LANES:(g + 1) * LANES]))
        y_pairs.append(jnp.where(lane < HEAD_DIM, pair[0], pair[1]))
        yield
    y_diag = jnp.concatenate(y_pairs, axis=1)

    state = st[...]
    y_off = jnp.exp(acs) * _dot(cm_b, state.astype(BF16))
    decay_end = jnp.exp(acs_last - acs)
    upd = _dot(bm.T.astype(BF16), (decay_end * xdt).astype(BF16))
    own = (_iota(st.shape, 0) // SSM_STATE) == (_iota(st.shape, 1) // LANES)
    st[...] = jnp.where(own, jnp.exp(acs_last) * state + upd, 0.0)
    yield

    y = (y_diag + y_off + dsk_ref[...] * xs) * _silu(z_ref[...].astype(F32))
    ms = _dot_lhs_f32(y * y, _group_sum_mat(D_GROUP, LANES), terms=2) * (1.0 / LANES)
    o_ref[...] = (y * lax.rsqrt(ms + EPS) * ng_ref[...]).astype(o_ref.dtype)


def _ssd_mixer(proj, small, conv_w, conv_b, dt_bias, a_log, d_skip, norm_g):
    bsz, seq, _ = proj.shape
    n = CHUNK
    dtb = jnp.zeros((1, LANES), F32).at[0, SM_DT:SM_DT + N_HEADS].set(dt_bias)
    alog_x = jnp.repeat(a_log, HEAD_DIM).reshape(1, D_GROUP)
    dsk_x = jnp.repeat(d_skip, HEAD_DIM).reshape(1, D_GROUP)
    full = lambda b, c: (0, 0)
    nb = _seqs_per_step(bsz)
    return pl.pallas_call(
        functools.partial(_ssd_kernel, nb=nb),
        grid=(bsz // nb, seq // n),
        in_specs=[pl.BlockSpec((nb, n, D_GROUP), lambda b, c: (b, c, PB_C_Z // 2)),
                  pl.BlockSpec((nb, n, SSM_XBC), lambda b, c: (b, c, PB_C_XBC * LANES // SSM_XBC)),
                  pl.BlockSpec((nb, n, LANES), lambda b, c: (b, c, 0)),
                  pl.BlockSpec((CONV_WIDTH, SSM_XBC), full),
                  pl.BlockSpec((1, SSM_XBC), full),
                  pl.BlockSpec((1, LANES), full),
                  pl.BlockSpec((1, D_GROUP), full),
                  pl.BlockSpec((1, D_GROUP), full),
                  pl.BlockSpec((1, D_GROUP), full)],
        out_specs=pl.BlockSpec((nb, n, D_GROUP), lambda b, c: (b, c, 0)),
        out_shape=jax.ShapeDtypeStruct((bsz, seq, D_GROUP), BF16),
        scratch_shapes=[pltpu.VMEM((nb, CONV_PAD + n, SSM_XBC), F32), pltpu.VMEM((nb, LANES, D_GROUP), F32)],
        compiler_params=_cparams(("arbitrary", "arbitrary")),
        name="ssd_mixer",
    )(proj, proj, small, conv_w, conv_b.reshape(1, SSM_XBC), dtb, alog_x, dsk_x, norm_g.reshape(1, D_GROUP))


def _mlstm_kernel(qk_ref, v_ref, sm_ref, og_ref, z_ref, cw_ref, cb_ref, ifb_ref, ng_ref, o_ref,
                  xext, c_st, n_st, m_st, *, nb):
    @pl.when(pl.program_id(1) == 0)
    def _():
        c_st[...] = jnp.zeros(c_st.shape, F32)
        n_st[...] = jnp.zeros(n_st.shape, F32)
        m_st[...] = jnp.zeros(m_st.shape, F32)
        xext[:, 0:CONV_PAD, :] = jnp.zeros((nb, CONV_PAD, 2 * D_GROUP), F32)

    _interleave(_mlstm_chunk(qk_ref.at[bb], v_ref.at[bb], sm_ref.at[bb], og_ref.at[bb], z_ref.at[bb], cw_ref,
                             cb_ref, ifb_ref, ng_ref, o_ref.at[bb], xext.at[bb], c_st.at[bb], n_st.at[bb],
                             m_st.at[bb]) for bb in range(nb))


def _mlstm_chunk(qk_ref, v_ref, sm_ref, og_ref, z_ref, cw_ref, cb_ref, ifb_ref, ng_ref, o_ref,
                 xext, c_st, n_st, m_st):
    n = CHUNK
    qk = _causal_conv_silu(qk_ref[...].astype(F32), xext, cw_ref, cb_ref)
    q = qk[:, 0:D_GROUP]
    k = qk[:, D_GROUP:] * (HEAD_DIM ** -0.5)
    q_b = q.astype(BF16)
    k_b = k.astype(BF16)
    v_b = v_ref[...]
    yield

    pre = sm_ref[...] + ifb_ref[...]
    logf = -_softplus(-pre)
    ig = _dot_lhs_f32(pre, _expand_mat(SM_I, D_GROUP))
    lf = _dot_lhs_f32(logf, _expand_mat(SM_F, D_GROUP))
    tri = _tri(n)
    b = _dot_rhs_f32(tri, lf)
    b_last = b[n - 1:n, :]
    sel = _row_select(D_GROUP)
    u_rows = _nt_rhs_f32(sel, ig - b)
    yield

    m_prev = m_st[...]
    c_prev = c_st[...]
    n_prev = n_st[...]

    lane = _iota((n, LANES), 1)
    causal = _iota((n, n), 0) >= _iota((n, n), 1)
    ones_b = jnp.ones((n, LANES), BF16)
    num_pairs, den_pairs, mt_pairs, wi_pairs = [], [], [], []
    for pr in range(N_HEADS // 2):
        lanes = slice(pr * LANES, (pr + 1) * LANES)
        qp = q[:, lanes]
        kp_b = k_b[:, lanes]
        rhs = jnp.concatenate([v_b[:, lanes], ones_b], axis=1)
        res, mts, wis = [], [], []
        for hh in range(2):
            h = 2 * pr + hh
            in_head = (lane // HEAD_DIM) == hh
            bcol = b[:, HEAD_DIM * h:HEAD_DIM * h + 1]
            dlog = jnp.where(causal, bcol + u_rows[h:h + 1, :], NEG_INF)
            inter = bcol + m_prev[:, HEAD_DIM * h:HEAD_DIM * h + 1]
            m_t = jnp.maximum(inter, jnp.max(dlog, axis=1, keepdims=True))
            s_qk = _nt(jnp.where(in_head, qp, 0.0).astype(BF16), kp_b) * jnp.exp(dlog - m_t)
            res.append(_dot(s_qk.astype(BF16), rhs))
            mts.append(jnp.broadcast_to(m_t, (n, LANES)))
            wis.append(jnp.broadcast_to(jnp.exp(inter - m_t), (n, LANES)))
            yield
        first = lane < HEAD_DIM
        num_pairs.append(jnp.where(first, res[0][:, :LANES], res[1][:, :LANES]))
        den_pairs.append(jnp.where(first, res[0][:, LANES:], res[1][:, LANES:]))
        mt_pairs.append(jnp.where(first, mts[0], mts[1]))
        wi_pairs.append(jnp.where(first, wis[0], wis[1]))
    num_intra = jnp.concatenate(num_pairs, axis=1)
    den_intra = jnp.concatenate(den_pairs, axis=1)
    m_t = jnp.concatenate(mt_pairs, axis=1)
    w_inter = jnp.concatenate(wi_pairs, axis=1)

    head_sum = _group_sum_mat(D_GROUP, HEAD_DIM)
    num = num_intra + w_inter * _dot(q_b, c_prev.astype(BF16))
    den = den_intra + w_inter * _dot_lhs_f32(q * n_prev, head_sum, terms=2)
    hid = num / jnp.maximum(jnp.abs(den), jnp.exp(-m_t))
    yield

    g_end = b_last - b + ig
    m_loc = jnp.max(g_end, axis=0, keepdims=True)
    m_new = jnp.maximum(b_last + m_prev, m_loc)
    a_prev = jnp.exp(b_last + m_prev - m_new)
    a_loc = jnp.exp(m_loc - m_new)
    kw = k * (jnp.exp(g_end - m_loc) * a_loc)
    kw_t = jnp.concatenate([kw[:, :LANES].T, kw[:, LANES:].T], axis=0)
    upd = _dot(kw_t.astype(BF16), v_b)
    own = (_iota(c_st.shape, 0) // HEAD_DIM) == (_iota(c_st.shape, 1) // HEAD_DIM)
    c_st[...] = jnp.where(own, a_prev * c_prev + upd, 0.0)
    n_st[...] = a_prev * n_prev + jnp.sum(kw, axis=0, keepdims=True)
    m_st[...] = m_new
    yield

    hm = _sigmoid(og_ref[...].astype(F32)) * hid
    ms = _dot_lhs_f32(hm * hm, head_sum, terms=2) * (1.0 / HEAD_DIM)
    o_ref[...] = (hm * lax.rsqrt(ms + EPS) * ng_ref[...] * _silu(z_ref[...].astype(F32))).astype(o_ref.dtype)


def _mlstm_mixer(proj, small, conv_w, conv_b, if_b, norm_g):
    bsz, seq, _ = proj.shape
    n = CHUNK
    ifb = jnp.zeros((1, LANES), F32).at[0, SM_I:SM_I + 2 * N_HEADS].set(if_b)
    ng = jnp.tile(norm_g, N_HEADS).reshape(1, D_GROUP)
    full = lambda b, c: (0, 0)
    nb = _seqs_per_step(bsz)
    return pl.pallas_call(
        functools.partial(_mlstm_kernel, nb=nb),
        grid=(bsz // nb, seq // n),
        in_specs=[pl.BlockSpec((nb, n, 2 * D_GROUP), lambda b, c: (b, c, PB_D_QK * LANES // (2 * D_GROUP))),
                  pl.BlockSpec((nb, n, D_GROUP), lambda b, c: (b, c, PB_D_V // 2)),
                  pl.BlockSpec((nb, n, LANES), lambda b, c: (b, c, 0)),
                  pl.BlockSpec((nb, n, D_GROUP), lambda b, c: (b, c, PB_D_O // 2)),
                  pl.BlockSpec((nb, n, D_GROUP), lambda b, c: (b, c, PB_D_Z // 2)),
                  pl.BlockSpec((CONV_WIDTH, 2 * D_GROUP), full),
                  pl.BlockSpec((1, 2 * D_GROUP), full),
                  pl.BlockSpec((1, LANES), full),
                  pl.BlockSpec((1, D_GROUP), full)],
        out_specs=pl.BlockSpec((nb, n, D_GROUP), lambda b, c: (b, c, 0)),
        out_shape=jax.ShapeDtypeStruct((bsz, seq, D_GROUP), BF16),
        scratch_shapes=[pltpu.VMEM((nb, CONV_PAD + n, 2 * D_GROUP), F32),
                        pltpu.VMEM((nb, D_GROUP, D_GROUP), F32),
                        pltpu.VMEM((nb, 1, D_GROUP), F32),
                        pltpu.VMEM((nb, 1, D_GROUP), F32)],
        compiler_params=_cparams(("arbitrary", "arbitrary")),
        name="mlstm_mixer",
    )(proj, proj, small, proj, proj, conv_w, conv_b.reshape(1, 2 * D_GROUP), ifb, ng)


def _outproj_kernel(ya_ref, yb_ref, yc_ref, yd_ref, x_ref, gate_ref, w_ref, fg_ref, o_ref, *, final):
    acc = _dot(ya_ref[0], w_ref[0:D_GROUP, :])
    acc = acc + _dot(yb_ref[0], w_ref[D_GROUP:2 * D_GROUP, :])
    acc = acc + _dot(yc_ref[0], w_ref[2 * D_GROUP:3 * D_GROUP, :])
    acc = acc + _dot(yd_ref[0], w_ref[3 * D_GROUP:4 * D_GROUP, :])
    out = x_ref[0] + gate_ref[0] * acc
    if final:
        ms = jnp.mean(out * out, axis=-1, keepdims=True)
        out = out * lax.rsqrt(ms + EPS) * fg_ref[...]
    o_ref[0] = out


def _out_projection(ys, x, gate, w_bf, final_g, final):
    bsz, seq, d = x.shape
    tm = min(512, seq)
    yspec = pl.BlockSpec((1, tm, D_GROUP), lambda b, i: (b, i, 0))
    return pl.pallas_call(
        functools.partial(_outproj_kernel, final=final),
        grid=(bsz, seq // tm),
        in_specs=[yspec, yspec, yspec, yspec,
                  pl.BlockSpec((1, tm, d), lambda b, i: (b, i, 0)),
                  pl.BlockSpec((1, 1, d), lambda b, i: (b, 0, 0)),
                  pl.BlockSpec((N_HEADS * D_GROUP, d), lambda b, i: (0, 0)),
                  pl.BlockSpec((1, d), lambda b, i: (0, 0))],
        out_specs=pl.BlockSpec((1, tm, d), lambda b, i: (b, i, 0)),
        out_shape=jax.ShapeDtypeStruct((bsz, seq, d), F32),
        compiler_params=_cparams(("arbitrary", "arbitrary")),
        name="out_projection",
    )(*ys, x, gate, w_bf, final_g.reshape(1, d))


def _cmp_to_sel_t(seq):
    n_cmp = (seq - NSA_CMP_BLOCK) // NSA_CMP_STRIDE + 1
    n_sel = seq // NSA_SEL_BLOCK
    start = np.arange(n_cmp)[:, None] * NSA_CMP_STRIDE
    sel_start = np.arange(n_sel)[None, :] * NSA_SEL_BLOCK
    overlap = np.clip(np.minimum(start + NSA_CMP_BLOCK, sel_start + NSA_SEL_BLOCK)
                      - np.maximum(start, sel_start), 0, None)
    out = np.zeros((n_sel, seq // NSA_CMP_STRIDE), np.float32)
    out[:, :n_cmp] = (overlap / NSA_CMP_BLOCK).T
    return jnp.asarray(out, BF16)


def _mixers(proj, small, layer_idx, p):
    bsz, seq, _ = proj.shape
    n_rows = seq // NSA_CMP_STRIDE
    width = NSA_CMP_STRIDE * HEAD_DIM
    c0 = PB_A_KCVC * LANES
    rk = proj[:, :, c0:c0 + HEAD_DIM].reshape(bsz, n_rows, width)
    rv = proj[:, :, c0 + HEAD_DIM:c0 + 2 * HEAD_DIM].reshape(bsz, n_rows, width)
    pos = p['nsa_cmp_pos'].reshape(1, NSA_CMP_BLOCK * HEAD_DIM)
    ckv, cvt = _nsa_compress(rk, rv, pos, p['nsa_ck_w1'], p['nsa_ck_w2'], p['nsa_cv_w1'], p['nsa_cv_w2'])
    y_a = _nsa_attention(proj, small, ckv, cvt, _cmp_to_sel_t(seq), p['nsa_norm_g'])
    y_b = _diff_attention(proj, p['diff_lam'], p['diff_norm_g'], layer_idx)
    y_c = _ssd_mixer(proj, small, p['ssm_conv_w'], p['ssm_conv_b'], p['ssm_dt_bias'], p['ssm_a_log'],
                     p['ssm_d'], p['ssm_norm_g'])
    y_d = _mlstm_mixer(proj, small, p['ml_conv_w'], p['ml_conv_b'], p['ml_if_b'], p['ml_norm_g'])
    return y_a, y_b, y_c, y_d


_LAYER_PARAMS = ('nsa_cmp_pos', 'nsa_ck_w1', 'nsa_ck_w2', 'nsa_cv_w1', 'nsa_cv_w2', 'nsa_norm_g',
                 'diff_lam', 'diff_norm_g', 'ssm_conv_w', 'ssm_conv_b', 'ssm_dt_bias', 'ssm_a_log',
                 'ssm_d', 'ssm_norm_g', 'ml_conv_w', 'ml_conv_b', 'ml_if_b', 'ml_norm_g')


def kernel(x, c, norm_g, ada_w, ada_b, w_in, w_out, nsa_cmp_pos, nsa_ck_w1, nsa_ck_w2, nsa_cv_w1, nsa_cv_w2, nsa_norm_g, diff_lam, diff_norm_g, ssm_conv_w, ssm_conv_b, ssm_dt_bias, ssm_a_log, ssm_d, ssm_norm_g, ml_conv_w, ml_conv_b, ml_if_b, ml_norm_g, final_g):
    stacked = dict(nsa_cmp_pos=nsa_cmp_pos, nsa_ck_w1=nsa_ck_w1, nsa_ck_w2=nsa_ck_w2, nsa_cv_w1=nsa_cv_w1,
                   nsa_cv_w2=nsa_cv_w2, nsa_norm_g=nsa_norm_g, diff_lam=diff_lam, diff_norm_g=diff_norm_g,
                   ssm_conv_w=ssm_conv_w, ssm_conv_b=ssm_conv_b, ssm_dt_bias=ssm_dt_bias,
                   ssm_a_log=ssm_a_log, ssm_d=ssm_d, ssm_norm_g=ssm_norm_g, ml_conv_w=ml_conv_w,
                   ml_conv_b=ml_conv_b, ml_if_b=ml_if_b, ml_norm_g=ml_norm_g)
    depth = w_in.shape[0]
    bsz, seq, d = x.shape
    mod = _ada_modulation(c, ada_w, ada_b)
    for l in range(depth):
        p = {name: stacked[name][l] for name in _LAYER_PARAMS}
        shift = mod[l, :, 0:d].reshape(bsz, 1, d)
        scale = mod[l, :, d:2 * d].reshape(bsz, 1, d)
        gate = mod[l, :, 2 * d:3 * d].reshape(bsz, 1, d)
        w_bf = _relayout_w_in(w_in[l]).astype(BF16)
        proj, small = _in_projection(x, norm_g[l], scale, shift, w_bf)
        ys = _mixers(proj, small, l, p)
        x = _out_projection(ys, x, gate, w_out[l].astype(BF16), final_g, final=(l == depth - 1))
    return x
```

```python
import functools
import math

import numpy as np
import jax
import jax.numpy as jnp
from jax import lax
from jax.experimental import pallas as pl
from jax.experimental.pallas import tpu as pltpu

F32 = jnp.float32
BF16 = jnp.bfloat16

D_MODEL = 1024
N_HEADS = 4
HEAD_DIM = 64
D_GROUP = N_HEADS * HEAD_DIM
NEG_INF = -1e30
EPS = 1e-6

NSA_CMP_BLOCK = 32
NSA_CMP_STRIDE = 16
NSA_SEL_BLOCK = 64
NSA_TOP_N = 16
NSA_WINDOW = 512
NSA_CMP_HIDDEN = 128
NSA_FORCED_SCORE = 1e4

DIFF_QK_DIM = HEAD_DIM // 2
SSM_STATE = 64
SSM_GROUPS = 2
SSM_XBC = D_GROUP + 2 * SSM_GROUPS * SSM_STATE
CONV_WIDTH = 4
CHUNK = 128
CONV_PAD = 8

LANES = 128
D_PROJ = 4096

_SPLITS = (
    D_GROUP, HEAD_DIM, HEAD_DIM, HEAD_DIM, HEAD_DIM, HEAD_DIM, HEAD_DIM, 3 * N_HEADS, D_GROUP,
    D_GROUP, D_GROUP, D_GROUP, D_GROUP,
    D_GROUP, SSM_XBC, N_HEADS,
    2 * D_GROUP, D_GROUP, 2 * N_HEADS, D_GROUP, D_GROUP,
)
_OFFS = [0] + [int(o) for o in np.cumsum(_SPLITS)]
(_A_Q, _A_KC, _A_VC, _A_KS, _A_VS, _A_KW, _A_VW, _A_G, _A_Z,
 _B_Q, _B_K, _B_V, _B_Z, _C_Z, _C_XBC, _C_DT, _D_QK, _D_V, _D_IF, _D_O, _D_Z) = range(21)

SM_GATE = 0
SM_DT = 12
SM_I = 16
SM_F = 20

PB_A_Q = 0
PB_A_KCVC = 2
PB_A_KSVS = 3
PB_A_KWVW = 4
PB_SMALL = 5
PB_A_Z = 6
PB_B_Q = 8
PB_B_K = 10
PB_B_V = 12
PB_B_Z = 14
PB_C_XBC = 16
PB_C_Z = 20
PB_D_V = 22
PB_D_QK = 24
PB_D_O = 28
PB_D_Z = 30

VMEM_LIMIT = 56 * 1024 * 1024
LOG2E = 1.4426950408889634
SUM_ROWS = 16
WINDOW_SUB = 256
SKEW = 1


def _cparams(sem):
    return pltpu.CompilerParams(dimension_semantics=sem, vmem_limit_bytes=VMEM_LIMIT)


def _relayout_w_in(w_in):
    def col(i):
        return w_in[:, _OFFS[i]:_OFFS[i + 1]]
    d = w_in.shape[0]
    small = jnp.concatenate(
        [col(_A_G), col(_C_DT), col(_D_IF), jnp.zeros((d, LANES - 24), w_in.dtype)], axis=1)
    parts = [col(_A_Q), col(_A_KC), col(_A_VC), col(_A_KS), col(_A_VS), col(_A_KW), col(_A_VW),
             small, col(_A_Z),
             col(_B_Q), col(_B_K), col(_B_V), col(_B_Z),
             col(_C_XBC), col(_C_Z),
             col(_D_V), col(_D_QK), col(_D_O), col(_D_Z)]
    out = jnp.concatenate(parts, axis=1)
    assert out.shape[1] == D_PROJ
    return out


def _dot(a, b):
    return jnp.dot(a, b, preferred_element_type=F32)


def _nt(a, b):
    return lax.dot_general(a, b, (((1,), (1,)), ((), ())), preferred_element_type=F32)


def _split3(a):
    hi = a.astype(BF16)
    r1 = a - hi.astype(F32)
    mid = r1.astype(BF16)
    lo = (r1 - mid.astype(F32)).astype(BF16)
    return hi, mid, lo


def _dot_lhs_f32(a, b_exact, terms=3):
    hi, mid, lo = _split3(a)
    out = _dot(hi, b_exact) + _dot(mid, b_exact)
    return out + _dot(lo, b_exact) if terms == 3 else out


def _dot_rhs_f32(a_exact, b, terms=3):
    hi, mid, lo = _split3(b)
    out = _dot(a_exact, hi) + _dot(a_exact, mid)
    return out + _dot(a_exact, lo) if terms == 3 else out


def _nt_rhs_f32(a_exact, b, terms=3):
    hi, mid, lo = _split3(b)
    out = _nt(a_exact, hi) + _nt(a_exact, mid)
    return out + _nt(a_exact, lo) if terms == 3 else out


def _seqs_per_step(bsz):
    return math.gcd(bsz, 4)


def _sigmoid(x):
    return 1.0 / (1.0 + jnp.exp(-x))


def _silu(x):
    return x * _sigmoid(x)


def _softplus(x):
    return jnp.maximum(x, 0.0) + jnp.log(1.0 + jnp.exp(-jnp.abs(x)))


def _iota(shape, dim):
    return lax.broadcasted_iota(jnp.int32, shape, dim)


def _ada_kernel(c_ref, w_ref, b_ref, o_ref):
    ca = _silu(c_ref[...])
    w = w_ref[0]
    c_hi, c_mid, _ = _split3(ca)
    w_hi, w_mid, _ = _split3(w)
    acc = _dot(c_hi, w_hi) + _dot(c_hi, w_mid) + _dot(c_mid, w_hi)
    o_ref[0] = acc + b_ref[0]


def _ada_modulation(c, ada_w, ada_b):
    depth, d, d3 = ada_w.shape
    bsz = c.shape[0]
    nb = d3 // d
    return pl.pallas_call(
        _ada_kernel,
        grid=(depth, nb),
        in_specs=[pl.BlockSpec((bsz, d), lambda l, j: (0, 0)),
                  pl.BlockSpec((1, d, d), lambda l, j: (l, 0, j)),
                  pl.BlockSpec((1, 1, d), lambda l, j: (l, 0, j))],
        out_specs=pl.BlockSpec((1, bsz, d), lambda l, j: (l, 0, j)),
        out_shape=jax.ShapeDtypeStruct((depth, bsz, d3), F32),
        compiler_params=_cparams(("arbitrary", "arbitrary")),
        name="ada_modulation",
    )(c, ada_w, ada_b.reshape(depth, 1, d3))


def _inproj_kernel(x_ref, g_ref, sc_ref, sh_ref, w_ref, o_ref, small_ref, *, tn):
    x = x_ref[0]
    ms = jnp.mean(x * x, axis=-1, keepdims=True)
    y = x * lax.rsqrt(ms + EPS) * g_ref[...]
    h = (y * (1.0 + sc_ref[0]) + sh_ref[0]).astype(BF16)
    for j in range(D_PROJ // tn):
        acc = _dot(h, w_ref[:, j * tn:(j + 1) * tn])
        o_ref[0, :, j * tn:(j + 1) * tn] = acc.astype(o_ref.dtype)
        if j == (PB_SMALL * LANES) // tn:
            off = PB_SMALL * LANES - j * tn
            small_ref[0] = acc[:, off:off + LANES]


def _in_projection(x, norm_g, scale, shift, w_bf):
    bsz, seq, d = x.shape
    tm = min(1024, seq)
    tn = 1024
    return pl.pallas_call(
        functools.partial(_inproj_kernel, tn=tn),
        grid=(bsz, seq // tm),
        in_specs=[pl.BlockSpec((1, tm, d), lambda b, i: (b, i, 0)),
                  pl.BlockSpec((1, d), lambda b, i: (0, 0)),
                  pl.BlockSpec((1, 1, d), lambda b, i: (b, 0, 0)),
                  pl.BlockSpec((1, 1, d), lambda b, i: (b, 0, 0)),
                  pl.BlockSpec((d, D_PROJ), lambda b, i: (0, 0))],
        out_specs=[pl.BlockSpec((1, tm, D_PROJ), lambda b, i: (b, i, 0)),
                   pl.BlockSpec((1, tm, LANES), lambda b, i: (b, i, 0))],
        out_shape=[jax.ShapeDtypeStruct((bsz, seq, D_PROJ), BF16),
                   jax.ShapeDtypeStruct((bsz, seq, LANES), F32)],
        compiler_params=_cparams(("arbitrary", "arbitrary")),
        name="in_projection",
    )(x, norm_g.reshape(1, d), scale, shift, w_bf)


def _compress_kernel(rk_ref, rv_ref, pos_ref, kw1_ref, kw2_ref, vw1_ref, vw2_ref, kv_ref, vt_ref):
    half = NSA_CMP_STRIDE * HEAD_DIM
    pos = jnp.broadcast_to(pos_ref[...], (8, 2 * half)).astype(BF16)

    def one(r_ref, w1_ref, w2_ref):
        r = r_ref[0].astype(BF16)
        w1 = w1_ref[...].astype(BF16)
        first = _dot(r, w1[:half])
        second = _dot(r, w1[half:])
        n = first.shape[0]
        second = pltpu.roll(second, n - 1, 0)
        bias = _dot(pos, w1)[0:1]
        hid = _silu(first + second + bias)
        return _dot(hid.astype(BF16), w2_ref[...].astype(BF16))

    kc = one(rk_ref, kw1_ref, kw2_ref)
    vc = one(rv_ref, vw1_ref, vw2_ref)
    kv = jnp.concatenate([kc, vc], axis=1)
    kv_ref[0] = kv.astype(BF16)
    ones = jnp.ones((SUM_ROWS, kv.shape[0]), BF16)
    vt_ref[0] = jnp.concatenate([kv.T[HEAD_DIM:].astype(BF16), ones], axis=0)


def _nsa_compress(rk, rv, pos, kw1, kw2, vw1, vw2):
    bsz, n, width = rk.shape
    full2 = lambda b: (0, 0)
    return pl.pallas_call(
        _compress_kernel,
        grid=(bsz,),
        in_specs=[pl.BlockSpec((1, n, width), lambda b: (b, 0, 0)),
                  pl.BlockSpec((1, n, width), lambda b: (b, 0, 0)),
                  pl.BlockSpec(pos.shape, full2),
                  pl.BlockSpec(kw1.shape, full2), pl.BlockSpec(kw2.shape, full2),
                  pl.BlockSpec(vw1.shape, full2), pl.BlockSpec(vw2.shape, full2)],
        out_specs=[pl.BlockSpec((1, n, LANES), lambda b: (b, 0, 0)),
                   pl.BlockSpec((1, HEAD_DIM + SUM_ROWS, n), lambda b: (b, 0, 0))],
        out_shape=[jax.ShapeDtypeStruct((bsz, n, LANES), BF16),
                   jax.ShapeDtypeStruct((bsz, HEAD_DIM + SUM_ROWS, n), BF16)],
        compiler_params=_cparams(("arbitrary",)),
        name="nsa_compress",
    )(rk, rv, pos, kw1, kw2, vw1, vw2)


def _online_softmax(s, m):
    m_new = jnp.maximum(m, jnp.max(s, axis=0, keepdims=True))
    return m_new, jnp.exp2(s - m_new).astype(BF16), jnp.exp2(m - m_new)


def _nsa_kernel(q_ref, ks_ref, kw_ref, sm_ref, z_ref, ckv_ref, cvt_ref, c2s_ref, ng_ref, o_ref,
                vst, vwt, sel_scr, imp_scr, ms_scr, as_scr, *, seq, tq, tk, top_n):
    i = pl.program_id(1)
    n_cmp = seq // NSA_CMP_STRIDE
    n_sel = seq // NSA_SEL_BLOCK
    sub = min(tq, WINDOW_SUB)
    span = NSA_WINDOW + sub

    @pl.when(i == 0)
    def _():
        ones = jnp.ones((SUM_ROWS, LANES), BF16)
        for blk in range(seq // LANES):
            rows = slice(blk * LANES, (blk + 1) * LANES)
            ks = ks_ref[0, rows, :].astype(F32)
            kw = kw_ref[0, rows, :].astype(F32)
            vst[:, rows] = jnp.concatenate([ks.T[HEAD_DIM:].astype(BF16), ones], axis=0)
            vwt[:, rows] = jnp.concatenate([kw.T[HEAD_DIM:].astype(BF16), ones], axis=0)

    q0 = pl.multiple_of(i * tq, tq)
    q = q_ref[0].astype(F32) * (HEAD_DIM ** -0.5 * LOG2E)
    low = _iota((tq, LANES), 1) < HEAD_DIM
    qh = []
    for pair in range(N_HEADS // 2):
        qp = q[:, pair * LANES:(pair + 1) * LANES]
        qh.append(jnp.where(low, qp, 0.0).astype(BF16))
        qh.append(jnp.where(low, pltpu.roll(qp, HEAD_DIM, 1), 0.0).astype(BF16))

    t_row = q0 + _iota((1, tq), 1)

    ckv = ckv_ref[0]
    cvt = cvt_ref[0]
    cmp_end = _iota((n_cmp, 1), 0) * NSA_CMP_STRIDE + (NSA_CMP_BLOCK - 1)
    valid_c = cmp_end <= t_row
    any_valid = t_row >= NSA_CMP_BLOCK - 1
    heads = range(N_HEADS)
    s_c = [jnp.where(valid_c, _nt(ckv, qh[h]), NEG_INF) for h in heads]
    m_c = [jnp.max(s_c[h], axis=0, keepdims=True) for h in heads]
    p_c = [jnp.exp2(s_c[h] - m_c[h]) for h in heads]
    pv_c = [_dot(cvt, p_c[h].astype(BF16)) for h in heads]
    inv_c = [jnp.where(any_valid, 1.0 / pv_c[h][HEAD_DIM:HEAD_DIM + 1], 0.0) for h in heads]
    o_c = [pv_c[h][:HEAD_DIM] * inv_c[h] for h in heads]
    p_sum = p_c[0] * inv_c[0]
    for h in range(1, N_HEADS):
        p_sum = p_sum + p_c[h] * inv_c[h]

    p_hi = p_sum.astype(BF16)
    p_lo = (p_sum - p_hi.astype(F32)).astype(BF16)
    c2s = c2s_ref[...]
    imp = _dot(c2s, p_hi) + _dot(c2s, p_lo)
    sid = _iota((n_sel, tq), 0)
    cur = t_row // NSA_SEL_BLOCK
    forced = (sid == cur) | (sid == 0)
    imp = jnp.where(forced, NSA_FORCED_SCORE, jnp.where(sid <= cur, imp, -1.0))
    imp_scr[...] = imp
    per_step = tq // NSA_SEL_BLOCK

    def rank_body(jj, rank):
        for u in range(per_step):
            j = jj * per_step + u
            vj = imp_scr[pl.ds(j, 1), :]
            ge = jnp.where(vj >= imp, 1.0, 0.0)
            gt = jnp.where(vj > imp, 1.0, 0.0)
            rank = rank + jnp.where(sid > j, ge, gt)
        return rank

    rank = lax.fori_loop(0, i + 1, rank_body, jnp.zeros((n_sel, tq), F32))
    sel_scr[...] = jnp.where(rank < float(top_n), 1.0, 0.0)

    blocks_per_chunk = tk // NSA_SEL_BLOCK

    def stream_step(scores_fn, vt_c, valid, m_scr, acc_scr):
        s, p, alpha = [None] * N_HEADS, [None] * N_HEADS, [None] * N_HEADS

        def update(h):
            acc_scr[h] = alpha[h] * acc_scr[h] + _dot(vt_c, p[h])

        s[0] = scores_fn(0)
        for h in range(N_HEADS):
            if h + 1 < N_HEADS:
                s[h + 1] = scores_fn(h + 1)
            m_scr[h], p[h], alpha[h] = _online_softmax(jnp.where(valid, s[h], NEG_INF), m_scr[h])
            if h >= 1:
                update(h - 1)
        update(N_HEADS - 1)

    def sel_step(kstart, causal):
        kstart = pl.multiple_of(kstart, tk)
        k_c = ks_ref[0, pl.ds(kstart, tk), :]
        sb = pl.multiple_of(kstart // NSA_SEL_BLOCK, blocks_per_chunk)
        selc = sel_scr[pl.ds(sb, blocks_per_chunk), :]
        rows = [jnp.broadcast_to(selc[r:r + 1, :], (NSA_SEL_BLOCK, tq)) for r in range(blocks_per_chunk)]
        valid = jnp.concatenate(rows, axis=0) > 0.5
        if causal:
            valid = valid & (kstart + _iota((tk, 1), 0) <= t_row)
        stream_step(lambda h: _nt(k_c, qh[h]), vst[:, pl.ds(kstart, tk)], valid, ms_scr, as_scr)

    ms_scr[...] = jnp.full(ms_scr.shape, NEG_INF, F32)
    as_scr[...] = jnp.zeros(as_scr.shape, F32)
    n_full = q0 // tk

    def full_chunk(c, carry):
        sel_step(c * tk, False)
        return carry

    lax.fori_loop(0, n_full, full_chunk, 0)
    sel_step(n_full * tk, True)

    streams = []
    for a in range(tq // sub):
        qs = slice(a * sub, (a + 1) * sub)
        wstart = pl.multiple_of(jnp.maximum(q0 + a * sub - NSA_WINDOW, 0), sub)
        kpos_w = wstart + _iota((span, 1), 0)
        t_sub = t_row[:, qs]
        valid_w = (kpos_w <= t_sub) & (kpos_w > t_sub - NSA_WINDOW)
        streams += [(kw_ref[0, pl.ds(wstart, span), :], vwt[:, pl.ds(wstart, span)], valid_w, qh[h][qs])
                    for h in range(N_HEADS)]
    s_w = [jnp.where(valid, _nt(k_w, q_w), NEG_INF) for k_w, _, valid, q_w in streams]
    m_w = [jnp.max(sw, axis=0, keepdims=True) for sw in s_w]
    p_w = [jnp.exp2(sw - mw).astype(BF16) for sw, mw in zip(s_w, m_w)]
    pv_w = [_dot(st[1], pw) for st, pw in zip(streams, p_w)]
    o_w_parts = [pv[:HEAD_DIM] * (1.0 / pv[HEAD_DIM:HEAD_DIM + 1]) for pv in pv_w]
    o_w = [jnp.concatenate(o_w_parts[h::N_HEADS], axis=1) for h in range(N_HEADS)]

    gates = _sigmoid(sm_ref[0].T[0:16])
    ng = ng_ref[...]
    outs = []
    for h in range(N_HEADS):
        g_c = gates[3 * h:3 * h + 1]
        g_s = gates[3 * h + 1:3 * h + 2]
        g_w = gates[3 * h + 2:3 * h + 3]
        acc_s = as_scr[h]
        o_s = acc_s[:HEAD_DIM] * (1.0 / acc_s[HEAD_DIM:HEAD_DIM + 1])
        oh = g_c * o_c[h] + g_s * o_s + g_w * o_w[h]
        ms = jnp.mean(oh * oh, axis=0, keepdims=True)
        outs.append(oh * lax.rsqrt(ms + EPS) * ng)
    y = jnp.concatenate(
        [jnp.concatenate(outs[0:2], axis=0).T, jnp.concatenate(outs[2:4], axis=0).T], axis=1)
    o_ref[0] = (y * _silu(z_ref[0].astype(F32))).astype(o_ref.dtype)


def _nsa_attention(proj, small, ckv, cvt, c2s_t, norm_g, *, tq=512, tk=512):
    bsz, seq, _ = proj.shape
    tk = min(tk, seq)
    assert tk % tq == 0
    n_cmp = seq // NSA_CMP_STRIDE
    n_sel = seq // NSA_SEL_BLOCK
    top_n = min(NSA_TOP_N, n_sel)
    ng = jnp.broadcast_to(norm_g.reshape(HEAD_DIM, 1), (HEAD_DIM, tq)).astype(F32)
    kern = functools.partial(_nsa_kernel, seq=seq, tq=tq, tk=tk, top_n=top_n)
    return pl.pallas_call(
        kern,
        grid=(bsz, seq // tq),
        in_specs=[pl.BlockSpec((1, tq, 2 * LANES), lambda b, i: (b, i, PB_A_Q // 2)),
                  pl.BlockSpec((1, seq, LANES), lambda b, i: (b, 0, PB_A_KSVS)),
                  pl.BlockSpec((1, seq, LANES), lambda b, i: (b, 0, PB_A_KWVW)),
                  pl.BlockSpec((1, tq, LANES), lambda b, i: (b, i, 0)),
                  pl.BlockSpec((1, tq, 2 * LANES), lambda b, i: (b, i, PB_A_Z // 2)),
                  pl.BlockSpec((1, n_cmp, LANES), lambda b, i: (b, 0, 0)),
                  pl.BlockSpec((1, HEAD_DIM + SUM_ROWS, n_cmp), lambda b, i: (b, 0, 0)),
                  pl.BlockSpec((n_sel, n_cmp), lambda b, i: (0, 0)),
                  pl.BlockSpec((HEAD_DIM, tq), lambda b, i: (0, 0))],
        out_specs=pl.BlockSpec((1, tq, D_GROUP), lambda b, i: (b, i, 0)),
        out_shape=jax.ShapeDtypeStruct((bsz, seq, D_GROUP), BF16),
        scratch_shapes=[pltpu.VMEM((HEAD_DIM + SUM_ROWS, seq), BF16),
                        pltpu.VMEM((HEAD_DIM + SUM_ROWS, seq), BF16),
                        pltpu.VMEM((n_sel, tq), F32), pltpu.VMEM((n_sel, tq), F32),
                        pltpu.VMEM((N_HEADS, 1, tq), F32),
                        pltpu.VMEM((N_HEADS, HEAD_DIM + SUM_ROWS, tq), F32)],
        compiler_params=_cparams(("arbitrary", "arbitrary")),
        name="nsa_attention",
    )(proj, proj, proj, small, proj, ckv, cvt, c2s_t, ng)


def _diff_kernel(q_ref, k_ref, v_ref, z_ref, lam_ref, ng_ref, o_ref, vt, m_scr, acc_scr, *, seq, tq, tk,
                 lambda_init):
    i = pl.program_id(1)
    vrows = HEAD_DIM + SUM_ROWS
    n_streams = 2 * N_HEADS

    @pl.when(i == 0)
    def _():
        ones = jnp.ones((SUM_ROWS, LANES), BF16)
        for blk in range(seq // LANES):
            rows = slice(blk * LANES, (blk + 1) * LANES)
            parts = []
            for pair in range(N_HEADS // 2):
                v_t = v_ref[0, rows, pair * LANES:(pair + 1) * LANES].astype(F32).T.astype(BF16)
                parts += [v_t[:HEAD_DIM], ones, v_t[HEAD_DIM:], ones]
            vt[:, rows] = jnp.concatenate(parts, axis=0)

    lam = lam_ref[...]
    lam_full = (jnp.exp(jnp.sum(lam[0:1] * lam[1:2], axis=1, keepdims=True))
                - jnp.exp(jnp.sum(lam[2:3] * lam[3:4], axis=1, keepdims=True)) + lambda_init)

    q0 = pl.multiple_of(i * tq, tq)
    q = q_ref[0].astype(F32) * (DIFF_QK_DIM ** -0.5 * LOG2E)
    part = _iota((tq, LANES), 1) // DIFF_QK_DIM
    qm = []
    for h in range(N_HEADS):
        qp = q[:, (h // 2) * LANES:(h // 2 + 1) * LANES]
        qm += [jnp.where(part == 2 * (h % 2) + c, qp, 0.0).astype(BF16) for c in range(2)]
    t_cols = q0 + _iota((1, tq), 1)

    def scores(st, kstart):
        pair = st // 4
        return _nt(k_ref[0, pl.ds(kstart, tk), pair * LANES:(pair + 1) * LANES], qm[st])

    def softmax(s, m):
        m_new = jnp.maximum(m, jnp.max(s, axis=0, keepdims=True))
        return m_new, jnp.exp2(s - m_new).astype(BF16), jnp.exp2(m - m_new)

    def accumulate(st, kstart, p, alpha, acc):
        h = st // 2
        return alpha * acc + _dot(vt[h * vrows:(h + 1) * vrows, pl.ds(kstart, tk)], p)

    def step(kstart, causal):
        kstart = pl.multiple_of(kstart, tk)
        s, p, alpha = [None] * n_streams, [None] * n_streams, [None] * n_streams
        kpos = kstart + _iota((tk, 1), 0)

        def update(st):
            acc_scr[st] = accumulate(st, kstart, p[st], alpha[st], acc_scr[st])

        for st in range(min(SKEW, n_streams)):
            s[st] = scores(st, kstart)
        for st in range(n_streams):
            if st + SKEW < n_streams:
                s[st + SKEW] = scores(st + SKEW, kstart)
            sh = jnp.where(kpos <= t_cols, s[st], NEG_INF) if causal else s[st]
            m_scr[st], p[st], alpha[st] = softmax(sh, m_scr[st])
            if st >= 1:
                update(st - 1)
        update(n_streams - 1)

    m_scr[...] = jnp.full(m_scr.shape, NEG_INF, F32)
    acc_scr[...] = jnp.zeros(acc_scr.shape, F32)
    n_full = q0 // tk

    def full_chunk(c, carry):
        step(c * tk, False)
        return carry

    lax.fori_loop(0, n_full, full_chunk, 0)
    step(n_full * tk, True)

    ng = ng_ref[...]
    outs = []
    for h in range(N_HEADS):
        maps = []
        for c in range(2):
            acc = acc_scr[2 * h + c]
            maps.append(acc[:HEAD_DIM] * (1.0 / acc[HEAD_DIM:HEAD_DIM + 1]))
        o = maps[0] - lam_full * maps[1]
        ms = jnp.mean(o * o, axis=0, keepdims=True)
        outs.append(o * lax.rsqrt(ms + EPS) * ng * (1.0 - lambda_init))
    y = jnp.concatenate(
        [jnp.concatenate(outs[0:2], axis=0).T, jnp.concatenate(outs[2:4], axis=0).T], axis=1)
    o_ref[0] = (y * _silu(z_ref[0].astype(F32))).astype(o_ref.dtype)


def _diff_attention(proj, lam, norm_g, layer_idx, *, tq=512, tk=512):
    bsz, seq, _ = proj.shape
    tk = min(tk, seq)
    assert tk % tq == 0
    lambda_init = 0.8 - 0.6 * math.exp(-0.3 * layer_idx)
    ng = jnp.broadcast_to(norm_g.reshape(HEAD_DIM, 1), (HEAD_DIM, tq)).astype(F32)
    kern = functools.partial(_diff_kernel, seq=seq, tq=tq, tk=tk, lambda_init=lambda_init)
    return pl.pallas_call(
        kern,
        grid=(bsz, seq // tq),
        in_specs=[pl.BlockSpec((1, tq, D_GROUP), lambda b, i: (b, i, PB_B_Q // 2)),
                  pl.BlockSpec((1, seq, D_GROUP), lambda b, i: (b, 0, PB_B_K // 2)),
                  pl.BlockSpec((1, seq, D_GROUP), lambda b, i: (b, 0, PB_B_V // 2)),
                  pl.BlockSpec((1, tq, D_GROUP), lambda b, i: (b, i, PB_B_Z // 2)),
                  pl.BlockSpec((4, DIFF_QK_DIM), lambda b, i: (0, 0)),
                  pl.BlockSpec((HEAD_DIM, tq), lambda b, i: (0, 0))],
        out_specs=pl.BlockSpec((1, tq, D_GROUP), lambda b, i: (b, i, 0)),
        out_shape=jax.ShapeDtypeStruct((bsz, seq, D_GROUP), BF16),
        scratch_shapes=[pltpu.VMEM((N_HEADS * (HEAD_DIM + SUM_ROWS), seq), BF16),
                        pltpu.VMEM((2 * N_HEADS, 1, tq), F32),
                        pltpu.VMEM((2 * N_HEADS, HEAD_DIM + SUM_ROWS, tq), F32)],
        compiler_params=_cparams(("arbitrary", "arbitrary")),
        name="diff_attention",
    )(proj, proj, proj, proj, lam, ng)


def _causal_conv_silu(x_raw, xext, cw_ref, cb_ref):
    n = x_raw.shape[0]
    xext[CONV_PAD:CONV_PAD + n, :] = x_raw
    cw = cw_ref[...]
    acc = cb_ref[...] + cw[0:1] * xext[pl.ds(CONV_PAD - 3, n), :]
    for k in range(1, CONV_WIDTH):
        acc = acc + cw[k:k + 1] * xext[pl.ds(CONV_PAD - 3 + k, n), :]
    xext[0:CONV_PAD, :] = x_raw[n - CONV_PAD:, :]
    return _silu(acc)


def _interleave(chunks):
    live = list(chunks)
    while live:
        nxt = []
        for g in live:
            try:
                next(g)
                nxt.append(g)
            except StopIteration:
                pass
        live = nxt


def _tri(n):
    return jnp.where(_iota((n, n), 0) >= _iota((n, n), 1), 1.0, 0.0).astype(BF16)


def _expand_mat(first_lane, width):
    r = _iota((LANES, width), 0)
    c = _iota((LANES, width), 1)
    return jnp.where(r - first_lane == c // HEAD_DIM, 1.0, 0.0).astype(BF16)


def _row_select(width):
    r = _iota((8, width), 0)
    c = _iota((8, width), 1)
    return jnp.where(c == r * HEAD_DIM, 1.0, 0.0).astype(BF16)


def _group_sum_mat(width, group):
    r = _iota((width, width), 0)
    c = _iota((width, width), 1)
    return jnp.where(r // group == c // group, 1.0, 0.0).astype(BF16)


def _ssd_kernel(z_ref, xbc_ref, sm_ref, cw_ref, cb_ref, dtb_ref, alog_ref, dsk_ref, ng_ref, o_ref,
                xext, st, *, nb):
    @pl.when(pl.program_id(1) == 0)
    def _():
        st[...] = jnp.zeros(st.shape, F32)
        xext[:, 0:CONV_PAD, :] = jnp.zeros((nb, CONV_PAD, SSM_XBC), F32)

    _interleave(_ssd_chunk(z_ref.at[bb], xbc_ref.at[bb], sm_ref.at[bb], cw_ref, cb_ref, dtb_ref, alog_ref,
                           dsk_ref, ng_ref, o_ref.at[bb], xext.at[bb], st.at[bb]) for bb in range(nb))


def _ssd_chunk(z_ref, xbc_ref, sm_ref, cw_ref, cb_ref, dtb_ref, alog_ref, dsk_ref, ng_ref, o_ref, xext, st):
    n = CHUNK
    xc = _causal_conv_silu(xbc_ref[...].astype(F32), xext, cw_ref, cb_ref)
    xs = xc[:, 0:D_GROUP]
    bm = xc[:, D_GROUP:D_GROUP + LANES]
    cm = xc[:, D_GROUP + LANES:D_GROUP + 2 * LANES]
    yield

    dt_c = _softplus(sm_ref[...] + dtb_ref[...])
    dtx = _dot_lhs_f32(dt_c, _expand_mat(SM_DT, D_GROUP), terms=2)
    a_x = -jnp.exp(alog_ref[...])
    tri = _tri(n)
    acs = _dot_rhs_f32(tri, dtx * a_x, terms=2)
    acs_rows = _nt_rhs_f32(_row_select(D_GROUP), acs, terms=2)
    acs_last = acs[n - 1:n, :]
    yield

    xdt = xs * dtx
    xdt_b = xdt.astype(BF16)
    bm_b = bm.astype(BF16)
    cm_b = cm.astype(BF16)
    lane = _iota((n, LANES), 1)
    causal = _iota((n, n), 0) >= _iota((n, n), 1)

    y_pairs = []
    for g in range(SSM_GROUPS):
        in_group = (lane // SSM_STATE) == g
        gmat = _nt(jnp.where(in_group, cm, 0.0).astype(BF16), bm_b)
        pair = []
        for h in (2 * g, 2 * g + 1):
            col = acs[:, HEAD_DIM * h:HEAD_DIM * h + 1]
            row = acs_rows[h:h + 1, :]
            decay = jnp.exp(jnp.where(causal, col - row, NEG_INF))
            pair.append(_dot((gmat * decay).astype(BF16), xdt_b[:, g * LANES:(g + 1) * LANES]))
        y_pairs.append(jnp.where(lane < HEAD_DIM, pair[0], pair[1]))
        yield
    y_diag = jnp.concatenate(y_pairs, axis=1)

    state = st[...]
    y_off = jnp.exp(acs) * _dot(cm_b, state.astype(BF16))
    decay_end = jnp.exp(acs_last - acs)
    upd = _dot(bm.T.astype(BF16), (decay_end * xdt).astype(BF16))
    own = (_iota(st.shape, 0) // SSM_STATE) == (_iota(st.shape, 1) // LANES)
    st[...] = jnp.where(own, jnp.exp(acs_last) * state + upd, 0.0)
    yield

    y = (y_diag + y_off + dsk_ref[...] * xs) * _silu(z_ref[...].astype(F32))
    ms = _dot_lhs_f32(y * y, _group_sum_mat(D_GROUP, LANES), terms=2) * (1.0 / LANES)
    o_ref[...] = (y * lax.rsqrt(ms + EPS) * ng_ref[...]).astype(o_ref.dtype)


def _ssd_mixer(proj, small, conv_w, conv_b, dt_bias, a_log, d_skip, norm_g):
    bsz, seq, _ = proj.shape
    n = CHUNK
    dtb = jnp.zeros((1, LANES), F32).at[0, SM_DT:SM_DT + N_HEADS].set(dt_bias)
    alog_x = jnp.repeat(a_log, HEAD_DIM).reshape(1, D_GROUP)
    dsk_x = jnp.repeat(d_skip, HEAD_DIM).reshape(1, D_GROUP)
    full = lambda b, c: (0, 0)
    nb = _seqs_per_step(bsz)
    return pl.pallas_call(
        functools.partial(_ssd_kernel, nb=nb),
        grid=(bsz // nb, seq // n),
        in_specs=[pl.BlockSpec((nb, n, D_GROUP), lambda b, c: (b, c, PB_C_Z // 2)),
                  pl.BlockSpec((nb, n, SSM_XBC), lambda b, c: (b, c, PB_C_XBC * LANES // SSM_XBC)),
                  pl.BlockSpec((nb, n, LANES), lambda b, c: (b, c, 0)),
                  pl.BlockSpec((CONV_WIDTH, SSM_XBC), full),
                  pl.BlockSpec((1, SSM_XBC), full),
                  pl.BlockSpec((1, LANES), full),
                  pl.BlockSpec((1, D_GROUP), full),
                  pl.BlockSpec((1, D_GROUP), full),
                  pl.BlockSpec((1, D_GROUP), full)],
        out_specs=pl.BlockSpec((nb, n, D_GROUP), lambda b, c: (b, c, 0)),
        out_shape=jax.ShapeDtypeStruct((bsz, seq, D_GROUP), BF16),
        scratch_shapes=[pltpu.VMEM((nb, CONV_PAD + n, SSM_XBC), F32), pltpu.VMEM((nb, LANES, D_GROUP), F32)],
        compiler_params=_cparams(("arbitrary", "arbitrary")),
        name="ssd_mixer",
    )(proj, proj, small, conv_w, conv_b.reshape(1, SSM_XBC), dtb, alog_x, dsk_x, norm_g.reshape(1, D_GROUP))


def _mlstm_kernel(qk_ref, v_ref, sm_ref, og_ref, z_ref, cw_ref, cb_ref, ifb_ref, ng_ref, o_ref,
                  xext, c_st, n_st, m_st, *, nb):
    @pl.when(pl.program_id(1) == 0)
    def _():
        c_st[...] = jnp.zeros(c_st.shape, F32)
        n_st[...] = jnp.zeros(n_st.shape, F32)
        m_st[...] = jnp.zeros(m_st.shape, F32)
        xext[:, 0:CONV_PAD, :] = jnp.zeros((nb, CONV_PAD, 2 * D_GROUP), F32)

    _interleave(_mlstm_chunk(qk_ref.at[bb], v_ref.at[bb], sm_ref.at[bb], og_ref.at[bb], z_ref.at[bb], cw_ref,
                             cb_ref, ifb_ref, ng_ref, o_ref.at[bb], xext.at[bb], c_st.at[bb], n_st.at[bb],
                             m_st.at[bb]) for bb in range(nb))


def _mlstm_chunk(qk_ref, v_ref, sm_ref, og_ref, z_ref, cw_ref, cb_ref, ifb_ref, ng_ref, o_ref,
                 xext, c_st, n_st, m_st):
    n = CHUNK
    qk = _causal_conv_silu(qk_ref[...].astype(F32), xext, cw_ref, cb_ref)
    q = qk[:, 0:D_GROUP]
    k = qk[:, D_GROUP:] * (HEAD_DIM ** -0.5)
    q_b = q.astype(BF16)
    k_b = k.astype(BF16)
    v_b = v_ref[...]
    yield

    pre = sm_ref[...] + ifb_ref[...]
    logf = -_softplus(-pre)
    ig = _dot_lhs_f32(pre, _expand_mat(SM_I, D_GROUP), terms=2)
    lf = _dot_lhs_f32(logf, _expand_mat(SM_F, D_GROUP), terms=2)
    tri = _tri(n)
    b = _dot_rhs_f32(tri, lf, terms=2)
    b_last = b[n - 1:n, :]
    sel = _row_select(D_GROUP)
    u_rows = _nt_rhs_f32(sel, ig - b, terms=2)
    yield

    m_prev = m_st[...]
    c_prev = c_st[...]
    n_prev = n_st[...]

    lane = _iota((n, LANES), 1)
    causal = _iota((n, n), 0) >= _iota((n, n), 1)
    ones_b = jnp.ones((n, LANES), BF16)
    num_pairs, den_pairs, mt_pairs, wi_pairs = [], [], [], []
    for pr in range(N_HEADS // 2):
        lanes = slice(pr * LANES, (pr + 1) * LANES)
        qp = q[:, lanes]
        kp_b = k_b[:, lanes]
        rhs = jnp.concatenate([v_b[:, lanes], ones_b], axis=1)
        res, mts, wis = [], [], []
        for hh in range(2):
            h = 2 * pr + hh
            in_head = (lane // HEAD_DIM) == hh
            bcol = b[:, HEAD_DIM * h:HEAD_DIM * h + 1]
            dlog = jnp.where(causal, bcol + u_rows[h:h + 1, :], NEG_INF)
            inter = bcol + m_prev[:, HEAD_DIM * h:HEAD_DIM * h + 1]
            m_t = jnp.maximum(inter, jnp.max(dlog, axis=1, keepdims=True))
            s_qk = _nt(jnp.where(in_head, qp, 0.0).astype(BF16), kp_b) * jnp.exp(dlog - m_t)
            res.append(_dot(s_qk.astype(BF16), rhs))
            mts.append(jnp.broadcast_to(m_t, (n, LANES)))
            wis.append(jnp.broadcast_to(jnp.exp(inter - m_t), (n, LANES)))
            yield
        first = lane < HEAD_DIM
        num_pairs.append(jnp.where(first, res[0][:, :LANES], res[1][:, :LANES]))
        den_pairs.append(jnp.where(first, res[0][:, LANES:], res[1][:, LANES:]))
        mt_pairs.append(jnp.where(first, mts[0], mts[1]))
        wi_pairs.append(jnp.where(first, wis[0], wis[1]))
    num_intra = jnp.concatenate(num_pairs, axis=1)
    den_intra = jnp.concatenate(den_pairs, axis=1)
    m_t = jnp.concatenate(mt_pairs, axis=1)
    w_inter = jnp.concatenate(wi_pairs, axis=1)

    head_sum = _group_sum_mat(D_GROUP, HEAD_DIM)
    num = num_intra + w_inter * _dot(q_b, c_prev.astype(BF16))
    den = den_intra + w_inter * _dot_lhs_f32(q * n_prev, head_sum, terms=2)
    hid = num / jnp.maximum(jnp.abs(den), jnp.exp(-m_t))
    yield

    g_end = b_last - b + ig
    m_loc = jnp.max(g_end, axis=0, keepdims=True)
    m_new = jnp.maximum(b_last + m_prev, m_loc)
    a_prev = jnp.exp(b_last + m_prev - m_new)
    a_loc = jnp.exp(m_loc - m_new)
    kw = k * (jnp.exp(g_end - m_loc) * a_loc)
    kw_t = jnp.concatenate([kw[:, :LANES].T, kw[:, LANES:].T], axis=0)
    upd = _dot(kw_t.astype(BF16), v_b)
    own = (_iota(c_st.shape, 0) // HEAD_DIM) == (_iota(c_st.shape, 1) // HEAD_DIM)
    c_st[...] = jnp.where(own, a_prev * c_prev + upd, 0.0)
    n_st[...] = a_prev * n_prev + jnp.sum(kw, axis=0, keepdims=True)
    m_st[...] = m_new
    yield

    hm = _sigmoid(og_ref[...].astype(F32)) * hid
    ms = _dot_lhs_f32(hm * hm, head_sum, terms=2) * (1.0 / HEAD_DIM)
    o_ref[...] = (hm * lax.rsqrt(ms + EPS) * ng_ref[...] * _silu(z_ref[...].astype(F32))).astype(o_ref.dtype)


def _mlstm_mixer(proj, small, conv_w, conv_b, if_b, norm_g):
    bsz, seq, _ = proj.shape
    n = CHUNK
    ifb = jnp.zeros((1, LANES), F32).at[0, SM_I:SM_I + 2 * N_HEADS].set(if_b)
    ng = jnp.tile(norm_g, N_HEADS).reshape(1, D_GROUP)
    full = lambda b, c: (0, 0)
    nb = _seqs_per_step(bsz)
    return pl.pallas_call(
        functools.partial(_mlstm_kernel, nb=nb),
        grid=(bsz // nb, seq // n),
        in_specs=[pl.BlockSpec((nb, n, 2 * D_GROUP), lambda b, c: (b, c, PB_D_QK * LANES // (2 * D_GROUP))),
                  pl.BlockSpec((nb, n, D_GROUP), lambda b, c: (b, c, PB_D_V // 2)),
                  pl.BlockSpec((nb, n, LANES), lambda b, c: (b, c, 0)),
                  pl.BlockSpec((nb, n, D_GROUP), lambda b, c: (b, c, PB_D_O // 2)),
                  pl.BlockSpec((nb, n, D_GROUP), lambda b, c: (b, c, PB_D_Z // 2)),
                  pl.BlockSpec((CONV_WIDTH, 2 * D_GROUP), full),
                  pl.BlockSpec((1, 2 * D_GROUP), full),
                  pl.BlockSpec((1, LANES), full),
                  pl.BlockSpec((1, D_GROUP), full)],
        out_specs=pl.BlockSpec((nb, n, D_GROUP), lambda b, c: (b, c, 0)),
        out_shape=jax.ShapeDtypeStruct((bsz, seq, D_GROUP), BF16),
        scratch_shapes=[pltpu.VMEM((nb, CONV_PAD + n, 2 * D_GROUP), F32),
                        pltpu.VMEM((nb, D_GROUP, D_GROUP), F32),
                        pltpu.VMEM((nb, 1, D_GROUP), F32),
                        pltpu.VMEM((nb, 1, D_GROUP), F32)],
        compiler_params=_cparams(("arbitrary", "arbitrary")),
        name="mlstm_mixer",
    )(proj, proj, small, proj, proj, conv_w, conv_b.reshape(1, 2 * D_GROUP), ifb, ng)


def _outproj_kernel(ya_ref, yb_ref, yc_ref, yd_ref, x_ref, gate_ref, w_ref, fg_ref, o_ref, *, final):
    acc = _dot(ya_ref[0], w_ref[0:D_GROUP, :])
    acc = acc + _dot(yb_ref[0], w_ref[D_GROUP:2 * D_GROUP, :])
    acc = acc + _dot(yc_ref[0], w_ref[2 * D_GROUP:3 * D_GROUP, :])
    acc = acc + _dot(yd_ref[0], w_ref[3 * D_GROUP:4 * D_GROUP, :])
    out = x_ref[0] + gate_ref[0] * acc
    if final:
        ms = jnp.mean(out * out, axis=-1, keepdims=True)
        out = out * lax.rsqrt(ms + EPS) * fg_ref[...]
    o_ref[0] = out


def _out_projection(ys, x, gate, w_bf, final_g, final):
    bsz, seq, d = x.shape
    tm = min(512, seq)
    yspec = pl.BlockSpec((1, tm, D_GROUP), lambda b, i: (b, i, 0))
    return pl.pallas_call(
        functools.partial(_outproj_kernel, final=final),
        grid=(bsz, seq // tm),
        in_specs=[yspec, yspec, yspec, yspec,
                  pl.BlockSpec((1, tm, d), lambda b, i: (b, i, 0)),
                  pl.BlockSpec((1, 1, d), lambda b, i: (b, 0, 0)),
                  pl.BlockSpec((N_HEADS * D_GROUP, d), lambda b, i: (0, 0)),
                  pl.BlockSpec((1, d), lambda b, i: (0, 0))],
        out_specs=pl.BlockSpec((1, tm, d), lambda b, i: (b, i, 0)),
        out_shape=jax.ShapeDtypeStruct((bsz, seq, d), F32),
        compiler_params=_cparams(("arbitrary", "arbitrary")),
        name="out_projection",
    )(*ys, x, gate, w_bf, final_g.reshape(1, d))


def _cmp_to_sel_t(seq):
    n_cmp = (seq - NSA_CMP_BLOCK) // NSA_CMP_STRIDE + 1
    n_sel = seq // NSA_SEL_BLOCK
    start = np.arange(n_cmp)[:, None] * NSA_CMP_STRIDE
    sel_start = np.arange(n_sel)[None, :] * NSA_SEL_BLOCK
    overlap = np.clip(np.minimum(start + NSA_CMP_BLOCK, sel_start + NSA_SEL_BLOCK)
                      - np.maximum(start, sel_start), 0, None)
    out = np.zeros((n_sel, seq // NSA_CMP_STRIDE), np.float32)
    out[:, :n_cmp] = (overlap / NSA_CMP_BLOCK).T
    return jnp.asarray(out, BF16)


def _mixers(proj, small, layer_idx, p):
    bsz, seq, _ = proj.shape
    n_rows = seq // NSA_CMP_STRIDE
    width = NSA_CMP_STRIDE * HEAD_DIM
    c0 = PB_A_KCVC * LANES
    rk = proj[:, :, c0:c0 + HEAD_DIM].reshape(bsz, n_rows, width)
    rv = proj[:, :, c0 + HEAD_DIM:c0 + 2 * HEAD_DIM].reshape(bsz, n_rows, width)
    pos = p['nsa_cmp_pos'].reshape(1, NSA_CMP_BLOCK * HEAD_DIM)
    ckv, cvt = _nsa_compress(rk, rv, pos, p['nsa_ck_w1'], p['nsa_ck_w2'], p['nsa_cv_w1'], p['nsa_cv_w2'])
    y_a = _nsa_attention(proj, small, ckv, cvt, _cmp_to_sel_t(seq), p['nsa_norm_g'])
    y_b = _diff_attention(proj, p['diff_lam'], p['diff_norm_g'], layer_idx)
    y_c = _ssd_mixer(proj, small, p['ssm_conv_w'], p['ssm_conv_b'], p['ssm_dt_bias'], p['ssm_a_log'],
                     p['ssm_d'], p['ssm_norm_g'])
    y_d = _mlstm_mixer(proj, small, p['ml_conv_w'], p['ml_conv_b'], p['ml_if_b'], p['ml_norm_g'])
    return y_a, y_b, y_c, y_d


_LAYER_PARAMS = ('nsa_cmp_pos', 'nsa_ck_w1', 'nsa_ck_w2', 'nsa_cv_w1', 'nsa_cv_w2', 'nsa_norm_g',
                 'diff_lam', 'diff_norm_g', 'ssm_conv_w', 'ssm_conv_b', 'ssm_dt_bias', 'ssm_a_log',
                 'ssm_d', 'ssm_norm_g', 'ml_conv_w', 'ml_conv_b', 'ml_if_b', 'ml_norm_g')


def kernel(x, c, norm_g, ada_w, ada_b, w_in, w_out, nsa_cmp_pos, nsa_ck_w1, nsa_ck_w2, nsa_cv_w1, nsa_cv_w2, nsa_norm_g, diff_lam, diff_norm_g, ssm_conv_w, ssm_conv_b, ssm_dt_bias, ssm_a_log, ssm_d, ssm_norm_g, ml_conv_w, ml_conv_b, ml_if_b, ml_norm_g, final_g):
    stacked = dict(nsa_cmp_pos=nsa_cmp_pos, nsa_ck_w1=nsa_ck_w1, nsa_ck_w2=nsa_ck_w2, nsa_cv_w1=nsa_cv_w1,
                   nsa_cv_w2=nsa_cv_w2, nsa_norm_g=nsa_norm_g, diff_lam=diff_lam, diff_norm_g=diff_norm_g,
                   ssm_conv_w=ssm_conv_w, ssm_conv_b=ssm_conv_b, ssm_dt_bias=ssm_dt_bias,
                   ssm_a_log=ssm_a_log, ssm_d=ssm_d, ssm_norm_g=ssm_norm_g, ml_conv_w=ml_conv_w,
                   ml_conv_b=ml_conv_b, ml_if_b=ml_if_b, ml_norm_g=ml_norm_g)
    depth = w_in.shape[0]
    bsz, seq, d = x.shape
    mod = _ada_modulation(c, ada_w, ada_b)
    for l in range(depth):
        p = {name: stacked[name][l] for name in _LAYER_PARAMS}
        shift = mod[l, :, 0:d].reshape(bsz, 1, d)
        scale = mod[l, :, d:2 * d].reshape(bsz, 1, d)
        gate = mod[l, :, 2 * d:3 * d].reshape(bsz, 1, d)
        w_bf = _relayout_w_in(w_in[l]).astype(BF16)
        proj, small = _in_projection(x, norm_g[l], scale, shift, w_bf)
        ys = _mixers(proj, small, l, p)
        x = _out_projection(ys, x, gate, w_out[l].astype(BF16), final_g, final=(l == depth - 1))
    return x
```

```python
import functools
import math

import numpy as np
import jax
import jax.numpy as jnp
from jax import lax
from jax.experimental import pallas as pl
from jax.experimental.pallas import tpu as pltpu

F32 = jnp.float32
BF16 = jnp.bfloat16

D_MODEL = 1024
N_HEADS = 4
HEAD_DIM = 64
D_GROUP = N_HEADS * HEAD_DIM
NEG_INF = -1e30
EPS = 1e-6

NSA_CMP_BLOCK = 32
NSA_CMP_STRIDE = 16
NSA_SEL_BLOCK = 64
NSA_TOP_N = 16
NSA_WINDOW = 512
NSA_CMP_HIDDEN = 128
NSA_FORCED_SCORE = 1e4

DIFF_QK_DIM = HEAD_DIM // 2
SSM_STATE = 64
SSM_GROUPS = 2
SSM_XBC = D_GROUP + 2 * SSM_GROUPS * SSM_STATE
CONV_WIDTH = 4
CHUNK = 128
CONV_PAD = 8

LANES = 128
D_PROJ = 4096

_SPLITS = (
    D_GROUP, HEAD_DIM, HEAD_DIM, HEAD_DIM, HEAD_DIM, HEAD_DIM, HEAD_DIM, 3 * N_HEADS, D_GROUP,
    D_GROUP, D_GROUP, D_GROUP, D_GROUP,
    D_GROUP, SSM_XBC, N_HEADS,
    2 * D_GROUP, D_GROUP, 2 * N_HEADS, D_GROUP, D_GROUP,
)
_OFFS = [0] + [int(o) for o in np.cumsum(_SPLITS)]
(_A_Q, _A_KC, _A_VC, _A_KS, _A_VS, _A_KW, _A_VW, _A_G, _A_Z,
 _B_Q, _B_K, _B_V, _B_Z, _C_Z, _C_XBC, _C_DT, _D_QK, _D_V, _D_IF, _D_O, _D_Z) = range(21)

SM_GATE = 0
SM_DT = 12
SM_I = 16
SM_F = 20

PB_A_Q = 0
PB_A_KCVC = 2
PB_A_KSVS = 3
PB_A_KWVW = 4
PB_SMALL = 5
PB_A_Z = 6
PB_B_Q = 8
PB_B_K = 10
PB_B_V = 12
PB_B_Z = 14
PB_C_XBC = 16
PB_C_Z = 20
PB_D_V = 22
PB_D_QK = 24
PB_D_O = 28
PB_D_Z = 30

VMEM_LIMIT = 56 * 1024 * 1024
LOG2E = 1.4426950408889634
SUM_ROWS = 16
WINDOW_SUB = 256


def _cparams(sem):
    return pltpu.CompilerParams(dimension_semantics=sem, vmem_limit_bytes=VMEM_LIMIT)


def _relayout_w_in(w_in):
    def col(i):
        return w_in[:, _OFFS[i]:_OFFS[i + 1]]
    d = w_in.shape[0]
    small = jnp.concatenate(
        [col(_A_G), col(_C_DT), col(_D_IF), jnp.zeros((d, LANES - 24), w_in.dtype)], axis=1)
    parts = [col(_A_Q), col(_A_KC), col(_A_VC), col(_A_KS), col(_A_VS), col(_A_KW), col(_A_VW),
             small, col(_A_Z),
             col(_B_Q), col(_B_K), col(_B_V), col(_B_Z),
             col(_C_XBC), col(_C_Z),
             col(_D_V), col(_D_QK), col(_D_O), col(_D_Z)]
    out = jnp.concatenate(parts, axis=1)
    assert out.shape[1] == D_PROJ
    return out


def _dot(a, b):
    return jnp.dot(a, b, preferred_element_type=F32)


def _nt(a, b):
    return lax.dot_general(a, b, (((1,), (1,)), ((), ())), preferred_element_type=F32)


def _split3(a):
    hi = a.astype(BF16)
    r1 = a - hi.astype(F32)
    mid = r1.astype(BF16)
    lo = (r1 - mid.astype(F32)).astype(BF16)
    return hi, mid, lo


def _dot_lhs_f32(a, b_exact, terms=3):
    hi, mid, lo = _split3(a)
    out = _dot(hi, b_exact) + _dot(mid, b_exact)
    return out + _dot(lo, b_exact) if terms == 3 else out


def _dot_rhs_f32(a_exact, b, terms=3):
    hi, mid, lo = _split3(b)
    out = _dot(a_exact, hi) + _dot(a_exact, mid)
    return out + _dot(a_exact, lo) if terms == 3 else out


def _nt_rhs_f32(a_exact, b, terms=3):
    hi, mid, lo = _split3(b)
    out = _nt(a_exact, hi) + _nt(a_exact, mid)
    return out + _nt(a_exact, lo) if terms == 3 else out


def _seqs_per_step(bsz):
    return math.gcd(bsz, 4)


def _sigmoid(x):
    return 1.0 / (1.0 + jnp.exp(-x))


def _silu(x):
    return x * _sigmoid(x)


def _softplus(x):
    return jnp.maximum(x, 0.0) + jnp.log(1.0 + jnp.exp(-jnp.abs(x)))


def _iota(shape, dim):
    return lax.broadcasted_iota(jnp.int32, shape, dim)


def _ada_kernel(c_ref, w_ref, b_ref, o_ref):
    ca = _silu(c_ref[...])
    w = w_ref[0]
    c_hi, c_mid, _ = _split3(ca)
    w_hi, w_mid, _ = _split3(w)
    acc = _dot(c_hi, w_hi) + _dot(c_hi, w_mid) + _dot(c_mid, w_hi)
    o_ref[0] = acc + b_ref[0]


def _ada_modulation(c, ada_w, ada_b):
    depth, d, d3 = ada_w.shape
    bsz = c.shape[0]
    nb = d3 // d
    return pl.pallas_call(
        _ada_kernel,
        grid=(depth, nb),
        in_specs=[pl.BlockSpec((bsz, d), lambda l, j: (0, 0)),
                  pl.BlockSpec((1, d, d), lambda l, j: (l, 0, j)),
                  pl.BlockSpec((1, 1, d), lambda l, j: (l, 0, j))],
        out_specs=pl.BlockSpec((1, bsz, d), lambda l, j: (l, 0, j)),
        out_shape=jax.ShapeDtypeStruct((depth, bsz, d3), F32),
        compiler_params=_cparams(("arbitrary", "arbitrary")),
        name="ada_modulation",
    )(c, ada_w, ada_b.reshape(depth, 1, d3))


def _inproj_kernel(x_ref, g_ref, sc_ref, sh_ref, w_ref, o_ref, small_ref, *, tn):
    x = x_ref[0]
    ms = jnp.mean(x * x, axis=-1, keepdims=True)
    y = x * lax.rsqrt(ms + EPS) * g_ref[...]
    h = (y * (1.0 + sc_ref[0]) + sh_ref[0]).astype(BF16)
    for j in range(D_PROJ // tn):
        acc = _dot(h, w_ref[:, j * tn:(j + 1) * tn])
        o_ref[0, :, j * tn:(j + 1) * tn] = acc.astype(o_ref.dtype)
        if j == (PB_SMALL * LANES) // tn:
            off = PB_SMALL * LANES - j * tn
            small_ref[0] = acc[:, off:off + LANES]


def _in_projection(x, norm_g, scale, shift, w_bf):
    bsz, seq, d = x.shape
    tm = min(1024, seq)
    tn = 1024
    return pl.pallas_call(
        functools.partial(_inproj_kernel, tn=tn),
        grid=(bsz, seq // tm),
        in_specs=[pl.BlockSpec((1, tm, d), lambda b, i: (b, i, 0)),
                  pl.BlockSpec((1, d), lambda b, i: (0, 0)),
                  pl.BlockSpec((1, 1, d), lambda b, i: (b, 0, 0)),
                  pl.BlockSpec((1, 1, d), lambda b, i: (b, 0, 0)),
                  pl.BlockSpec((d, D_PROJ), lambda b, i: (0, 0))],
        out_specs=[pl.BlockSpec((1, tm, D_PROJ), lambda b, i: (b, i, 0)),
                   pl.BlockSpec((1, tm, LANES), lambda b, i: (b, i, 0))],
        out_shape=[jax.ShapeDtypeStruct((bsz, seq, D_PROJ), BF16),
                   jax.ShapeDtypeStruct((bsz, seq, LANES), F32)],
        compiler_params=_cparams(("arbitrary", "arbitrary")),
        name="in_projection",
    )(x, norm_g.reshape(1, d), scale, shift, w_bf)


def _compress_kernel(rk_ref, rv_ref, pos_ref, kw1_ref, kw2_ref, vw1_ref, vw2_ref, kv_ref, vt_ref):
    half = NSA_CMP_STRIDE * HEAD_DIM
    pos = jnp.broadcast_to(pos_ref[...], (8, 2 * half)).astype(BF16)

    def one(r_ref, w1_ref, w2_ref):
        r = r_ref[0].astype(BF16)
        w1 = w1_ref[...].astype(BF16)
        first = _dot(r, w1[:half])
        second = _dot(r, w1[half:])
        n = first.shape[0]
        second = pltpu.roll(second, n - 1, 0)
        bias = _dot(pos, w1)[0:1]
        hid = _silu(first + second + bias)
        return _dot(hid.astype(BF16), w2_ref[...].astype(BF16))

    kc = one(rk_ref, kw1_ref, kw2_ref)
    vc = one(rv_ref, vw1_ref, vw2_ref)
    kv = jnp.concatenate([kc, vc], axis=1)
    kv_ref[0] = kv.astype(BF16)
    ones = jnp.ones((SUM_ROWS, kv.shape[0]), BF16)
    vt_ref[0] = jnp.concatenate([kv.T[HEAD_DIM:].astype(BF16), ones], axis=0)


def _nsa_compress(rk, rv, pos, kw1, kw2, vw1, vw2):
    bsz, n, width = rk.shape
    full2 = lambda b: (0, 0)
    return pl.pallas_call(
        _compress_kernel,
        grid=(bsz,),
        in_specs=[pl.BlockSpec((1, n, width), lambda b: (b, 0, 0)),
                  pl.BlockSpec((1, n, width), lambda b: (b, 0, 0)),
                  pl.BlockSpec(pos.shape, full2),
                  pl.BlockSpec(kw1.shape, full2), pl.BlockSpec(kw2.shape, full2),
                  pl.BlockSpec(vw1.shape, full2), pl.BlockSpec(vw2.shape, full2)],
        out_specs=[pl.BlockSpec((1, n, LANES), lambda b: (b, 0, 0)),
                   pl.BlockSpec((1, HEAD_DIM + SUM_ROWS, n), lambda b: (b, 0, 0))],
        out_shape=[jax.ShapeDtypeStruct((bsz, n, LANES), BF16),
                   jax.ShapeDtypeStruct((bsz, HEAD_DIM + SUM_ROWS, n), BF16)],
        compiler_params=_cparams(("arbitrary",)),
        name="nsa_compress",
    )(rk, rv, pos, kw1, kw2, vw1, vw2)


def _online_softmax(s, m):
    m_new = jnp.maximum(m, jnp.max(s, axis=0, keepdims=True))
    return m_new, jnp.exp2(s - m_new).astype(BF16), jnp.exp2(m - m_new)


def _nsa_kernel(q_ref, ks_ref, kw_ref, sm_ref, z_ref, ckv_ref, cvt_ref, c2s_ref, ng_ref, o_ref,
                vst, vwt, sel_scr, imp_scr, ms_scr, as_scr, *, seq, tq, tk, top_n):
    i = pl.program_id(1)
    n_cmp = seq // NSA_CMP_STRIDE
    n_sel = seq // NSA_SEL_BLOCK
    sub = min(tq, WINDOW_SUB)
    span = NSA_WINDOW + sub

    @pl.when(i == 0)
    def _():
        ones = jnp.ones((SUM_ROWS, LANES), BF16)
        for blk in range(seq // LANES):
            rows = slice(blk * LANES, (blk + 1) * LANES)
            ks = ks_ref[0, rows, :].astype(F32)
            kw = kw_ref[0, rows, :].astype(F32)
            vst[:, rows] = jnp.concatenate([ks.T[HEAD_DIM:].astype(BF16), ones], axis=0)
            vwt[:, rows] = jnp.concatenate([kw.T[HEAD_DIM:].astype(BF16), ones], axis=0)

    q0 = pl.multiple_of(i * tq, tq)
    q = q_ref[0].astype(F32) * (HEAD_DIM ** -0.5 * LOG2E)
    low = _iota((tq, LANES), 1) < HEAD_DIM
    qh = []
    for pair in range(N_HEADS // 2):
        qp = q[:, pair * LANES:(pair + 1) * LANES]
        qh.append(jnp.where(low, qp, 0.0).astype(BF16))
        qh.append(jnp.where(low, pltpu.roll(qp, HEAD_DIM, 1), 0.0).astype(BF16))

    t_row = q0 + _iota((1, tq), 1)

    ckv = ckv_ref[0]
    cvt = cvt_ref[0]
    cmp_end = _iota((n_cmp, 1), 0) * NSA_CMP_STRIDE + (NSA_CMP_BLOCK - 1)
    valid_c = cmp_end <= t_row
    any_valid = t_row >= NSA_CMP_BLOCK - 1
    heads = range(N_HEADS)
    s_c = [jnp.where(valid_c, _nt(ckv, qh[h]), NEG_INF) for h in heads]
    m_c = [jnp.max(s_c[h], axis=0, keepdims=True) for h in heads]
    p_c = [jnp.exp2(s_c[h] - m_c[h]) for h in heads]
    pv_c = [_dot(cvt, p_c[h].astype(BF16)) for h in heads]
    inv_c = [jnp.where(any_valid, 1.0 / pv_c[h][HEAD_DIM:HEAD_DIM + 1], 0.0) for h in heads]
    o_c = [pv_c[h][:HEAD_DIM] * inv_c[h] for h in heads]
    p_sum = p_c[0] * inv_c[0]
    for h in range(1, N_HEADS):
        p_sum = p_sum + p_c[h] * inv_c[h]

    p_hi = p_sum.astype(BF16)
    p_lo = (p_sum - p_hi.astype(F32)).astype(BF16)
    c2s = c2s_ref[...]
    imp = _dot(c2s, p_hi) + _dot(c2s, p_lo)
    sid = _iota((n_sel, tq), 0)
    cur = t_row // NSA_SEL_BLOCK
    forced = (sid == cur) | (sid == 0)
    imp = jnp.where(forced, NSA_FORCED_SCORE, jnp.where(sid <= cur, imp, -1.0))
    imp_scr[...] = imp
    per_step = tq // NSA_SEL_BLOCK

    def rank_body(jj, rank):
        for u in range(per_step):
            j = jj * per_step + u
            vj = imp_scr[pl.ds(j, 1), :]
            ge = jnp.where(vj >= imp, 1.0, 0.0)
            gt = jnp.where(vj > imp, 1.0, 0.0)
            rank = rank + jnp.where(sid > j, ge, gt)
        return rank

    rank = lax.fori_loop(0, i + 1, rank_body, jnp.zeros((n_sel, tq), F32))
    sel_scr[...] = jnp.where(rank < float(top_n), 1.0, 0.0)

    blocks_per_chunk = tk // NSA_SEL_BLOCK

    def sel_step(kstart, nkeys, blk_off, col0, causal):
        kstart = pl.multiple_of(kstart, nkeys)
        k_c = ks_ref[0, pl.ds(kstart, nkeys), :]
        vt_c = vst[:, pl.ds(kstart, nkeys)]
        sb = pl.multiple_of(kstart // NSA_SEL_BLOCK - blk_off, blocks_per_chunk)
        selc = sel_scr[pl.ds(sb, blocks_per_chunk), col0:]
        rows = [jnp.broadcast_to(selc[r:r + 1, :], (NSA_SEL_BLOCK, tq - col0))
                for r in range(blk_off, blk_off + nkeys // NSA_SEL_BLOCK)]
        valid = jnp.concatenate(rows, axis=0) > 0.5
        if causal:
            valid = valid & (kstart + _iota((nkeys, 1), 0) <= t_row[:, col0:])
        s, p, alpha = [None] * N_HEADS, [None] * N_HEADS, [None] * N_HEADS

        def scores(h):
            return _nt(k_c, qh[h][col0:])

        def update(h):
            as_scr[h, :, col0:] = alpha[h] * as_scr[h, :, col0:] + _dot(vt_c, p[h])

        s[0] = scores(0)
        for h in range(N_HEADS):
            if h + 1 < N_HEADS:
                s[h + 1] = scores(h + 1)
            ms_scr[h, :, col0:], p[h], alpha[h] = _online_softmax(jnp.where(valid, s[h], NEG_INF),
                                                                  ms_scr[h, :, col0:])
            if h >= 1:
                update(h - 1)
        update(N_HEADS - 1)

    ms_scr[...] = jnp.full(ms_scr.shape, NEG_INF, F32)
    as_scr[...] = jnp.zeros(as_scr.shape, F32)

    def full_chunk(c, carry):
        sel_step(c * tk, tk, 0, 0, False)
        return carry

    half_blocks = blocks_per_chunk // 2
    lax.fori_loop(0, q0 // tk, full_chunk, 0)
    sel_step(q0, tk // 2, 0, 0, True)
    sel_step(q0 + tk // 2, tk // 2, half_blocks, tq // 2, True)

    streams = []
    for a in range(tq // sub):
        qs = slice(a * sub, (a + 1) * sub)
        wstart = pl.multiple_of(jnp.maximum(q0 + a * sub - NSA_WINDOW, 0), sub)
        kpos_w = wstart + _iota((span, 1), 0)
        t_sub = t_row[:, qs]
        valid_w = (kpos_w <= t_sub) & (kpos_w > t_sub - NSA_WINDOW)
        streams += [(kw_ref[0, pl.ds(wstart, span), :], vwt[:, pl.ds(wstart, span)], valid_w, qh[h][qs])
                    for h in range(N_HEADS)]
    s_w = [jnp.where(valid, _nt(k_w, q_w), NEG_INF) for k_w, _, valid, q_w in streams]
    m_w = [jnp.max(sw, axis=0, keepdims=True) for sw in s_w]
    p_w = [jnp.exp2(sw - mw).astype(BF16) for sw, mw in zip(s_w, m_w)]
    pv_w = [_dot(st[1], pw) for st, pw in zip(streams, p_w)]
    o_w_parts = [pv[:HEAD_DIM] * (1.0 / pv[HEAD_DIM:HEAD_DIM + 1]) for pv in pv_w]
    o_w = [jnp.concatenate(o_w_parts[h::N_HEADS], axis=1) for h in range(N_HEADS)]

    gates = _sigmoid(sm_ref[0].T[0:16])
    ng = ng_ref[...]
    outs = []
    for h in range(N_HEADS):
        g_c = gates[3 * h:3 * h + 1]
        g_s = gates[3 * h + 1:3 * h + 2]
        g_w = gates[3 * h + 2:3 * h + 3]
        acc_s = as_scr[h]
        o_s = acc_s[:HEAD_DIM] * (1.0 / acc_s[HEAD_DIM:HEAD_DIM + 1])
        oh = g_c * o_c[h] + g_s * o_s + g_w * o_w[h]
        ms = jnp.mean(oh * oh, axis=0, keepdims=True)
        outs.append(oh * lax.rsqrt(ms + EPS) * ng)
    y = jnp.concatenate(
        [jnp.concatenate(outs[0:2], axis=0).T, jnp.concatenate(outs[2:4], axis=0).T], axis=1)
    o_ref[0] = (y * _silu(z_ref[0].astype(F32))).astype(o_ref.dtype)


def _nsa_attention(proj, small, ckv, cvt, c2s_t, norm_g, *, tq=512):
    bsz, seq, _ = proj.shape
    tk = tq
    assert tk // NSA_SEL_BLOCK == 8
    n_cmp = seq // NSA_CMP_STRIDE
    n_sel = seq // NSA_SEL_BLOCK
    top_n = min(NSA_TOP_N, n_sel)
    ng = jnp.broadcast_to(norm_g.reshape(HEAD_DIM, 1), (HEAD_DIM, tq)).astype(F32)
    kern = functools.partial(_nsa_kernel, seq=seq, tq=tq, tk=tk, top_n=top_n)
    return pl.pallas_call(
        kern,
        grid=(bsz, seq // tq),
        in_specs=[pl.BlockSpec((1, tq, 2 * LANES), lambda b, i: (b, i, PB_A_Q // 2)),
                  pl.BlockSpec((1, seq, LANES), lambda b, i: (b, 0, PB_A_KSVS)),
                  pl.BlockSpec((1, seq, LANES), lambda b, i: (b, 0, PB_A_KWVW)),
                  pl.BlockSpec((1, tq, LANES), lambda b, i: (b, i, 0)),
                  pl.BlockSpec((1, tq, 2 * LANES), lambda b, i: (b, i, PB_A_Z // 2)),
                  pl.BlockSpec((1, n_cmp, LANES), lambda b, i: (b, 0, 0)),
                  pl.BlockSpec((1, HEAD_DIM + SUM_ROWS, n_cmp), lambda b, i: (b, 0, 0)),
                  pl.BlockSpec((n_sel, n_cmp), lambda b, i: (0, 0)),
                  pl.BlockSpec((HEAD_DIM, tq), lambda b, i: (0, 0))],
        out_specs=pl.BlockSpec((1, tq, D_GROUP), lambda b, i: (b, i, 0)),
        out_shape=jax.ShapeDtypeStruct((bsz, seq, D_GROUP), BF16),
        scratch_shapes=[pltpu.VMEM((HEAD_DIM + SUM_ROWS, seq), BF16),
                        pltpu.VMEM((HEAD_DIM + SUM_ROWS, seq), BF16),
                        pltpu.VMEM((n_sel, tq), F32), pltpu.VMEM((n_sel, tq), F32),
                        pltpu.VMEM((N_HEADS, 1, tq), F32),
                        pltpu.VMEM((N_HEADS, HEAD_DIM + SUM_ROWS, tq), F32)],
        compiler_params=_cparams(("arbitrary", "arbitrary")),
        name="nsa_attention",
    )(proj, proj, proj, small, proj, ckv, cvt, c2s_t, ng)


def _diff_kernel(q_ref, k_ref, v_ref, z_ref, lam_ref, ng_ref, o_ref, vt, m_scr, acc_scr, *, seq, tq, tk,
                 lambda_init):
    i = pl.program_id(1)
    vrows = HEAD_DIM + SUM_ROWS
    cols = 2 * tq
    hq = tq // 2

    @pl.when(i == 0)
    def _():
        ones = jnp.ones((SUM_ROWS, LANES), BF16)
        for blk in range(seq // LANES):
            rows = slice(blk * LANES, (blk + 1) * LANES)
            parts = []
            for pair in range(N_HEADS // 2):
                v_t = v_ref[0, rows, pair * LANES:(pair + 1) * LANES].astype(F32).T.astype(BF16)
                parts += [v_t[:HEAD_DIM], ones, v_t[HEAD_DIM:], ones]
            vt[:, rows] = jnp.concatenate(parts, axis=0)

    lam = lam_ref[...]
    lam_full = (jnp.exp(jnp.sum(lam[0:1] * lam[1:2], axis=1, keepdims=True))
                - jnp.exp(jnp.sum(lam[2:3] * lam[3:4], axis=1, keepdims=True)) + lambda_init)

    q0 = pl.multiple_of(i * tq, tq)
    q = q_ref[0].astype(F32) * (DIFF_QK_DIM ** -0.5 * LOG2E)
    part = _iota((tq, LANES), 1) // DIFF_QK_DIM
    q2 = []
    for h in range(N_HEADS):
        qp = q[:, (h // 2) * LANES:(h // 2 + 1) * LANES]
        m0, m1 = [jnp.where(part == 2 * (h % 2) + c, qp, 0.0).astype(BF16) for c in range(2)]
        q2.append(jnp.concatenate([m0[:hq], m1[:hq], m0[hq:], m1[hq:]], axis=0))
    t_half = q0 + _iota((1, hq), 1)
    t_cols = jnp.concatenate([t_half, t_half, t_half + hq, t_half + hq], axis=1)

    def softmax(s, m):
        m_new = jnp.maximum(m, jnp.max(s, axis=0, keepdims=True))
        return m_new, jnp.exp2(s - m_new).astype(BF16), jnp.exp2(m - m_new)

    def step(kstart, nkeys, col0, causal):
        kstart = pl.multiple_of(kstart, nkeys)
        kpos = kstart + _iota((nkeys, 1), 0)
        t_c = t_cols[:, col0:]
        s, p, alpha = [None] * N_HEADS, [None] * N_HEADS, [None] * N_HEADS

        def scores(h):
            lanes = slice((h // 2) * LANES, (h // 2 + 1) * LANES)
            return _nt(k_ref[0, pl.ds(kstart, nkeys), lanes], q2[h][col0:])

        def update(h):
            pv = _dot(vt[h * vrows:(h + 1) * vrows, pl.ds(kstart, nkeys)], p[h])
            acc_scr[h, :, col0:] = alpha[h] * acc_scr[h, :, col0:] + pv

        s[0] = scores(0)
        for h in range(N_HEADS):
            if h + 1 < N_HEADS:
                s[h + 1] = scores(h + 1)
            sh = jnp.where(kpos <= t_c, s[h], NEG_INF) if causal else s[h]
            m_scr[h, :, col0:], p[h], alpha[h] = softmax(sh, m_scr[h, :, col0:])
            if h >= 1:
                update(h - 1)
        update(N_HEADS - 1)

    m_scr[...] = jnp.full(m_scr.shape, NEG_INF, F32)
    acc_scr[...] = jnp.zeros(acc_scr.shape, F32)

    def full_chunk(c, carry):
        step(c * tk, tk, 0, False)
        return carry

    lax.fori_loop(0, q0 // tk, full_chunk, 0)
    step(q0, tk // 2, 0, True)
    step(q0 + tk // 2, tk // 2, cols // 2, True)

    ng = ng_ref[...]
    outs = []
    for h in range(N_HEADS):
        acc = acc_scr[h]
        o_all = acc[:HEAD_DIM] * (1.0 / acc[HEAD_DIM:HEAD_DIM + 1])
        map0 = jnp.concatenate([o_all[:, 0:hq], o_all[:, 2 * hq:3 * hq]], axis=1)
        map1 = jnp.concatenate([o_all[:, hq:2 * hq], o_all[:, 3 * hq:4 * hq]], axis=1)
        o = map0 - lam_full * map1
        ms = jnp.mean(o * o, axis=0, keepdims=True)
        outs.append(o * lax.rsqrt(ms + EPS) * ng * (1.0 - lambda_init))
    y = jnp.concatenate(
        [jnp.concatenate(outs[0:2], axis=0).T, jnp.concatenate(outs[2:4], axis=0).T], axis=1)
    o_ref[0] = (y * _silu(z_ref[0].astype(F32))).astype(o_ref.dtype)


def _diff_attention(proj, lam, norm_g, layer_idx, *, tq=512):
    bsz, seq, _ = proj.shape
    tk = tq
    lambda_init = 0.8 - 0.6 * math.exp(-0.3 * layer_idx)
    ng = jnp.broadcast_to(norm_g.reshape(HEAD_DIM, 1), (HEAD_DIM, tq)).astype(F32)
    kern = functools.partial(_diff_kernel, seq=seq, tq=tq, tk=tk, lambda_init=lambda_init)
    return pl.pallas_call(
        kern,
        grid=(bsz, seq // tq),
        in_specs=[pl.BlockSpec((1, tq, D_GROUP), lambda b, i: (b, i, PB_B_Q // 2)),
                  pl.BlockSpec((1, seq, D_GROUP), lambda b, i: (b, 0, PB_B_K // 2)),
                  pl.BlockSpec((1, seq, D_GROUP), lambda b, i: (b, 0, PB_B_V // 2)),
                  pl.BlockSpec((1, tq, D_GROUP), lambda b, i: (b, i, PB_B_Z // 2)),
                  pl.BlockSpec((4, DIFF_QK_DIM), lambda b, i: (0, 0)),
                  pl.BlockSpec((HEAD_DIM, tq), lambda b, i: (0, 0))],
        out_specs=pl.BlockSpec((1, tq, D_GROUP), lambda b, i: (b, i, 0)),
        out_shape=jax.ShapeDtypeStruct((bsz, seq, D_GROUP), BF16),
        scratch_shapes=[pltpu.VMEM((N_HEADS * (HEAD_DIM + SUM_ROWS), seq), BF16),
                        pltpu.VMEM((N_HEADS, 1, 2 * tq), F32),
                        pltpu.VMEM((N_HEADS, HEAD_DIM + SUM_ROWS, 2 * tq), F32)],
        compiler_params=_cparams(("arbitrary", "arbitrary")),
        name="diff_attention",
    )(proj, proj, proj, proj, lam, ng)


def _causal_conv_silu(x_raw, xext, cw_ref, cb_ref):
    n = x_raw.shape[0]
    xext[CONV_PAD:CONV_PAD + n, :] = x_raw
    cw = cw_ref[...]
    acc = cb_ref[...] + cw[0:1] * xext[pl.ds(CONV_PAD - 3, n), :]
    for k in range(1, CONV_WIDTH):
        acc = acc + cw[k:k + 1] * xext[pl.ds(CONV_PAD - 3 + k, n), :]
    xext[0:CONV_PAD, :] = x_raw[n - CONV_PAD:, :]
    return _silu(acc)


def _interleave(chunks):
    live = list(chunks)
    while live:
        nxt = []
        for g in live:
            try:
                next(g)
                nxt.append(g)
            except StopIteration:
                pass
        live = nxt


def _tri(n):
    return jnp.where(_iota((n, n), 0) >= _iota((n, n), 1), 1.0, 0.0).astype(BF16)


def _expand_mat(first_lane, width):
    r = _iota((LANES, width), 0)
    c = _iota((LANES, width), 1)
    return jnp.where(r - first_lane == c // HEAD_DIM, 1.0, 0.0).astype(BF16)


def _row_select(width):
    r = _iota((8, width), 0)
    c = _iota((8, width), 1)
    return jnp.where(c == r * HEAD_DIM, 1.0, 0.0).astype(BF16)


def _group_sum_mat(width, group):
    r = _iota((width, width), 0)
    c = _iota((width, width), 1)
    return jnp.where(r // group == c // group, 1.0, 0.0).astype(BF16)


def _ssd_kernel(z_ref, xbc_ref, sm_ref, cw_ref, cb_ref, dtb_ref, alog_ref, dsk_ref, ng_ref, o_ref,
                xext, st, *, nb):
    @pl.when(pl.program_id(1) == 0)
    def _():
        st[...] = jnp.zeros(st.shape, F32)
        xext[:, 0:CONV_PAD, :] = jnp.zeros((nb, CONV_PAD, SSM_XBC), F32)

    _interleave(_ssd_chunk(z_ref.at[bb], xbc_ref.at[bb], sm_ref.at[bb], cw_ref, cb_ref, dtb_ref, alog_ref,
                           dsk_ref, ng_ref, o_ref.at[bb], xext.at[bb], st.at[bb]) for bb in range(nb))


def _ssd_chunk(z_ref, xbc_ref, sm_ref, cw_ref, cb_ref, dtb_ref, alog_ref, dsk_ref, ng_ref, o_ref, xext, st):
    n = CHUNK
    xc = _causal_conv_silu(xbc_ref[...].astype(F32), xext, cw_ref, cb_ref)
    xs = xc[:, 0:D_GROUP]
    bm = xc[:, D_GROUP:D_GROUP + LANES]
    cm = xc[:, D_GROUP + LANES:D_GROUP + 2 * LANES]
    yield

    dt_c = _softplus(sm_ref[...] + dtb_ref[...])
    dtx = _dot_lhs_f32(dt_c, _expand_mat(SM_DT, D_GROUP), terms=2)
    a_x = -jnp.exp(alog_ref[...])
    tri = _tri(n)
    acs = _dot_rhs_f32(tri, dtx * a_x, terms=2)
    acs_rows = _nt_rhs_f32(_row_select(D_GROUP), acs, terms=2)
    acs_last = acs[n - 1:n, :]
    yield

    xdt = xs * dtx
    xdt_b = xdt.astype(BF16)
    bm_b = bm.astype(BF16)
    cm_b = cm.astype(BF16)
    lane = _iota((n, LANES), 1)
    causal = _iota((n, n), 0) >= _iota((n, n), 1)

    y_pairs = []
    for g in range(SSM_GROUPS):
        in_group = (lane // SSM_STATE) == g
        gmat = _nt(jnp.where(in_group, cm, 0.0).astype(BF16), bm_b)
        pair = []
        for h in (2 * g, 2 * g + 1):
            col = acs[:, HEAD_DIM * h:HEAD_DIM * h + 1]
            row = acs_rows[h:h + 1, :]
            decay = jnp.exp(jnp.where(causal, col - row, NEG_INF))
            pair.append(_dot((gmat * decay).astype(BF16), xdt_b[:, g * LANES:(g + 1) * LANES]))
        y_pairs.append(jnp.where(lane < HEAD_DIM, pair[0], pair[1]))
        yield
    y_diag = jnp.concatenate(y_pairs, axis=1)

    state = st[...]
    y_off = jnp.exp(acs) * _dot(cm_b, state.astype(BF16))
    decay_end = jnp.exp(acs_last - acs)
    upd = _dot(bm.T.astype(BF16), (decay_end * xdt).astype(BF16))
    own = (_iota(st.shape, 0) // SSM_STATE) == (_iota(st.shape, 1) // LANES)
    st[...] = jnp.where(own, jnp.exp(acs_last) * state + upd, 0.0)
    yield

    y = (y_diag + y_off + dsk_ref[...] * xs) * _silu(z_ref[...].astype(F32))
    ms = _dot_lhs_f32(y * y, _group_sum_mat(D_GROUP, LANES), terms=2) * (1.0 / LANES)
    o_ref[...] = (y * lax.rsqrt(ms + EPS) * ng_ref[...]).astype(o_ref.dtype)


def _ssd_mixer(proj, small, conv_w, conv_b, dt_bias, a_log, d_skip, norm_g):
    bsz, seq, _ = proj.shape
    n = CHUNK
    dtb = jnp.zeros((1, LANES), F32).at[0, SM_DT:SM_DT + N_HEADS].set(dt_bias)
    alog_x = jnp.repeat(a_log, HEAD_DIM).reshape(1, D_GROUP)
    dsk_x = jnp.repeat(d_skip, HEAD_DIM).reshape(1, D_GROUP)
    full = lambda b, c: (0, 0)
    nb = _seqs_per_step(bsz)
    return pl.pallas_call(
        functools.partial(_ssd_kernel, nb=nb),
        grid=(bsz // nb, seq // n),
        in_specs=[pl.BlockSpec((nb, n, D_GROUP), lambda b, c: (b, c, PB_C_Z // 2)),
                  pl.BlockSpec((nb, n, SSM_XBC), lambda b, c: (b, c, PB_C_XBC * LANES // SSM_XBC)),
                  pl.BlockSpec((nb, n, LANES), lambda b, c: (b, c, 0)),
                  pl.BlockSpec((CONV_WIDTH, SSM_XBC), full),
                  pl.BlockSpec((1, SSM_XBC), full),
                  pl.BlockSpec((1, LANES), full),
                  pl.BlockSpec((1, D_GROUP), full),
                  pl.BlockSpec((1, D_GROUP), full),
                  pl.BlockSpec((1, D_GROUP), full)],
        out_specs=pl.BlockSpec((nb, n, D_GROUP), lambda b, c: (b, c, 0)),
        out_shape=jax.ShapeDtypeStruct((bsz, seq, D_GROUP), BF16),
        scratch_shapes=[pltpu.VMEM((nb, CONV_PAD + n, SSM_XBC), F32), pltpu.VMEM((nb, LANES, D_GROUP), F32)],
        compiler_params=_cparams(("arbitrary", "arbitrary")),
        name="ssd_mixer",
    )(proj, proj, small, conv_w, conv_b.reshape(1, SSM_XBC), dtb, alog_x, dsk_x, norm_g.reshape(1, D_GROUP))


def _mlstm_kernel(qk_ref, v_ref, sm_ref, og_ref, z_ref, cw_ref, cb_ref, ifb_ref, ng_ref, o_ref,
                  xext, c_st, n_st, m_st, *, nb):
    @pl.when(pl.program_id(1) == 0)
    def _():
        c_st[...] = jnp.zeros(c_st.shape, F32)
        n_st[...] = jnp.zeros(n_st.shape, F32)
        m_st[...] = jnp.zeros(m_st.shape, F32)
        xext[:, 0:CONV_PAD, :] = jnp.zeros((nb, CONV_PAD, 2 * D_GROUP), F32)

    _interleave(_mlstm_chunk(qk_ref.at[bb], v_ref.at[bb], sm_ref.at[bb], og_ref.at[bb], z_ref.at[bb], cw_ref,
                             cb_ref, ifb_ref, ng_ref, o_ref.at[bb], xext.at[bb], c_st.at[bb], n_st.at[bb],
                             m_st.at[bb]) for bb in range(nb))


def _mlstm_chunk(qk_ref, v_ref, sm_ref, og_ref, z_ref, cw_ref, cb_ref, ifb_ref, ng_ref, o_ref,
                 xext, c_st, n_st, m_st):
    n = CHUNK
    qk = _causal_conv_silu(qk_ref[...].astype(F32), xext, cw_ref, cb_ref)
    q = qk[:, 0:D_GROUP]
    k = qk[:, D_GROUP:] * (HEAD_DIM ** -0.5)
    q_b = q.astype(BF16)
    k_b = k.astype(BF16)
    v_b = v_ref[...]
    yield

    pre = sm_ref[...] + ifb_ref[...]
    logf = -_softplus(-pre)
    ig = _dot_lhs_f32(pre, _expand_mat(SM_I, D_GROUP), terms=2)
    lf = _dot_lhs_f32(logf, _expand_mat(SM_F, D_GROUP), terms=2)
    tri = _tri(n)
    b = _dot_rhs_f32(tri, lf, terms=2)
    b_last = b[n - 1:n, :]
    sel = _row_select(D_GROUP)
    u_rows = _nt_rhs_f32(sel, ig - b, terms=2)
    yield

    m_prev = m_st[...]
    c_prev = c_st[...]
    n_prev = n_st[...]

    lane = _iota((n, LANES), 1)
    causal = _iota((n, n), 0) >= _iota((n, n), 1)
    ones_b = jnp.ones((n, LANES), BF16)
    num_pairs, den_pairs, mt_pairs, wi_pairs = [], [], [], []
    for pr in range(N_HEADS // 2):
        lanes = slice(pr * LANES, (pr + 1) * LANES)
        qp = q[:, lanes]
        kp_b = k_b[:, lanes]
        rhs = jnp.concatenate([v_b[:, lanes], ones_b], axis=1)
        res, mts, wis = [], [], []
        for hh in range(2):
            h = 2 * pr + hh
            in_head = (lane // HEAD_DIM) == hh
            bcol = b[:, HEAD_DIM * h:HEAD_DIM * h + 1]
            dlog = jnp.where(causal, bcol + u_rows[h:h + 1, :], NEG_INF)
            inter = bcol + m_prev[:, HEAD_DIM * h:HEAD_DIM * h + 1]
            m_t = jnp.maximum(inter, jnp.max(dlog, axis=1, keepdims=True))
            s_qk = _nt(jnp.where(in_head, qp, 0.0).astype(BF16), kp_b) * jnp.exp(dlog - m_t)
            res.append(_dot(s_qk.astype(BF16), rhs))
            mts.append(jnp.broadcast_to(m_t, (n, LANES)))
            wis.append(jnp.broadcast_to(jnp.exp(inter - m_t), (n, LANES)))
            yield
        first = lane < HEAD_DIM
        num_pairs.append(jnp.where(first, res[0][:, :LANES], res[1][:, :LANES]))
        den_pairs.append(jnp.where(first, res[0][:, LANES:], res[1][:, LANES:]))
        mt_pairs.append(jnp.where(first, mts[0], mts[1]))
        wi_pairs.append(jnp.where(first, wis[0], wis[1]))
    num_intra = jnp.concatenate(num_pairs, axis=1)
    den_intra = jnp.concatenate(den_pairs, axis=1)
    m_t = jnp.concatenate(mt_pairs, axis=1)
    w_inter = jnp.concatenate(wi_pairs, axis=1)

    head_sum = _group_sum_mat(D_GROUP, HEAD_DIM)
    num = num_intra + w_inter * _dot(q_b, c_prev.astype(BF16))
    den = den_intra + w_inter * _dot_lhs_f32(q * n_prev, head_sum, terms=2)
    hid = num / jnp.maximum(jnp.abs(den), jnp.exp(-m_t))
    yield

    g_end = b_last - b + ig
    m_loc = jnp.max(g_end, axis=0, keepdims=True)
    m_new = jnp.maximum(b_last + m_prev, m_loc)
    a_prev = jnp.exp(b_last + m_prev - m_new)
    a_loc = jnp.exp(m_loc - m_new)
    kw = k * (jnp.exp(g_end - m_loc) * a_loc)
    kw_t = jnp.concatenate([kw[:, :LANES].T, kw[:, LANES:].T], axis=0)
    upd = _dot(kw_t.astype(BF16), v_b)
    own = (_iota(c_st.shape, 0) // HEAD_DIM) == (_iota(c_st.shape, 1) // HEAD_DIM)
    c_st[...] = jnp.where(own, a_prev * c_prev + upd, 0.0)
    n_st[...] = a_prev * n_prev + jnp.sum(kw, axis=0, keepdims=True)
    m_st[...] = m_new
    yield

    hm = _sigmoid(og_ref[...].astype(F32)) * hid
    ms = _dot_lhs_f32(hm * hm, head_sum, terms=2) * (1.0 / HEAD_DIM)
    o_ref[...] = (hm * lax.rsqrt(ms + EPS) * ng_ref[...] * _silu(z_ref[...].astype(F32))).astype(o_ref.dtype)


def _mlstm_mixer(proj, small, conv_w, conv_b, if_b, norm_g):
    bsz, seq, _ = proj.shape
    n = CHUNK
    ifb = jnp.zeros((1, LANES), F32).at[0, SM_I:SM_I + 2 * N_HEADS].set(if_b)
    ng = jnp.tile(norm_g, N_HEADS).reshape(1, D_GROUP)
    full = lambda b, c: (0, 0)
    nb = _seqs_per_step(bsz)
    return pl.pallas_call(
        functools.partial(_mlstm_kernel, nb=nb),
        grid=(bsz // nb, seq // n),
        in_specs=[pl.BlockSpec((nb, n, 2 * D_GROUP), lambda b, c: (b, c, PB_D_QK * LANES // (2 * D_GROUP))),
                  pl.BlockSpec((nb, n, D_GROUP), lambda b, c: (b, c, PB_D_V // 2)),
                  pl.BlockSpec((nb, n, LANES), lambda b, c: (b, c, 0)),
                  pl.BlockSpec((nb, n, D_GROUP), lambda b, c: (b, c, PB_D_O // 2)),
                  pl.BlockSpec((nb, n, D_GROUP), lambda b, c: (b, c, PB_D_Z // 2)),
                  pl.BlockSpec((CONV_WIDTH, 2 * D_GROUP), full),
                  pl.BlockSpec((1, 2 * D_GROUP), full),
                  pl.BlockSpec((1, LANES), full),
                  pl.BlockSpec((1, D_GROUP), full)],
        out_specs=pl.BlockSpec((nb, n, D_GROUP), lambda b, c: (b, c, 0)),
        out_shape=jax.ShapeDtypeStruct((bsz, seq, D_GROUP), BF16),
        scratch_shapes=[pltpu.VMEM((nb, CONV_PAD + n, 2 * D_GROUP), F32),
                        pltpu.VMEM((nb, D_GROUP, D_GROUP), F32),
                        pltpu.VMEM((nb, 1, D_GROUP), F32),
                        pltpu.VMEM((nb, 1, D_GROUP), F32)],
        compiler_params=_cparams(("arbitrary", "arbitrary")),
        name="mlstm_mixer",
    )(proj, proj, small, proj, proj, conv_w, conv_b.reshape(1, 2 * D_GROUP), ifb, ng)


def _outproj_kernel(ya_ref, yb_ref, yc_ref, yd_ref, x_ref, gate_ref, w_ref, fg_ref, o_ref, *, final):
    acc = _dot(ya_ref[0], w_ref[0:D_GROUP, :])
    acc = acc + _dot(yb_ref[0], w_ref[D_GROUP:2 * D_GROUP, :])
    acc = acc + _dot(yc_ref[0], w_ref[2 * D_GROUP:3 * D_GROUP, :])
    acc = acc + _dot(yd_ref[0], w_ref[3 * D_GROUP:4 * D_GROUP, :])
    out = x_ref[0] + gate_ref[0] * acc
    if final:
        ms = jnp.mean(out * out, axis=-1, keepdims=True)
        out = out * lax.rsqrt(ms + EPS) * fg_ref[...]
    o_ref[0] = out


def _out_projection(ys, x, gate, w_bf, final_g, final):
    bsz, seq, d = x.shape
    tm = min(512, seq)
    yspec = pl.BlockSpec((1, tm, D_GROUP), lambda b, i: (b, i, 0))
    return pl.pallas_call(
        functools.partial(_outproj_kernel, final=final),
        grid=(bsz, seq // tm),
        in_specs=[yspec, yspec, yspec, yspec,
                  pl.BlockSpec((1, tm, d), lambda b, i: (b, i, 0)),
                  pl.BlockSpec((1, 1, d), lambda b, i: (b, 0, 0)),
                  pl.BlockSpec((N_HEADS * D_GROUP, d), lambda b, i: (0, 0)),
                  pl.BlockSpec((1, d), lambda b, i: (0, 0))],
        out_specs=pl.BlockSpec((1, tm, d), lambda b, i: (b, i, 0)),
        out_shape=jax.ShapeDtypeStruct((bsz, seq, d), F32),
        compiler_params=_cparams(("arbitrary", "arbitrary")),
        name="out_projection",
    )(*ys, x, gate, w_bf, final_g.reshape(1, d))


def _cmp_to_sel_t(seq):
    n_cmp = (seq - NSA_CMP_BLOCK) // NSA_CMP_STRIDE + 1
    n_sel = seq // NSA_SEL_BLOCK
    start = np.arange(n_cmp)[:, None] * NSA_CMP_STRIDE
    sel_start = np.arange(n_sel)[None, :] * NSA_SEL_BLOCK
    overlap = np.clip(np.minimum(start + NSA_CMP_BLOCK, sel_start + NSA_SEL_BLOCK)
                      - np.maximum(start, sel_start), 0, None)
    out = np.zeros((n_sel, seq // NSA_CMP_STRIDE), np.float32)
    out[:, :n_cmp] = (overlap / NSA_CMP_BLOCK).T
    return jnp.asarray(out, BF16)


def _mixers(proj, small, layer_idx, p):
    bsz, seq, _ = proj.shape
    n_rows = seq // NSA_CMP_STRIDE
    width = NSA_CMP_STRIDE * HEAD_DIM
    c0 = PB_A_KCVC * LANES
    rk = proj[:, :, c0:c0 + HEAD_DIM].reshape(bsz, n_rows, width)
    rv = proj[:, :, c0 + HEAD_DIM:c0 + 2 * HEAD_DIM].reshape(bsz, n_rows, width)
    pos = p['nsa_cmp_pos'].reshape(1, NSA_CMP_BLOCK * HEAD_DIM)
    ckv, cvt = _nsa_compress(rk, rv, pos, p['nsa_ck_w1'], p['nsa_ck_w2'], p['nsa_cv_w1'], p['nsa_cv_w2'])
    y_a = _nsa_attention(proj, small, ckv, cvt, _cmp_to_sel_t(seq), p['nsa_norm_g'])
    y_b = _diff_attention(proj, p['diff_lam'], p['diff_norm_g'], layer_idx)
    y_c = _ssd_mixer(proj, small, p['ssm_conv_w'], p['ssm_conv_b'], p['ssm_dt_bias'], p['ssm_a_log'],
                     p['ssm_d'], p['ssm_norm_g'])
    y_d = _mlstm_mixer(proj, small, p['ml_conv_w'], p['ml_conv_b'], p['ml_if_b'], p['ml_norm_g'])
    return y_a, y_b, y_c, y_d


_LAYER_PARAMS = ('nsa_cmp_pos', 'nsa_ck_w1', 'nsa_ck_w2', 'nsa_cv_w1', 'nsa_cv_w2', 'nsa_norm_g',
                 'diff_lam', 'diff_norm_g', 'ssm_conv_w', 'ssm_conv_b', 'ssm_dt_bias', 'ssm_a_log',
                 'ssm_d', 'ssm_norm_g', 'ml_conv_w', 'ml_conv_b', 'ml_if_b', 'ml_norm_g')


def kernel(x, c, norm_g, ada_w, ada_b, w_in, w_out, nsa_cmp_pos, nsa_ck_w1, nsa_ck_w2, nsa_cv_w1, nsa_cv_w2, nsa_norm_g, diff_lam, diff_norm_g, ssm_conv_w, ssm_conv_b, ssm_dt_bias, ssm_a_log, ssm_d, ssm_norm_g, ml_conv_w, ml_conv_b, ml_if_b, ml_norm_g, final_g):
    stacked = dict(nsa_cmp_pos=nsa_cmp_pos, nsa_ck_w1=nsa_ck_w1, nsa_ck_w2=nsa_ck_w2, nsa_cv_w1=nsa_cv_w1,
                   nsa_cv_w2=nsa_cv_w2, nsa_norm_g=nsa_norm_g, diff_lam=diff_lam, diff_norm_g=diff_norm_g,
                   ssm_conv_w=ssm_conv_w, ssm_conv_b=ssm_conv_b, ssm_dt_bias=ssm_dt_bias,
                   ssm_a_log=ssm_a_log, ssm_d=ssm_d, ssm_norm_g=ssm_norm_g, ml_conv_w=ml_conv_w,
                   ml_conv_b=ml_conv_b, ml_if_b=ml_if_b, ml_norm_g=ml_norm_g)
    depth = w_in.shape[0]
    bsz, seq, d = x.shape
    mod = _ada_modulation(c, ada_w, ada_b)
    for l in range(depth):
        p = {name: stacked[name][l] for name in _LAYER_PARAMS}
        shift = mod[l, :, 0:d].reshape(bsz, 1, d)
        scale = mod[l, :, d:2 * d].reshape(bsz, 1, d)
        gate = mod[l, :, 2 * d:3 * d].reshape(bsz, 1, d)
        w_bf = _relayout_w_in(w_in[l]).astype(BF16)
        proj, small = _in_projection(x, norm_g[l], scale, shift, w_bf)
        ys = _mixers(proj, small, l, p)
        x = _out_projection(ys, x, gate, w_out[l].astype(BF16), final_g, final=(l == depth - 1))
    return x
```

```python
import functools
import math

import numpy as np
import jax
import jax.numpy as jnp
from jax import lax
from jax.experimental import pallas as pl
from jax.experimental.pallas import tpu as pltpu

F32 = jnp.float32
BF16 = jnp.bfloat16

D_MODEL = 1024
N_HEADS = 4
HEAD_DIM = 64
D_GROUP = N_HEADS * HEAD_DIM
NEG_INF = -1e30
EPS = 1e-6

NSA_CMP_BLOCK = 32
NSA_CMP_STRIDE = 16
NSA_SEL_BLOCK = 64
NSA_TOP_N = 16
NSA_WINDOW = 512
NSA_CMP_HIDDEN = 128
NSA_FORCED_SCORE = 1e4

DIFF_QK_DIM = HEAD_DIM // 2
SSM_STATE = 64
SSM_GROUPS = 2
SSM_XBC = D_GROUP + 2 * SSM_GROUPS * SSM_STATE
CONV_WIDTH = 4
CHUNK = 128
CONV_PAD = 8

LANES = 128
D_PROJ = 4096

_SPLITS = (
    D_GROUP, HEAD_DIM, HEAD_DIM, HEAD_DIM, HEAD_DIM, HEAD_DIM, HEAD_DIM, 3 * N_HEADS, D_GROUP,
    D_GROUP, D_GROUP, D_GROUP, D_GROUP,
    D_GROUP, SSM_XBC, N_HEADS,
    2 * D_GROUP, D_GROUP, 2 * N_HEADS, D_GROUP, D_GROUP,
)
_OFFS = [0] + [int(o) for o in np.cumsum(_SPLITS)]
(_A_Q, _A_KC, _A_VC, _A_KS, _A_VS, _A_KW, _A_VW, _A_G, _A_Z,
 _B_Q, _B_K, _B_V, _B_Z, _C_Z, _C_XBC, _C_DT, _D_QK, _D_V, _D_IF, _D_O, _D_Z) = range(21)

SM_GATE = 0
SM_DT = 12
SM_I = 16
SM_F = 20

PB_A_Q = 0
PB_A_KCVC = 2
PB_A_KSVS = 3
PB_A_KWVW = 4
PB_SMALL = 5
PB_A_Z = 6
PB_B_Q = 8
PB_B_K = 10
PB_B_V = 12
PB_B_Z = 14
PB_C_XBC = 16
PB_C_Z = 20
PB_D_V = 22
PB_D_QK = 24
PB_D_O = 28
PB_D_Z = 30

VMEM_LIMIT = 56 * 1024 * 1024
LOG2E = 1.4426950408889634
SUM_ROWS = 16
WINDOW_SUB = 256


def _cparams(sem):
    return pltpu.CompilerParams(dimension_semantics=sem, vmem_limit_bytes=VMEM_LIMIT)


def _relayout_w_in(w_in):
    def col(i):
        return w_in[:, _OFFS[i]:_OFFS[i + 1]]
    d = w_in.shape[0]
    small = jnp.concatenate(
        [col(_A_G), col(_C_DT), col(_D_IF), jnp.zeros((d, LANES - 24), w_in.dtype)], axis=1)
    parts = [col(_A_Q), col(_A_KC), col(_A_VC), col(_A_KS), col(_A_VS), col(_A_KW), col(_A_VW),
             small, col(_A_Z),
             col(_B_Q), col(_B_K), col(_B_V), col(_B_Z),
             col(_C_XBC), col(_C_Z),
             col(_D_V), col(_D_QK), col(_D_O), col(_D_Z)]
    out = jnp.concatenate(parts, axis=1)
    assert out.shape[1] == D_PROJ
    return out


def _dot(a, b):
    return jnp.dot(a, b, preferred_element_type=F32)


def _nt(a, b):
    return lax.dot_general(a, b, (((1,), (1,)), ((), ())), preferred_element_type=F32)


def _split3(a):
    hi = a.astype(BF16)
    r1 = a - hi.astype(F32)
    mid = r1.astype(BF16)
    lo = (r1 - mid.astype(F32)).astype(BF16)
    return hi, mid, lo


def _dot_lhs_f32(a, b_exact, terms=3):
    hi, mid, lo = _split3(a)
    out = _dot(hi, b_exact) + _dot(mid, b_exact)
    return out + _dot(lo, b_exact) if terms == 3 else out


def _dot_rhs_f32(a_exact, b, terms=3):
    hi, mid, lo = _split3(b)
    out = _dot(a_exact, hi) + _dot(a_exact, mid)
    return out + _dot(a_exact, lo) if terms == 3 else out


def _seqs_per_step(bsz):
    return math.gcd(bsz, 4)


def _sigmoid(x):
    return 1.0 / (1.0 + jnp.exp(-x))


def _silu(x):
    return x * _sigmoid(x)


def _softplus(x):
    return jnp.maximum(x, 0.0) + jnp.log(1.0 + jnp.exp(-jnp.abs(x)))


def _iota(shape, dim):
    return lax.broadcasted_iota(jnp.int32, shape, dim)


def _ada_kernel(c_ref, w_ref, b_ref, o_ref):
    ca = _silu(c_ref[...])
    w = w_ref[0]
    c_hi, c_mid, _ = _split3(ca)
    w_hi, w_mid, _ = _split3(w)
    acc = _dot(c_hi, w_hi) + _dot(c_hi, w_mid) + _dot(c_mid, w_hi)
    o_ref[0] = acc + b_ref[0]


def _ada_modulation(c, ada_w, ada_b):
    depth, d, d3 = ada_w.shape
    bsz = c.shape[0]
    nb = d3 // d
    return pl.pallas_call(
        _ada_kernel,
        grid=(depth, nb),
        in_specs=[pl.BlockSpec((bsz, d), lambda l, j: (0, 0)),
                  pl.BlockSpec((1, d, d), lambda l, j: (l, 0, j)),
                  pl.BlockSpec((1, 1, d), lambda l, j: (l, 0, j))],
        out_specs=pl.BlockSpec((1, bsz, d), lambda l, j: (l, 0, j)),
        out_shape=jax.ShapeDtypeStruct((depth, bsz, d3), F32),
        compiler_params=_cparams(("arbitrary", "arbitrary")),
        name="ada_modulation",
    )(c, ada_w, ada_b.reshape(depth, 1, d3))


def _inproj_kernel(x_ref, g_ref, sc_ref, sh_ref, w_ref, o_ref, small_ref, kcvc_ref, *, tn):
    x = x_ref[0]
    ms = jnp.mean(x * x, axis=-1, keepdims=True)
    y = x * lax.rsqrt(ms + EPS) * g_ref[...]
    h = (y * (1.0 + sc_ref[0]) + sh_ref[0]).astype(BF16)
    for j in range(D_PROJ // tn):
        acc = _dot(h, w_ref[:, j * tn:(j + 1) * tn])
        o_ref[0, :, j * tn:(j + 1) * tn] = acc.astype(o_ref.dtype)
        if j == (PB_SMALL * LANES) // tn:
            off = PB_SMALL * LANES - j * tn
            small_ref[0] = acc[:, off:off + LANES]
        if j == (PB_A_KCVC * LANES) // tn:
            off = PB_A_KCVC * LANES - j * tn
            kcvc_ref[0] = acc[:, off:off + LANES]


def _in_projection(x, norm_g, scale, shift, w_bf):
    bsz, seq, d = x.shape
    tm = min(1024, seq)
    tn = 1024
    return pl.pallas_call(
        functools.partial(_inproj_kernel, tn=tn),
        grid=(bsz, seq // tm),
        in_specs=[pl.BlockSpec((1, tm, d), lambda b, i: (b, i, 0)),
                  pl.BlockSpec((1, d), lambda b, i: (0, 0)),
                  pl.BlockSpec((1, 1, d), lambda b, i: (b, 0, 0)),
                  pl.BlockSpec((1, 1, d), lambda b, i: (b, 0, 0)),
                  pl.BlockSpec((d, D_PROJ), lambda b, i: (0, 0))],
        out_specs=[pl.BlockSpec((1, tm, D_PROJ), lambda b, i: (b, i, 0)),
                   pl.BlockSpec((1, tm, LANES), lambda b, i: (b, i, 0)),
                   pl.BlockSpec((1, tm, LANES), lambda b, i: (b, i, 0))],
        out_shape=[jax.ShapeDtypeStruct((bsz, seq, D_PROJ), BF16),
                   jax.ShapeDtypeStruct((bsz, seq, LANES), F32),
                   jax.ShapeDtypeStruct((bsz, seq, LANES), F32)],
        compiler_params=_cparams(("arbitrary", "arbitrary")),
        name="in_projection",
    )(x, norm_g.reshape(1, d), scale, shift, w_bf)


def _compress_kernel(x_ref, pos_ref, wall_ref, kw1_ref, kw2_ref, vw1_ref, vw2_ref, kv_ref, vt_ref):
    n = x_ref.shape[1] // NSA_CMP_STRIDE
    acc = jnp.zeros((n, 4 * NSA_CMP_HIDDEN), F32)
    for r in range(NSA_CMP_STRIDE):
        x_r = x_ref[0, pl.ds(r, n, stride=NSA_CMP_STRIDE), :].astype(BF16)
        acc = acc + _dot(x_r, wall_ref[r])
    pos = jnp.broadcast_to(pos_ref[...], (8, NSA_CMP_BLOCK * HEAD_DIM)).astype(BF16)

    def one(first, second, w1_ref, w2_ref):
        second = pltpu.roll(second, n - 1, 0)
        bias = _dot(pos, w1_ref[...].astype(BF16))[0:1]
        hid = _silu(first + second + bias)
        return _dot(hid.astype(BF16), w2_ref[...].astype(BF16))

    hw = NSA_CMP_HIDDEN
    kc = one(acc[:, 0:hw], acc[:, hw:2 * hw], kw1_ref, kw2_ref)
    vc = one(acc[:, 2 * hw:3 * hw], acc[:, 3 * hw:4 * hw], vw1_ref, vw2_ref)
    kv = jnp.concatenate([kc, vc], axis=1)
    kv_ref[0] = kv.astype(BF16)
    ones = jnp.ones((SUM_ROWS, kv.shape[0]), BF16)
    vt_ref[0] = jnp.concatenate([kv.T[HEAD_DIM:].astype(BF16), ones], axis=0)


def _compress_slot_weights(kw1, vw1):
    k1 = kw1.reshape(2, NSA_CMP_STRIDE, HEAD_DIM, NSA_CMP_HIDDEN)
    v1 = vw1.reshape(2, NSA_CMP_STRIDE, HEAD_DIM, NSA_CMP_HIDDEN)
    zero = jnp.zeros_like(k1[0])
    top = jnp.concatenate([k1[0], k1[1], zero, zero], axis=-1)
    bottom = jnp.concatenate([zero, zero, v1[0], v1[1]], axis=-1)
    return jnp.concatenate([top, bottom], axis=1).astype(BF16)


def _nsa_compress(kcvc, pos, kw1, kw2, vw1, vw2):
    bsz, seq, _ = kcvc.shape
    n = seq // NSA_CMP_STRIDE
    wall = _compress_slot_weights(kw1, vw1)
    full2 = lambda b: (0, 0)
    return pl.pallas_call(
        _compress_kernel,
        grid=(bsz,),
        in_specs=[pl.BlockSpec((1, seq, LANES), lambda b: (b, 0, 0)),
                  pl.BlockSpec(pos.shape, full2),
                  pl.BlockSpec(wall.shape, lambda b: (0, 0, 0)),
                  pl.BlockSpec(kw1.shape, full2), pl.BlockSpec(kw2.shape, full2),
                  pl.BlockSpec(vw1.shape, full2), pl.BlockSpec(vw2.shape, full2)],
        out_specs=[pl.BlockSpec((1, n, LANES), lambda b: (b, 0, 0)),
                   pl.BlockSpec((1, HEAD_DIM + SUM_ROWS, n), lambda b: (b, 0, 0))],
        out_shape=[jax.ShapeDtypeStruct((bsz, n, LANES), BF16),
                   jax.ShapeDtypeStruct((bsz, HEAD_DIM + SUM_ROWS, n), BF16)],
        compiler_params=_cparams(("arbitrary",)),
        name="nsa_compress",
    )(kcvc, pos, wall, kw1, kw2, vw1, vw2)


def _online_softmax(s, m):
    m_new = jnp.maximum(m, jnp.max(s, axis=0, keepdims=True))
    return m_new, jnp.exp2(s - m_new).astype(BF16), jnp.exp2(m - m_new)


def _nsa_kernel(q_ref, ks_ref, kw_ref, sm_ref, z_ref, ckv_ref, cvt_ref, c2s_ref, ng_ref, o_ref,
                vst, vwt, sel_scr, imp_scr, ms_scr, as_scr, *, seq, tq, tk, top_n):
    i = pl.program_id(1)
    n_cmp = seq // NSA_CMP_STRIDE
    n_sel = seq // NSA_SEL_BLOCK
    sub = min(tq, WINDOW_SUB)
    span = NSA_WINDOW + sub

    @pl.when(i == 0)
    def _():
        ones = jnp.ones((SUM_ROWS, LANES), BF16)
        for blk in range(seq // LANES):
            rows = slice(blk * LANES, (blk + 1) * LANES)
            ks = ks_ref[0, rows, :].astype(F32)
            kw = kw_ref[0, rows, :].astype(F32)
            vst[:, rows] = jnp.concatenate([ks.T[HEAD_DIM:].astype(BF16), ones], axis=0)
            vwt[:, rows] = jnp.concatenate([kw.T[HEAD_DIM:].astype(BF16), ones], axis=0)

    q0 = pl.multiple_of(i * tq, tq)
    q = q_ref[0].astype(F32) * (HEAD_DIM ** -0.5 * LOG2E)
    low = _iota((tq, LANES), 1) < HEAD_DIM
    qh = []
    for pair in range(N_HEADS // 2):
        qp = q[:, pair * LANES:(pair + 1) * LANES]
        qh.append(jnp.where(low, qp, 0.0).astype(BF16))
        qh.append(jnp.where(low, pltpu.roll(qp, HEAD_DIM, 1), 0.0).astype(BF16))

    t_row = q0 + _iota((1, tq), 1)

    ckv = ckv_ref[0]
    cvt = cvt_ref[0]
    cmp_end = _iota((n_cmp, 1), 0) * NSA_CMP_STRIDE + (NSA_CMP_BLOCK - 1)
    valid_c = cmp_end <= t_row
    any_valid = t_row >= NSA_CMP_BLOCK - 1
    heads = range(N_HEADS)
    s_c = [jnp.where(valid_c, _nt(ckv, qh[h]), NEG_INF) for h in heads]
    m_c = [jnp.max(s_c[h], axis=0, keepdims=True) for h in heads]
    p_c = [jnp.exp2(s_c[h] - m_c[h]) for h in heads]
    pv_c = [_dot(cvt, p_c[h].astype(BF16)) for h in heads]
    inv_c = [jnp.where(any_valid, 1.0 / pv_c[h][HEAD_DIM:HEAD_DIM + 1], 0.0) for h in heads]
    o_c = [pv_c[h][:HEAD_DIM] * inv_c[h] for h in heads]
    p_sum = p_c[0] * inv_c[0]
    for h in range(1, N_HEADS):
        p_sum = p_sum + p_c[h] * inv_c[h]

    p_hi = p_sum.astype(BF16)
    p_lo = (p_sum - p_hi.astype(F32)).astype(BF16)
    c2s = c2s_ref[...]
    imp = _dot(c2s, p_hi) + _dot(c2s, p_lo)
    sid = _iota((n_sel, tq), 0)
    cur = t_row // NSA_SEL_BLOCK
    forced = (sid == cur) | (sid == 0)
    imp = jnp.where(forced, NSA_FORCED_SCORE, jnp.where(sid <= cur, imp, -1.0))
    imp_scr[...] = imp
    per_step = tq // NSA_SEL_BLOCK

    def rank_body(jj, rank):
        for u in range(per_step):
            j = jj * per_step + u
            vj = imp_scr[pl.ds(j, 1), :]
            ge = jnp.where(vj >= imp, 1.0, 0.0)
            gt = jnp.where(vj > imp, 1.0, 0.0)
            rank = rank + jnp.where(sid > j, ge, gt)
        return rank

    n_rank = jnp.where((i + 1) * per_step > top_n, i + 1, 0)
    rank = lax.fori_loop(0, n_rank, rank_body, jnp.zeros((n_sel, tq), F32))
    sel_scr[...] = jnp.where(rank < float(top_n), 1.0, 0.0)

    blocks_per_chunk = tk // NSA_SEL_BLOCK

    def sel_step(kstart, nkeys, blk_off, col0, causal):
        kstart = pl.multiple_of(kstart, nkeys)
        k_c = ks_ref[0, pl.ds(kstart, nkeys), :]
        vt_c = vst[:, pl.ds(kstart, nkeys)]
        sb = pl.multiple_of(kstart // NSA_SEL_BLOCK - blk_off, blocks_per_chunk)
        selc = sel_scr[pl.ds(sb, blocks_per_chunk), col0:]
        rows = [jnp.broadcast_to(selc[r:r + 1, :], (NSA_SEL_BLOCK, tq - col0))
                for r in range(blk_off, blk_off + nkeys // NSA_SEL_BLOCK)]
        valid = jnp.concatenate(rows, axis=0) > 0.5
        if causal:
            valid = valid & (kstart + _iota((nkeys, 1), 0) <= t_row[:, col0:])
        s, p, alpha = [None] * N_HEADS, [None] * N_HEADS, [None] * N_HEADS

        def scores(h):
            return _nt(k_c, qh[h][col0:])

        def update(h):
            as_scr[h, :, col0:] = alpha[h] * as_scr[h, :, col0:] + _dot(vt_c, p[h])

        s[0] = scores(0)
        for h in range(N_HEADS):
            if h + 1 < N_HEADS:
                s[h + 1] = scores(h + 1)
            ms_scr[h, :, col0:], p[h], alpha[h] = _online_softmax(jnp.where(valid, s[h], NEG_INF),
                                                                  ms_scr[h, :, col0:])
            if h >= 1:
                update(h - 1)
        update(N_HEADS - 1)

    ms_scr[...] = jnp.full(ms_scr.shape, NEG_INF, F32)
    as_scr[...] = jnp.zeros(as_scr.shape, F32)

    def full_chunk(c, carry):
        sel_step(c * tk, tk, 0, 0, False)
        return carry

    half_blocks = blocks_per_chunk // 2
    lax.fori_loop(0, q0 // tk, full_chunk, 0)
    sel_step(q0, tk // 2, 0, 0, True)
    sel_step(q0 + tk // 2, tk // 2, half_blocks, tq // 2, True)

    streams = []
    for a in range(tq // sub):
        qs = slice(a * sub, (a + 1) * sub)
        wstart = pl.multiple_of(jnp.maximum(q0 + a * sub - NSA_WINDOW, 0), sub)
        kpos_w = wstart + _iota((span, 1), 0)
        t_sub = t_row[:, qs]
        valid_w = (kpos_w <= t_sub) & (kpos_w > t_sub - NSA_WINDOW)
        streams += [(kw_ref[0, pl.ds(wstart, span), :], vwt[:, pl.ds(wstart, span)], valid_w, qh[h][qs])
                    for h in range(N_HEADS)]
    s_w = [jnp.where(valid, _nt(k_w, q_w), NEG_INF) for k_w, _, valid, q_w in streams]
    m_w = [jnp.max(sw, axis=0, keepdims=True) for sw in s_w]
    p_w = [jnp.exp2(sw - mw).astype(BF16) for sw, mw in zip(s_w, m_w)]
    pv_w = [_dot(st[1], pw) for st, pw in zip(streams, p_w)]
    o_w_parts = [pv[:HEAD_DIM] * (1.0 / pv[HEAD_DIM:HEAD_DIM + 1]) for pv in pv_w]
    o_w = [jnp.concatenate(o_w_parts[h::N_HEADS], axis=1) for h in range(N_HEADS)]

    gates = _sigmoid(sm_ref[0].T[0:16])
    ng = ng_ref[...]
    outs = []
    for h in range(N_HEADS):
        g_c = gates[3 * h:3 * h + 1]
        g_s = gates[3 * h + 1:3 * h + 2]
        g_w = gates[3 * h + 2:3 * h + 3]
        acc_s = as_scr[h]
        o_s = acc_s[:HEAD_DIM] * (1.0 / acc_s[HEAD_DIM:HEAD_DIM + 1])
        oh = g_c * o_c[h] + g_s * o_s + g_w * o_w[h]
        ms = jnp.mean(oh * oh, axis=0, keepdims=True)
        outs.append(oh * lax.rsqrt(ms + EPS) * ng)
    y = jnp.concatenate(
        [jnp.concatenate(outs[0:2], axis=0).T, jnp.concatenate(outs[2:4], axis=0).T], axis=1)
    o_ref[0] = (y * _silu(z_ref[0].astype(F32))).astype(o_ref.dtype)


def _nsa_attention(proj, small, ckv, cvt, c2s_t, norm_g, *, tq=512):
    bsz, seq, _ = proj.shape
    tk = tq
    assert tk // NSA_SEL_BLOCK == 8
    n_cmp = seq // NSA_CMP_STRIDE
    n_sel = seq // NSA_SEL_BLOCK
    top_n = min(NSA_TOP_N, n_sel)
    ng = jnp.broadcast_to(norm_g.reshape(HEAD_DIM, 1), (HEAD_DIM, tq)).astype(F32)
    kern = functools.partial(_nsa_kernel, seq=seq, tq=tq, tk=tk, top_n=top_n)
    return pl.pallas_call(
        kern,
        grid=(bsz, seq // tq),
        in_specs=[pl.BlockSpec((1, tq, 2 * LANES), lambda b, i: (b, i, PB_A_Q // 2)),
                  pl.BlockSpec((1, seq, LANES), lambda b, i: (b, 0, PB_A_KSVS)),
                  pl.BlockSpec((1, seq, LANES), lambda b, i: (b, 0, PB_A_KWVW)),
                  pl.BlockSpec((1, tq, LANES), lambda b, i: (b, i, 0)),
                  pl.BlockSpec((1, tq, 2 * LANES), lambda b, i: (b, i, PB_A_Z // 2)),
                  pl.BlockSpec((1, n_cmp, LANES), lambda b, i: (b, 0, 0)),
                  pl.BlockSpec((1, HEAD_DIM + SUM_ROWS, n_cmp), lambda b, i: (b, 0, 0)),
                  pl.BlockSpec((n_sel, n_cmp), lambda b, i: (0, 0)),
                  pl.BlockSpec((HEAD_DIM, tq), lambda b, i: (0, 0))],
        out_specs=pl.BlockSpec((1, tq, D_GROUP), lambda b, i: (b, i, 0)),
        out_shape=jax.ShapeDtypeStruct((bsz, seq, D_GROUP), BF16),
        scratch_shapes=[pltpu.VMEM((HEAD_DIM + SUM_ROWS, seq), BF16),
                        pltpu.VMEM((HEAD_DIM + SUM_ROWS, seq), BF16),
                        pltpu.VMEM((n_sel, tq), F32), pltpu.VMEM((n_sel, tq), F32),
                        pltpu.VMEM((N_HEADS, 1, tq), F32),
                        pltpu.VMEM((N_HEADS, HEAD_DIM + SUM_ROWS, tq), F32)],
        compiler_params=_cparams(("arbitrary", "arbitrary")),
        name="nsa_attention",
    )(proj, proj, proj, small, proj, ckv, cvt, c2s_t, ng)


def _diff_kernel(q_ref, k_ref, v_ref, z_ref, lam_ref, ng_ref, o_ref, vt, m_scr, acc_scr, *, seq, tq, tk,
                 lambda_init):
    i = pl.program_id(1)
    vrows = HEAD_DIM + SUM_ROWS
    cols = 2 * tq
    hq = tq // 2

    @pl.when(i == 0)
    def _():
        ones = jnp.ones((SUM_ROWS, LANES), BF16)
        for blk in range(seq // LANES):
            rows = slice(blk * LANES, (blk + 1) * LANES)
            parts = []
            for pair in range(N_HEADS // 2):
                v_t = v_ref[0, rows, pair * LANES:(pair + 1) * LANES].astype(F32).T.astype(BF16)
                parts += [v_t[:HEAD_DIM], ones, v_t[HEAD_DIM:], ones]
            vt[:, rows] = jnp.concatenate(parts, axis=0)

    lam = lam_ref[...]
    lam_full = (jnp.exp(jnp.sum(lam[0:1] * lam[1:2], axis=1, keepdims=True))
                - jnp.exp(jnp.sum(lam[2:3] * lam[3:4], axis=1, keepdims=True)) + lambda_init)

    q0 = pl.multiple_of(i * tq, tq)
    q = q_ref[0].astype(F32) * (DIFF_QK_DIM ** -0.5 * LOG2E)
    part = _iota((tq, LANES), 1) // DIFF_QK_DIM
    q2 = []
    for h in range(N_HEADS):
        qp = q[:, (h // 2) * LANES:(h // 2 + 1) * LANES]
        m0, m1 = [jnp.where(part == 2 * (h % 2) + c, qp, 0.0).astype(BF16) for c in range(2)]
        q2.append(jnp.concatenate([m0[:hq], m1[:hq], m0[hq:], m1[hq:]], axis=0))
    t_half = q0 + _iota((1, hq), 1)
    t_cols = jnp.concatenate([t_half, t_half, t_half + hq, t_half + hq], axis=1)

    def softmax(s, m):
        m_new = jnp.maximum(m, jnp.max(s, axis=0, keepdims=True))
        return m_new, jnp.exp2(s - m_new).astype(BF16), jnp.exp2(m - m_new)

    def step(kstart, nkeys, col0, causal):
        kstart = pl.multiple_of(kstart, nkeys)
        kpos = kstart + _iota((nkeys, 1), 0)
        t_c = t_cols[:, col0:]
        s, p, alpha = [None] * N_HEADS, [None] * N_HEADS, [None] * N_HEADS

        def scores(h):
            lanes = slice((h // 2) * LANES, (h // 2 + 1) * LANES)
            return _nt(k_ref[0, pl.ds(kstart, nkeys), lanes], q2[h][col0:])

        def update(h):
            pv = _dot(vt[h * vrows:(h + 1) * vrows, pl.ds(kstart, nkeys)], p[h])
            acc_scr[h, :, col0:] = alpha[h] * acc_scr[h, :, col0:] + pv

        s[0] = scores(0)
        for h in range(N_HEADS):
            if h + 1 < N_HEADS:
                s[h + 1] = scores(h + 1)
            sh = jnp.where(kpos <= t_c, s[h], NEG_INF) if causal else s[h]
            m_scr[h, :, col0:], p[h], alpha[h] = softmax(sh, m_scr[h, :, col0:])
            if h >= 1:
                update(h - 1)
        update(N_HEADS - 1)

    m_scr[...] = jnp.full(m_scr.shape, NEG_INF, F32)
    acc_scr[...] = jnp.zeros(acc_scr.shape, F32)

    def full_chunk(c, carry):
        step(c * tk, tk, 0, False)
        return carry

    lax.fori_loop(0, q0 // tk, full_chunk, 0)
    step(q0, tk // 2, 0, True)
    step(q0 + tk // 2, tk // 2, cols // 2, True)

    ng = ng_ref[...]
    outs = []
    for h in range(N_HEADS):
        acc = acc_scr[h]
        o_all = acc[:HEAD_DIM] * (1.0 / acc[HEAD_DIM:HEAD_DIM + 1])
        map0 = jnp.concatenate([o_all[:, 0:hq], o_all[:, 2 * hq:3 * hq]], axis=1)
        map1 = jnp.concatenate([o_all[:, hq:2 * hq], o_all[:, 3 * hq:4 * hq]], axis=1)
        o = map0 - lam_full * map1
        ms = jnp.mean(o * o, axis=0, keepdims=True)
        outs.append(o * lax.rsqrt(ms + EPS) * ng * (1.0 - lambda_init))
    y = jnp.concatenate(
        [jnp.concatenate(outs[0:2], axis=0).T, jnp.concatenate(outs[2:4], axis=0).T], axis=1)
    o_ref[0] = (y * _silu(z_ref[0].astype(F32))).astype(o_ref.dtype)


def _diff_attention(proj, lam, norm_g, layer_idx, *, tq=512):
    bsz, seq, _ = proj.shape
    tk = tq
    lambda_init = 0.8 - 0.6 * math.exp(-0.3 * layer_idx)
    ng = jnp.broadcast_to(norm_g.reshape(HEAD_DIM, 1), (HEAD_DIM, tq)).astype(F32)
    kern = functools.partial(_diff_kernel, seq=seq, tq=tq, tk=tk, lambda_init=lambda_init)
    return pl.pallas_call(
        kern,
        grid=(bsz, seq // tq),
        in_specs=[pl.BlockSpec((1, tq, D_GROUP), lambda b, i: (b, i, PB_B_Q // 2)),
                  pl.BlockSpec((1, seq, D_GROUP), lambda b, i: (b, 0, PB_B_K // 2)),
                  pl.BlockSpec((1, seq, D_GROUP), lambda b, i: (b, 0, PB_B_V // 2)),
                  pl.BlockSpec((1, tq, D_GROUP), lambda b, i: (b, i, PB_B_Z // 2)),
                  pl.BlockSpec((4, DIFF_QK_DIM), lambda b, i: (0, 0)),
                  pl.BlockSpec((HEAD_DIM, tq), lambda b, i: (0, 0))],
        out_specs=pl.BlockSpec((1, tq, D_GROUP), lambda b, i: (b, i, 0)),
        out_shape=jax.ShapeDtypeStruct((bsz, seq, D_GROUP), BF16),
        scratch_shapes=[pltpu.VMEM((N_HEADS * (HEAD_DIM + SUM_ROWS), seq), BF16),
                        pltpu.VMEM((N_HEADS, 1, 2 * tq), F32),
                        pltpu.VMEM((N_HEADS, HEAD_DIM + SUM_ROWS, 2 * tq), F32)],
        compiler_params=_cparams(("arbitrary", "arbitrary")),
        name="diff_attention",
    )(proj, proj, proj, proj, lam, ng)


def _causal_conv_silu(x_raw, xext, cw_ref, cb_ref):
    n = x_raw.shape[0]
    xext[CONV_PAD:CONV_PAD + n, :] = x_raw
    cw = cw_ref[...]
    acc = cb_ref[...] + cw[0:1] * xext[pl.ds(CONV_PAD - 3, n), :]
    for k in range(1, CONV_WIDTH):
        acc = acc + cw[k:k + 1] * xext[pl.ds(CONV_PAD - 3 + k, n), :]
    xext[0:CONV_PAD, :] = x_raw[n - CONV_PAD:, :]
    return _silu(acc)


def _interleave(chunks):
    live = list(chunks)
    while live:
        nxt = []
        for g in live:
            try:
                next(g)
                nxt.append(g)
            except StopIteration:
                pass
        live = nxt


def _tri(n):
    return jnp.where(_iota((n, n), 0) >= _iota((n, n), 1), 1.0, 0.0).astype(BF16)


def _tri_upper(n):
    return jnp.where(_iota((n, n), 0) <= _iota((n, n), 1), 1.0, 0.0).astype(BF16)


def _expand_mat(first_lane, width):
    r = _iota((LANES, width), 0)
    c = _iota((LANES, width), 1)
    return jnp.where(r - first_lane == c // HEAD_DIM, 1.0, 0.0).astype(BF16)


def _group_sum_mat(width, group):
    r = _iota((width, width), 0)
    c = _iota((width, width), 1)
    return jnp.where(r // group == c // group, 1.0, 0.0).astype(BF16)


def _ssd_kernel(z_ref, xbc_ref, sm_ref, cw_ref, cb_ref, dtb_ref, alog_ref, alogc_ref, dsk_ref, ng_ref, o_ref,
                xext, st, *, nb):
    @pl.when(pl.program_id(1) == 0)
    def _():
        st[...] = jnp.zeros(st.shape, F32)
        xext[:, 0:CONV_PAD, :] = jnp.zeros((nb, CONV_PAD, SSM_XBC), F32)

    _interleave(_ssd_chunk(z_ref.at[bb], xbc_ref.at[bb], sm_ref.at[bb], cw_ref, cb_ref, dtb_ref, alog_ref,
                           alogc_ref, dsk_ref, ng_ref, o_ref.at[bb], xext.at[bb], st.at[bb])
                for bb in range(nb))


def _ssd_chunk(z_ref, xbc_ref, sm_ref, cw_ref, cb_ref, dtb_ref, alog_ref, alogc_ref, dsk_ref, ng_ref, o_ref,
               xext, st):
    n = CHUNK
    xc = _causal_conv_silu(xbc_ref[...].astype(F32), xext, cw_ref, cb_ref)
    xs = xc[:, 0:D_GROUP]
    bm = xc[:, D_GROUP:D_GROUP + LANES]
    cm = xc[:, D_GROUP + LANES:D_GROUP + 2 * LANES]
    yield

    dt_c = _softplus(sm_ref[...] + dtb_ref[...])
    dtx = _dot_lhs_f32(dt_c, _expand_mat(SM_DT, D_GROUP), terms=2)
    a_x = -jnp.exp(alog_ref[...])
    tri = _tri(n)
    acs = _dot_rhs_f32(tri, dtx * a_x, terms=2)
    da_t = (dt_c * -jnp.exp(alogc_ref[...])).T[SM_DT - N_HEADS:SM_DT + N_HEADS]
    acs_rows = _dot_lhs_f32(da_t, _tri_upper(n), terms=2)[N_HEADS:]
    acs_last = acs[n - 1:n, :]
    yield

    xdt = xs * dtx
    xdt_b = xdt.astype(BF16)
    bm_b = bm.astype(BF16)
    cm_b = cm.astype(BF16)
    lane = _iota((n, LANES), 1)
    causal = _iota((n, n), 0) >= _iota((n, n), 1)

    y_pairs = []
    for g in range(SSM_GROUPS):
        in_group = (lane // SSM_STATE) == g
        gmat = _nt(jnp.where(in_group, cm, 0.0).astype(BF16), bm_b)
        pair = []
        for h in (2 * g, 2 * g + 1):
            col = acs[:, HEAD_DIM * h:HEAD_DIM * h + 1]
            row = acs_rows[h:h + 1, :]
            decay = jnp.exp(jnp.where(causal, col - row, NEG_INF))
            pair.append(_dot((gmat * decay).astype(BF16), xdt_b[:, g * LANES:(g + 1) * LANES]))
        y_pairs.append(jnp.where(lane < HEAD_DIM, pair[0], pair[1]))
        yield
    y_diag = jnp.concatenate(y_pairs, axis=1)

    state = st[...]
    y_off = jnp.exp(acs) * _dot(cm_b, state.astype(BF16))
    decay_end = jnp.exp(acs_last - acs)
    upd = _dot(bm.T.astype(BF16), (decay_end * xdt).astype(BF16))
    own = (_iota(st.shape, 0) // SSM_STATE) == (_iota(st.shape, 1) // LANES)
    st[...] = jnp.where(own, jnp.exp(acs_last) * state + upd, 0.0)
    yield

    y = (y_diag + y_off + dsk_ref[...] * xs) * _silu(z_ref[...].astype(F32))
    ms = _dot_lhs_f32(y * y, _group_sum_mat(D_GROUP, LANES), terms=2) * (1.0 / LANES)
    o_ref[...] = (y * lax.rsqrt(ms + EPS) * ng_ref[...]).astype(o_ref.dtype)


def _ssd_mixer(proj, small, conv_w, conv_b, dt_bias, a_log, d_skip, norm_g):
    bsz, seq, _ = proj.shape
    n = CHUNK
    dtb = jnp.zeros((1, LANES), F32).at[0, SM_DT:SM_DT + N_HEADS].set(dt_bias)
    alog_x = jnp.repeat(a_log, HEAD_DIM).reshape(1, D_GROUP)
    alog_c = jnp.zeros((1, LANES), F32).at[0, SM_DT:SM_DT + N_HEADS].set(a_log)
    dsk_x = jnp.repeat(d_skip, HEAD_DIM).reshape(1, D_GROUP)
    full = lambda b, c: (0, 0)
    nb = _seqs_per_step(bsz)
    return pl.pallas_call(
        functools.partial(_ssd_kernel, nb=nb),
        grid=(bsz // nb, seq // n),
        in_specs=[pl.BlockSpec((nb, n, D_GROUP), lambda b, c: (b, c, PB_C_Z // 2)),
                  pl.BlockSpec((nb, n, SSM_XBC), lambda b, c: (b, c, PB_C_XBC * LANES // SSM_XBC)),
                  pl.BlockSpec((nb, n, LANES), lambda b, c: (b, c, 0)),
                  pl.BlockSpec((CONV_WIDTH, SSM_XBC), full),
                  pl.BlockSpec((1, SSM_XBC), full),
                  pl.BlockSpec((1, LANES), full),
                  pl.BlockSpec((1, D_GROUP), full),
                  pl.BlockSpec((1, LANES), full),
                  pl.BlockSpec((1, D_GROUP), full),
                  pl.BlockSpec((1, D_GROUP), full)],
        out_specs=pl.BlockSpec((nb, n, D_GROUP), lambda b, c: (b, c, 0)),
        out_shape=jax.ShapeDtypeStruct((bsz, seq, D_GROUP), BF16),
        scratch_shapes=[pltpu.VMEM((nb, CONV_PAD + n, SSM_XBC), F32), pltpu.VMEM((nb, LANES, D_GROUP), F32)],
        compiler_params=_cparams(("arbitrary", "arbitrary")),
        name="ssd_mixer",
    )(proj, proj, small, conv_w, conv_b.reshape(1, SSM_XBC), dtb, alog_x, alog_c, dsk_x,
      norm_g.reshape(1, D_GROUP))


def _mlstm_kernel(qk_ref, v_ref, sm_ref, og_ref, z_ref, cw_ref, cb_ref, ifb_ref, ng_ref, o_ref,
                  xext, c_st, n_st, m_st, *, nb):
    @pl.when(pl.program_id(1) == 0)
    def _():
        c_st[...] = jnp.zeros(c_st.shape, F32)
        n_st[...] = jnp.zeros(n_st.shape, F32)
        m_st[...] = jnp.zeros(m_st.shape, F32)
        xext[:, 0:CONV_PAD, :] = jnp.zeros((nb, CONV_PAD, 2 * D_GROUP), F32)

    _interleave(_mlstm_chunk(qk_ref.at[bb], v_ref.at[bb], sm_ref.at[bb], og_ref.at[bb], z_ref.at[bb], cw_ref,
                             cb_ref, ifb_ref, ng_ref, o_ref.at[bb], xext.at[bb], c_st.at[bb], n_st.at[bb],
                             m_st.at[bb]) for bb in range(nb))


def _mlstm_chunk(qk_ref, v_ref, sm_ref, og_ref, z_ref, cw_ref, cb_ref, ifb_ref, ng_ref, o_ref,
                 xext, c_st, n_st, m_st):
    n = CHUNK
    qk = _causal_conv_silu(qk_ref[...].astype(F32), xext, cw_ref, cb_ref)
    q = qk[:, 0:D_GROUP]
    k = qk[:, D_GROUP:] * (HEAD_DIM ** -0.5)
    q_b = q.astype(BF16)
    k_b = k.astype(BF16)
    v_b = v_ref[...]
    yield

    pre = sm_ref[...] + ifb_ref[...]
    logf = -_softplus(-pre)
    ig = _dot_lhs_f32(pre, _expand_mat(SM_I, D_GROUP), terms=2)
    lf = _dot_lhs_f32(logf, _expand_mat(SM_F, D_GROUP), terms=2)
    tri = _tri(n)
    b = _dot_rhs_f32(tri, lf, terms=2)
    b_last = b[n - 1:n, :]
    gates_t = jnp.where(_iota((n, LANES), 1) >= SM_F, logf, pre).T[SM_I:SM_I + 2 * N_HEADS]
    cum_t = _dot_lhs_f32(gates_t, _tri_upper(n), terms=2)
    u_rows = gates_t[0:N_HEADS] - cum_t[N_HEADS:2 * N_HEADS]
    yield

    m_prev = m_st[...]
    c_prev = c_st[...]
    n_prev = n_st[...]

    lane = _iota((n, LANES), 1)
    causal = _iota((n, n), 0) >= _iota((n, n), 1)
    ones_b = jnp.ones((n, LANES), BF16)
    num_pairs, den_pairs, mt_pairs, wi_pairs = [], [], [], []
    for pr in range(N_HEADS // 2):
        lanes = slice(pr * LANES, (pr + 1) * LANES)
        qp = q[:, lanes]
        kp_b = k_b[:, lanes]
        rhs = jnp.concatenate([v_b[:, lanes], ones_b], axis=1)
        res, mts, wis = [], [], []
        for hh in range(2):
            h = 2 * pr + hh
            in_head = (lane // HEAD_DIM) == hh
            bcol = b[:, HEAD_DIM * h:HEAD_DIM * h + 1]
            dlog = jnp.where(causal, bcol + u_rows[h:h + 1, :], NEG_INF)
            inter = bcol + m_prev[:, HEAD_DIM * h:HEAD_DIM * h + 1]
            m_t = jnp.maximum(inter, jnp.max(dlog, axis=1, keepdims=True))
            s_qk = _nt(jnp.where(in_head, qp, 0.0).astype(BF16), kp_b) * jnp.exp(dlog - m_t)
            res.append(_dot(s_qk.astype(BF16), rhs))
            mts.append(jnp.broadcast_to(m_t, (n, LANES)))
            wis.append(jnp.broadcast_to(jnp.exp(inter - m_t), (n, LANES)))
            yield
        first = lane < HEAD_DIM
        num_pairs.append(jnp.where(first, res[0][:, :LANES], res[1][:, :LANES]))
        den_pairs.append(jnp.where(first, res[0][:, LANES:], res[1][:, LANES:]))
        mt_pairs.append(jnp.where(first, mts[0], mts[1]))
        wi_pairs.append(jnp.where(first, wis[0], wis[1]))
    num_intra = jnp.concatenate(num_pairs, axis=1)
    den_intra = jnp.concatenate(den_pairs, axis=1)
    m_t = jnp.concatenate(mt_pairs, axis=1)
    w_inter = jnp.concatenate(wi_pairs, axis=1)

    head_sum = _group_sum_mat(D_GROUP, HEAD_DIM)
    num = num_intra + w_inter * _dot(q_b, c_prev.astype(BF16))
    den = den_intra + w_inter * _dot_lhs_f32(q * n_prev, head_sum, terms=2)
    hid = num / jnp.maximum(jnp.abs(den), jnp.exp(-m_t))
    yield

    g_end = b_last - b + ig
    m_loc = jnp.max(g_end, axis=0, keepdims=True)
    m_new = jnp.maximum(b_last + m_prev, m_loc)
    a_prev = jnp.exp(b_last + m_prev - m_new)
    a_loc = jnp.exp(m_loc - m_new)
    kw = k * (jnp.exp(g_end - m_loc) * a_loc)
    kw_t = jnp.concatenate([kw[:, :LANES].T, kw[:, LANES:].T], axis=0)
    upd = _dot(kw_t.astype(BF16), v_b)
    own = (_iota(c_st.shape, 0) // HEAD_DIM) == (_iota(c_st.shape, 1) // HEAD_DIM)
    c_st[...] = jnp.where(own, a_prev * c_prev + upd, 0.0)
    n_st[...] = a_prev * n_prev + jnp.sum(kw, axis=0, keepdims=True)
    m_st[...] = m_new
    yield

    hm = _sigmoid(og_ref[...].astype(F32)) * hid
    ms = _dot_lhs_f32(hm * hm, head_sum, terms=2) * (1.0 / HEAD_DIM)
    o_ref[...] = (hm * lax.rsqrt(ms + EPS) * ng_ref[...] * _silu(z_ref[...].astype(F32))).astype(o_ref.dtype)


def _mlstm_mixer(proj, small, conv_w, conv_b, if_b, norm_g):
    bsz, seq, _ = proj.shape
    n = CHUNK
    ifb = jnp.zeros((1, LANES), F32).at[0, SM_I:SM_I + 2 * N_HEADS].set(if_b)
    ng = jnp.tile(norm_g, N_HEADS).reshape(1, D_GROUP)
    full = lambda b, c: (0, 0)
    nb = _seqs_per_step(bsz)
    return pl.pallas_call(
        functools.partial(_mlstm_kernel, nb=nb),
        grid=(bsz // nb, seq // n),
        in_specs=[pl.BlockSpec((nb, n, 2 * D_GROUP), lambda b, c: (b, c, PB_D_QK * LANES // (2 * D_GROUP))),
                  pl.BlockSpec((nb, n, D_GROUP), lambda b, c: (b, c, PB_D_V // 2)),
                  pl.BlockSpec((nb, n, LANES), lambda b, c: (b, c, 0)),
                  pl.BlockSpec((nb, n, D_GROUP), lambda b, c: (b, c, PB_D_O // 2)),
                  pl.BlockSpec((nb, n, D_GROUP), lambda b, c: (b, c, PB_D_Z // 2)),
                  pl.BlockSpec((CONV_WIDTH, 2 * D_GROUP), full),
                  pl.BlockSpec((1, 2 * D_GROUP), full),
                  pl.BlockSpec((1, LANES), full),
                  pl.BlockSpec((1, D_GROUP), full)],
        out_specs=pl.BlockSpec((nb, n, D_GROUP), lambda b, c: (b, c, 0)),
        out_shape=jax.ShapeDtypeStruct((bsz, seq, D_GROUP), BF16),
        scratch_shapes=[pltpu.VMEM((nb, CONV_PAD + n, 2 * D_GROUP), F32),
                        pltpu.VMEM((nb, D_GROUP, D_GROUP), F32),
                        pltpu.VMEM((nb, 1, D_GROUP), F32),
                        pltpu.VMEM((nb, 1, D_GROUP), F32)],
        compiler_params=_cparams(("arbitrary", "arbitrary")),
        name="mlstm_mixer",
    )(proj, proj, small, proj, proj, conv_w, conv_b.reshape(1, 2 * D_GROUP), ifb, ng)


def _outproj_kernel(ya_ref, yb_ref, yc_ref, yd_ref, x_ref, gate_ref, w_ref, fg_ref, o_ref, *, final):
    acc = _dot(ya_ref[0], w_ref[0:D_GROUP, :])
    acc = acc + _dot(yb_ref[0], w_ref[D_GROUP:2 * D_GROUP, :])
    acc = acc + _dot(yc_ref[0], w_ref[2 * D_GROUP:3 * D_GROUP, :])
    acc = acc + _dot(yd_ref[0], w_ref[3 * D_GROUP:4 * D_GROUP, :])
    out = x_ref[0] + gate_ref[0] * acc
    if final:
        ms = jnp.mean(out * out, axis=-1, keepdims=True)
        out = out * lax.rsqrt(ms + EPS) * fg_ref[...]
    o_ref[0] = out


def _out_projection(ys, x, gate, w_bf, final_g, final):
    bsz, seq, d = x.shape
    tm = min(1024, seq)
    yspec = pl.BlockSpec((1, tm, D_GROUP), lambda b, i: (b, i, 0))
    return pl.pallas_call(
        functools.partial(_outproj_kernel, final=final),
        grid=(bsz, seq // tm),
        in_specs=[yspec, yspec, yspec, yspec,
                  pl.BlockSpec((1, tm, d), lambda b, i: (b, i, 0)),
                  pl.BlockSpec((1, 1, d), lambda b, i: (b, 0, 0)),
                  pl.BlockSpec((N_HEADS * D_GROUP, d), lambda b, i: (0, 0)),
                  pl.BlockSpec((1, d), lambda b, i: (0, 0))],
        out_specs=pl.BlockSpec((1, tm, d), lambda b, i: (b, i, 0)),
        out_shape=jax.ShapeDtypeStruct((bsz, seq, d), F32),
        compiler_params=_cparams(("arbitrary", "arbitrary")),
        name="out_projection",
    )(*ys, x, gate, w_bf, final_g.reshape(1, d))


def _cmp_to_sel_t(seq):
    n_cmp = (seq - NSA_CMP_BLOCK) // NSA_CMP_STRIDE + 1
    n_sel = seq // NSA_SEL_BLOCK
    start = np.arange(n_cmp)[:, None] * NSA_CMP_STRIDE
    sel_start = np.arange(n_sel)[None, :] * NSA_SEL_BLOCK
    overlap = np.clip(np.minimum(start + NSA_CMP_BLOCK, sel_start + NSA_SEL_BLOCK)
                      - np.maximum(start, sel_start), 0, None)
    out = np.zeros((n_sel, seq // NSA_CMP_STRIDE), np.float32)
    out[:, :n_cmp] = (overlap / NSA_CMP_BLOCK).T
    return jnp.asarray(out, BF16)


def _mixers(proj, small, kcvc, layer_idx, p):
    seq = proj.shape[1]
    pos = p['nsa_cmp_pos'].reshape(1, NSA_CMP_BLOCK * HEAD_DIM)
    ckv, cvt = _nsa_compress(kcvc, pos, p['nsa_ck_w1'], p['nsa_ck_w2'], p['nsa_cv_w1'], p['nsa_cv_w2'])
    y_a = _nsa_attention(proj, small, ckv, cvt, _cmp_to_sel_t(seq), p['nsa_norm_g'])
    y_b = _diff_attention(proj, p['diff_lam'], p['diff_norm_g'], layer_idx)
    y_c = _ssd_mixer(proj, small, p['ssm_conv_w'], p['ssm_conv_b'], p['ssm_dt_bias'], p['ssm_a_log'],
                     p['ssm_d'], p['ssm_norm_g'])
    y_d = _mlstm_mixer(proj, small, p['ml_conv_w'], p['ml_conv_b'], p['ml_if_b'], p['ml_norm_g'])
    return y_a, y_b, y_c, y_d


_LAYER_PARAMS = ('nsa_cmp_pos', 'nsa_ck_w1', 'nsa_ck_w2', 'nsa_cv_w1', 'nsa_cv_w2', 'nsa_norm_g',
                 'diff_lam', 'diff_norm_g', 'ssm_conv_w', 'ssm_conv_b', 'ssm_dt_bias', 'ssm_a_log',
                 'ssm_d', 'ssm_norm_g', 'ml_conv_w', 'ml_conv_b', 'ml_if_b', 'ml_norm_g')


def kernel(x, c, norm_g, ada_w, ada_b, w_in, w_out, nsa_cmp_pos, nsa_ck_w1, nsa_ck_w2, nsa_cv_w1, nsa_cv_w2, nsa_norm_g, diff_lam, diff_norm_g, ssm_conv_w, ssm_conv_b, ssm_dt_bias, ssm_a_log, ssm_d, ssm_norm_g, ml_conv_w, ml_conv_b, ml_if_b, ml_norm_g, final_g):
    stacked = dict(nsa_cmp_pos=nsa_cmp_pos, nsa_ck_w1=nsa_ck_w1, nsa_ck_w2=nsa_ck_w2, nsa_cv_w1=nsa_cv_w1,
                   nsa_cv_w2=nsa_cv_w2, nsa_norm_g=nsa_norm_g, diff_lam=diff_lam, diff_norm_g=diff_norm_g,
                   ssm_conv_w=ssm_conv_w, ssm_conv_b=ssm_conv_b, ssm_dt_bias=ssm_dt_bias,
                   ssm_a_log=ssm_a_log, ssm_d=ssm_d, ssm_norm_g=ssm_norm_g, ml_conv_w=ml_conv_w,
                   ml_conv_b=ml_conv_b, ml_if_b=ml_if_b, ml_norm_g=ml_norm_g)
    depth = w_in.shape[0]
    bsz, seq, d = x.shape
    mod = _ada_modulation(c, ada_w, ada_b)
    for l in range(depth):
        p = {name: stacked[name][l] for name in _LAYER_PARAMS}
        shift = mod[l, :, 0:d].reshape(bsz, 1, d)
        scale = mod[l, :, d:2 * d].reshape(bsz, 1, d)
        gate = mod[l, :, 2 * d:3 * d].reshape(bsz, 1, d)
        w_bf = _relayout_w_in(w_in[l]).astype(BF16)
        proj, small, kcvc = _in_projection(x, norm_g[l], scale, shift, w_bf)
        ys = _mixers(proj, small, kcvc, l, p)
        x = _out_projection(ys, x, gate, w_out[l].astype(BF16), final_g, final=(l == depth - 1))
    return x
```

```python
import functools
import math

import numpy as np
import jax
import jax.numpy as jnp
from jax import lax
from jax.experimental import pallas as pl
from jax.experimental.pallas import tpu as pltpu

F32 = jnp.float32
BF16 = jnp.bfloat16

D_MODEL = 1024
N_HEADS = 4
HEAD_DIM = 64
D_GROUP = N_HEADS * HEAD_DIM
NEG_INF = -1e30
EPS = 1e-6

NSA_CMP_BLOCK = 32
NSA_CMP_STRIDE = 16
NSA_SEL_BLOCK = 64
NSA_TOP_N = 16
NSA_WINDOW = 512
NSA_CMP_HIDDEN = 128
NSA_FORCED_SCORE = 1e4

DIFF_QK_DIM = HEAD_DIM // 2
SSM_STATE = 64
SSM_GROUPS = 2
SSM_XBC = D_GROUP + 2 * SSM_GROUPS * SSM_STATE
CONV_WIDTH = 4
CHUNK = 128
CONV_PAD = 8

LANES = 128
D_PROJ = 4096

_SPLITS = (
    D_GROUP, HEAD_DIM, HEAD_DIM, HEAD_DIM, HEAD_DIM, HEAD_DIM, HEAD_DIM, 3 * N_HEADS, D_GROUP,
    D_GROUP, D_GROUP, D_GROUP, D_GROUP,
    D_GROUP, SSM_XBC, N_HEADS,
    2 * D_GROUP, D_GROUP, 2 * N_HEADS, D_GROUP, D_GROUP,
)
_OFFS = [0] + [int(o) for o in np.cumsum(_SPLITS)]
(_A_Q, _A_KC, _A_VC, _A_KS, _A_VS, _A_KW, _A_VW, _A_G, _A_Z,
 _B_Q, _B_K, _B_V, _B_Z, _C_Z, _C_XBC, _C_DT, _D_QK, _D_V, _D_IF, _D_O, _D_Z) = range(21)

SM_GATE = 0
SM_DT = 12
SM_I = 16
SM_F = 20

PB_A_Q = 0
PB_A_KCVC = 2
PB_A_KSVS = 3
PB_A_KWVW = 4
PB_SMALL = 5
PB_A_Z = 6
PB_B_Q = 8
PB_B_K = 10
PB_B_V = 12
PB_B_Z = 14
PB_C_XBC = 16
PB_C_Z = 20
PB_D_V = 22
PB_D_QK = 24
PB_D_O = 28
PB_D_Z = 30

VMEM_LIMIT = 56 * 1024 * 1024
LOG2E = 1.4426950408889634
SUM_ROWS = 16
WINDOW_SUB = 256


def _cparams(sem):
    return pltpu.CompilerParams(dimension_semantics=sem, vmem_limit_bytes=VMEM_LIMIT)


def _relayout_w_in(w_in):
    def col(i):
        return w_in[:, _OFFS[i]:_OFFS[i + 1]]
    d = w_in.shape[0]
    small = jnp.concatenate(
        [col(_A_G), col(_C_DT), col(_D_IF), jnp.zeros((d, LANES - 24), w_in.dtype)], axis=1)
    parts = [col(_A_Q), col(_A_KC), col(_A_VC), col(_A_KS), col(_A_VS), col(_A_KW), col(_A_VW),
             small, col(_A_Z),
             col(_B_Q), col(_B_K), col(_B_V), col(_B_Z),
             col(_C_XBC), col(_C_Z),
             col(_D_V), col(_D_QK), col(_D_O), col(_D_Z)]
    out = jnp.concatenate(parts, axis=1)
    assert out.shape[1] == D_PROJ
    return out


def _dot(a, b):
    return jnp.dot(a, b, preferred_element_type=F32)


def _nt(a, b):
    return lax.dot_general(a, b, (((1,), (1,)), ((), ())), preferred_element_type=F32)


def _split3(a):
    hi = a.astype(BF16)
    r1 = a - hi.astype(F32)
    mid = r1.astype(BF16)
    lo = (r1 - mid.astype(F32)).astype(BF16)
    return hi, mid, lo


def _dot_lhs_f32(a, b_exact, terms=3):
    hi, mid, lo = _split3(a)
    out = _dot(hi, b_exact) + _dot(mid, b_exact)
    return out + _dot(lo, b_exact) if terms == 3 else out


def _seqs_per_step(bsz):
    return math.gcd(bsz, 4)


def _sigmoid(x):
    return 1.0 / (1.0 + jnp.exp(-x))


def _silu(x):
    return x * _sigmoid(x)


def _softplus(x):
    return jnp.maximum(x, 0.0) + jnp.log(1.0 + jnp.exp(-jnp.abs(x)))


def _iota(shape, dim):
    return lax.broadcasted_iota(jnp.int32, shape, dim)


def _ada_kernel(c_ref, w_ref, b_ref, o_ref):
    ca = _silu(c_ref[...])
    w = w_ref[0]
    c_hi, c_mid, _ = _split3(ca)
    w_hi, w_mid, _ = _split3(w)
    acc = _dot(c_hi, w_hi) + _dot(c_hi, w_mid) + _dot(c_mid, w_hi)
    o_ref[0] = acc + b_ref[0]


def _ada_modulation(c, ada_w, ada_b):
    depth, d, d3 = ada_w.shape
    bsz = c.shape[0]
    nb = d3 // d
    return pl.pallas_call(
        _ada_kernel,
        grid=(depth, nb),
        in_specs=[pl.BlockSpec((bsz, d), lambda l, j: (0, 0)),
                  pl.BlockSpec((1, d, d), lambda l, j: (l, 0, j)),
                  pl.BlockSpec((1, 1, d), lambda l, j: (l, 0, j))],
        out_specs=pl.BlockSpec((1, bsz, d), lambda l, j: (l, 0, j)),
        out_shape=jax.ShapeDtypeStruct((depth, bsz, d3), F32),
        compiler_params=_cparams(("arbitrary", "arbitrary")),
        name="ada_modulation",
    )(c, ada_w, ada_b.reshape(depth, 1, d3))


def _inproj_kernel(x_ref, g_ref, sc_ref, sh_ref, w_ref, o_ref, small_ref, kcvc_ref, *, tn):
    x = x_ref[0]
    ms = jnp.mean(x * x, axis=-1, keepdims=True)
    y = x * lax.rsqrt(ms + EPS) * g_ref[...]
    h = (y * (1.0 + sc_ref[0]) + sh_ref[0]).astype(BF16)
    for j in range(D_PROJ // tn):
        acc = _dot(h, w_ref[:, j * tn:(j + 1) * tn])
        o_ref[0, :, j * tn:(j + 1) * tn] = acc.astype(o_ref.dtype)
        if j == (PB_SMALL * LANES) // tn:
            off = PB_SMALL * LANES - j * tn
            small_ref[0] = acc[:, off:off + LANES]
        if j == (PB_A_KCVC * LANES) // tn:
            off = PB_A_KCVC * LANES - j * tn
            kcvc_ref[0] = acc[:, off:off + LANES]


def _in_projection(x, norm_g, scale, shift, w_bf):
    bsz, seq, d = x.shape
    tm = min(1024, seq)
    tn = 1024
    return pl.pallas_call(
        functools.partial(_inproj_kernel, tn=tn),
        grid=(bsz, seq // tm),
        in_specs=[pl.BlockSpec((1, tm, d), lambda b, i: (b, i, 0)),
                  pl.BlockSpec((1, d), lambda b, i: (0, 0)),
                  pl.BlockSpec((1, 1, d), lambda b, i: (b, 0, 0)),
                  pl.BlockSpec((1, 1, d), lambda b, i: (b, 0, 0)),
                  pl.BlockSpec((d, D_PROJ), lambda b, i: (0, 0))],
        out_specs=[pl.BlockSpec((1, tm, D_PROJ), lambda b, i: (b, i, 0)),
                   pl.BlockSpec((1, tm, LANES), lambda b, i: (b, i, 0)),
                   pl.BlockSpec((1, tm, LANES), lambda b, i: (b, i, 0))],
        out_shape=[jax.ShapeDtypeStruct((bsz, seq, D_PROJ), BF16),
                   jax.ShapeDtypeStruct((bsz, seq, LANES), F32),
                   jax.ShapeDtypeStruct((bsz, seq, LANES), F32)],
        compiler_params=_cparams(("arbitrary", "arbitrary")),
        name="in_projection",
    )(x, norm_g.reshape(1, d), scale, shift, w_bf)


def _compress_kernel(x_ref, pos_ref, wall_ref, kw1_ref, kw2_ref, vw1_ref, vw2_ref, kv_ref, vt_ref):
    n = x_ref.shape[1] // NSA_CMP_STRIDE
    acc = jnp.zeros((n, 4 * NSA_CMP_HIDDEN), F32)
    for r in range(NSA_CMP_STRIDE):
        x_r = x_ref[0, pl.ds(r, n, stride=NSA_CMP_STRIDE), :].astype(BF16)
        acc = acc + _dot(x_r, wall_ref[r])
    pos = jnp.broadcast_to(pos_ref[...], (8, NSA_CMP_BLOCK * HEAD_DIM)).astype(BF16)

    def one(first, second, w1_ref, w2_ref):
        second = pltpu.roll(second, n - 1, 0)
        bias = _dot(pos, w1_ref[...].astype(BF16))[0:1]
        hid = _silu(first + second + bias)
        return _dot(hid.astype(BF16), w2_ref[...].astype(BF16))

    hw = NSA_CMP_HIDDEN
    kc = one(acc[:, 0:hw], acc[:, hw:2 * hw], kw1_ref, kw2_ref)
    vc = one(acc[:, 2 * hw:3 * hw], acc[:, 3 * hw:4 * hw], vw1_ref, vw2_ref)
    kv = jnp.concatenate([kc, vc], axis=1)
    kv_ref[0] = kv.astype(BF16)
    ones = jnp.ones((SUM_ROWS, kv.shape[0]), BF16)
    vt_ref[0] = jnp.concatenate([kv.T[HEAD_DIM:].astype(BF16), ones], axis=0)


def _compress_slot_weights(kw1, vw1):
    k1 = kw1.reshape(2, NSA_CMP_STRIDE, HEAD_DIM, NSA_CMP_HIDDEN)
    v1 = vw1.reshape(2, NSA_CMP_STRIDE, HEAD_DIM, NSA_CMP_HIDDEN)
    zero = jnp.zeros_like(k1[0])
    top = jnp.concatenate([k1[0], k1[1], zero, zero], axis=-1)
    bottom = jnp.concatenate([zero, zero, v1[0], v1[1]], axis=-1)
    return jnp.concatenate([top, bottom], axis=1).astype(BF16)


def _nsa_compress(kcvc, pos, kw1, kw2, vw1, vw2):
    bsz, seq, _ = kcvc.shape
    n = seq // NSA_CMP_STRIDE
    wall = _compress_slot_weights(kw1, vw1)
    full2 = lambda b: (0, 0)
    return pl.pallas_call(
        _compress_kernel,
        grid=(bsz,),
        in_specs=[pl.BlockSpec((1, seq, LANES), lambda b: (b, 0, 0)),
                  pl.BlockSpec(pos.shape, full2),
                  pl.BlockSpec(wall.shape, lambda b: (0, 0, 0)),
                  pl.BlockSpec(kw1.shape, full2), pl.BlockSpec(kw2.shape, full2),
                  pl.BlockSpec(vw1.shape, full2), pl.BlockSpec(vw2.shape, full2)],
        out_specs=[pl.BlockSpec((1, n, LANES), lambda b: (b, 0, 0)),
                   pl.BlockSpec((1, HEAD_DIM + SUM_ROWS, n), lambda b: (b, 0, 0))],
        out_shape=[jax.ShapeDtypeStruct((bsz, n, LANES), BF16),
                   jax.ShapeDtypeStruct((bsz, HEAD_DIM + SUM_ROWS, n), BF16)],
        compiler_params=_cparams(("arbitrary",)),
        name="nsa_compress",
    )(kcvc, pos, wall, kw1, kw2, vw1, vw2)


def _online_softmax(s, m):
    m_new = jnp.maximum(m, jnp.max(s, axis=0, keepdims=True))
    return m_new, jnp.exp2(s - m_new).astype(BF16), jnp.exp2(m - m_new)


def _nsa_kernel(q_ref, ks_ref, kw_ref, sm_ref, z_ref, ckv_ref, cvt_ref, c2s_ref, ng_ref, o_ref,
                vst, vwt, sel_scr, imp_scr, ms_scr, as_scr, *, seq, tq, tk, top_n):
    i = pl.program_id(1)
    n_cmp = seq // NSA_CMP_STRIDE
    n_sel = seq // NSA_SEL_BLOCK
    sub = min(tq, WINDOW_SUB)
    span = NSA_WINDOW + sub

    @pl.when(i == 0)
    def _():
        ones = jnp.ones((SUM_ROWS, LANES), BF16)
        for blk in range(seq // LANES):
            rows = slice(blk * LANES, (blk + 1) * LANES)
            ks = ks_ref[0, rows, :].astype(F32)
            kw = kw_ref[0, rows, :].astype(F32)
            vst[:, rows] = jnp.concatenate([ks.T[HEAD_DIM:].astype(BF16), ones], axis=0)
            vwt[:, rows] = jnp.concatenate([kw.T[HEAD_DIM:].astype(BF16), ones], axis=0)

    q0 = pl.multiple_of(i * tq, tq)
    q = q_ref[0].astype(F32) * (HEAD_DIM ** -0.5 * LOG2E)
    low = _iota((tq, LANES), 1) < HEAD_DIM
    qh = []
    for pair in range(N_HEADS // 2):
        qp = q[:, pair * LANES:(pair + 1) * LANES]
        qh.append(jnp.where(low, qp, 0.0).astype(BF16))
        qh.append(jnp.where(low, pltpu.roll(qp, HEAD_DIM, 1), 0.0).astype(BF16))

    t_row = q0 + _iota((1, tq), 1)

    ckv = ckv_ref[0]
    cvt = cvt_ref[0]
    cmp_end = _iota((n_cmp, 1), 0) * NSA_CMP_STRIDE + (NSA_CMP_BLOCK - 1)
    valid_c = cmp_end <= t_row
    any_valid = t_row >= NSA_CMP_BLOCK - 1
    heads = range(N_HEADS)
    s_c = [jnp.where(valid_c, _nt(ckv, qh[h]), NEG_INF) for h in heads]
    m_c = [jnp.max(s_c[h], axis=0, keepdims=True) for h in heads]
    p_c = [jnp.exp2(s_c[h] - m_c[h]) for h in heads]
    pv_c = [_dot(cvt, p_c[h].astype(BF16)) for h in heads]
    inv_c = [jnp.where(any_valid, 1.0 / pv_c[h][HEAD_DIM:HEAD_DIM + 1], 0.0) for h in heads]
    o_c = [pv_c[h][:HEAD_DIM] * inv_c[h] for h in heads]
    p_sum = p_c[0] * inv_c[0]
    for h in range(1, N_HEADS):
        p_sum = p_sum + p_c[h] * inv_c[h]

    p_hi = p_sum.astype(BF16)
    p_lo = (p_sum - p_hi.astype(F32)).astype(BF16)
    c2s = c2s_ref[...]
    imp = _dot(c2s, p_hi) + _dot(c2s, p_lo)
    sid = _iota((n_sel, tq), 0)
    cur = t_row // NSA_SEL_BLOCK
    forced = (sid == cur) | (sid == 0)
    imp = jnp.where(forced, NSA_FORCED_SCORE, jnp.where(sid <= cur, imp, -1.0))
    imp_scr[...] = imp
    per_step = tq // NSA_SEL_BLOCK

    def rank_body(jj, rank):
        for u in range(per_step):
            j = jj * per_step + u
            vj = imp_scr[pl.ds(j, 1), :]
            ge = jnp.where(vj >= imp, 1.0, 0.0)
            gt = jnp.where(vj > imp, 1.0, 0.0)
            rank = rank + jnp.where(sid > j, ge, gt)
        return rank

    n_rank = jnp.where((i + 1) * per_step > top_n, i + 1, 0)
    rank = lax.fori_loop(0, n_rank, rank_body, jnp.zeros((n_sel, tq), F32))
    sel_scr[...] = jnp.where(rank < float(top_n), 1.0, 0.0)

    blocks_per_chunk = tk // NSA_SEL_BLOCK

    def sel_step(kstart, nkeys, blk_off, col0, causal):
        kstart = pl.multiple_of(kstart, nkeys)
        k_c = ks_ref[0, pl.ds(kstart, nkeys), :]
        vt_c = vst[:, pl.ds(kstart, nkeys)]
        sb = pl.multiple_of(kstart // NSA_SEL_BLOCK - blk_off, blocks_per_chunk)
        selc = sel_scr[pl.ds(sb, blocks_per_chunk), col0:]
        rows = [jnp.broadcast_to(selc[r:r + 1, :], (NSA_SEL_BLOCK, tq - col0))
                for r in range(blk_off, blk_off + nkeys // NSA_SEL_BLOCK)]
        valid = jnp.concatenate(rows, axis=0) > 0.5
        if causal:
            valid = valid & (kstart + _iota((nkeys, 1), 0) <= t_row[:, col0:])
        s, p, alpha = [None] * N_HEADS, [None] * N_HEADS, [None] * N_HEADS

        def scores(h):
            return _nt(k_c, qh[h][col0:])

        def update(h):
            as_scr[h, :, col0:] = alpha[h] * as_scr[h, :, col0:] + _dot(vt_c, p[h])

        s[0] = scores(0)
        for h in range(N_HEADS):
            if h + 1 < N_HEADS:
                s[h + 1] = scores(h + 1)
            ms_scr[h, :, col0:], p[h], alpha[h] = _online_softmax(jnp.where(valid, s[h], NEG_INF),
                                                                  ms_scr[h, :, col0:])
            if h >= 1:
                update(h - 1)
        update(N_HEADS - 1)

    ms_scr[...] = jnp.full(ms_scr.shape, NEG_INF, F32)
    as_scr[...] = jnp.zeros(as_scr.shape, F32)

    def full_chunk(c, carry):
        sel_step(c * tk, tk, 0, 0, False)
        return carry

    half_blocks = blocks_per_chunk // 2
    lax.fori_loop(0, q0 // tk, full_chunk, 0)
    sel_step(q0, tk // 2, 0, 0, True)
    sel_step(q0 + tk // 2, tk // 2, half_blocks, tq // 2, True)

    streams = []
    for a in range(tq // sub):
        qs = slice(a * sub, (a + 1) * sub)
        wstart = pl.multiple_of(jnp.maximum(q0 + a * sub - NSA_WINDOW, 0), sub)
        kpos_w = wstart + _iota((span, 1), 0)
        t_sub = t_row[:, qs]
        valid_w = (kpos_w <= t_sub) & (kpos_w > t_sub - NSA_WINDOW)
        streams += [(kw_ref[0, pl.ds(wstart, span), :], vwt[:, pl.ds(wstart, span)], valid_w, qh[h][qs])
                    for h in range(N_HEADS)]
    s_w = [jnp.where(valid, _nt(k_w, q_w), NEG_INF) for k_w, _, valid, q_w in streams]
    m_w = [jnp.max(sw, axis=0, keepdims=True) for sw in s_w]
    p_w = [jnp.exp2(sw - mw).astype(BF16) for sw, mw in zip(s_w, m_w)]
    pv_w = [_dot(st[1], pw) for st, pw in zip(streams, p_w)]
    o_w_parts = [pv[:HEAD_DIM] * (1.0 / pv[HEAD_DIM:HEAD_DIM + 1]) for pv in pv_w]
    o_w = [jnp.concatenate(o_w_parts[h::N_HEADS], axis=1) for h in range(N_HEADS)]

    gates = _sigmoid(sm_ref[0].T[0:16])
    ng = ng_ref[...]
    outs = []
    for h in range(N_HEADS):
        g_c = gates[3 * h:3 * h + 1]
        g_s = gates[3 * h + 1:3 * h + 2]
        g_w = gates[3 * h + 2:3 * h + 3]
        acc_s = as_scr[h]
        o_s = acc_s[:HEAD_DIM] * (1.0 / acc_s[HEAD_DIM:HEAD_DIM + 1])
        oh = g_c * o_c[h] + g_s * o_s + g_w * o_w[h]
        ms = jnp.mean(oh * oh, axis=0, keepdims=True)
        outs.append(oh * lax.rsqrt(ms + EPS) * ng)
    y = jnp.concatenate(
        [jnp.concatenate(outs[0:2], axis=0).T, jnp.concatenate(outs[2:4], axis=0).T], axis=1)
    o_ref[0] = (y * _silu(z_ref[0].astype(F32))).astype(o_ref.dtype)


def _nsa_attention(proj, small, ckv, cvt, c2s_t, norm_g, *, tq=512):
    bsz, seq, _ = proj.shape
    tk = tq
    assert tk // NSA_SEL_BLOCK == 8
    n_cmp = seq // NSA_CMP_STRIDE
    n_sel = seq // NSA_SEL_BLOCK
    top_n = min(NSA_TOP_N, n_sel)
    ng = jnp.broadcast_to(norm_g.reshape(HEAD_DIM, 1), (HEAD_DIM, tq)).astype(F32)
    kern = functools.partial(_nsa_kernel, seq=seq, tq=tq, tk=tk, top_n=top_n)
    return pl.pallas_call(
        kern,
        grid=(bsz, seq // tq),
        in_specs=[pl.BlockSpec((1, tq, 2 * LANES), lambda b, i: (b, i, PB_A_Q // 2)),
                  pl.BlockSpec((1, seq, LANES), lambda b, i: (b, 0, PB_A_KSVS)),
                  pl.BlockSpec((1, seq, LANES), lambda b, i: (b, 0, PB_A_KWVW)),
                  pl.BlockSpec((1, tq, LANES), lambda b, i: (b, i, 0)),
                  pl.BlockSpec((1, tq, 2 * LANES), lambda b, i: (b, i, PB_A_Z // 2)),
                  pl.BlockSpec((1, n_cmp, LANES), lambda b, i: (b, 0, 0)),
                  pl.BlockSpec((1, HEAD_DIM + SUM_ROWS, n_cmp), lambda b, i: (b, 0, 0)),
                  pl.BlockSpec((n_sel, n_cmp), lambda b, i: (0, 0)),
                  pl.BlockSpec((HEAD_DIM, tq), lambda b, i: (0, 0))],
        out_specs=pl.BlockSpec((1, tq, D_GROUP), lambda b, i: (b, i, 0)),
        out_shape=jax.ShapeDtypeStruct((bsz, seq, D_GROUP), BF16),
        scratch_shapes=[pltpu.VMEM((HEAD_DIM + SUM_ROWS, seq), BF16),
                        pltpu.VMEM((HEAD_DIM + SUM_ROWS, seq), BF16),
                        pltpu.VMEM((n_sel, tq), F32), pltpu.VMEM((n_sel, tq), F32),
                        pltpu.VMEM((N_HEADS, 1, tq), F32),
                        pltpu.VMEM((N_HEADS, HEAD_DIM + SUM_ROWS, tq), F32)],
        compiler_params=_cparams(("arbitrary", "arbitrary")),
        name="nsa_attention",
    )(proj, proj, proj, small, proj, ckv, cvt, c2s_t, ng)


def _diff_kernel(q_ref, k_ref, v_ref, z_ref, lam_ref, ng_ref, o_ref, vt, m_scr, acc_scr, *, seq, tq, tk,
                 lambda_init):
    i = pl.program_id(1)
    vrows = HEAD_DIM + SUM_ROWS
    cols = 2 * tq
    hq = tq // 2

    @pl.when(i == 0)
    def _():
        ones = jnp.ones((SUM_ROWS, LANES), BF16)
        for blk in range(seq // LANES):
            rows = slice(blk * LANES, (blk + 1) * LANES)
            parts = []
            for pair in range(N_HEADS // 2):
                v_t = v_ref[0, rows, pair * LANES:(pair + 1) * LANES].astype(F32).T.astype(BF16)
                parts += [v_t[:HEAD_DIM], ones, v_t[HEAD_DIM:], ones]
            vt[:, rows] = jnp.concatenate(parts, axis=0)

    lam = lam_ref[...]
    lam_full = (jnp.exp(jnp.sum(lam[0:1] * lam[1:2], axis=1, keepdims=True))
                - jnp.exp(jnp.sum(lam[2:3] * lam[3:4], axis=1, keepdims=True)) + lambda_init)

    q0 = pl.multiple_of(i * tq, tq)
    q = q_ref[0].astype(F32) * (DIFF_QK_DIM ** -0.5 * LOG2E)
    part = _iota((tq, LANES), 1) // DIFF_QK_DIM
    q2 = []
    for h in range(N_HEADS):
        qp = q[:, (h // 2) * LANES:(h // 2 + 1) * LANES]
        m0, m1 = [jnp.where(part == 2 * (h % 2) + c, qp, 0.0).astype(BF16) for c in range(2)]
        q2.append(jnp.concatenate([m0[:hq], m1[:hq], m0[hq:], m1[hq:]], axis=0))
    t_half = q0 + _iota((1, hq), 1)
    t_cols = jnp.concatenate([t_half, t_half, t_half + hq, t_half + hq], axis=1)

    def step(kstart, nkeys, col0, causal):
        kstart = pl.multiple_of(kstart, nkeys)
        kpos = kstart + _iota((nkeys, 1), 0)
        t_c = t_cols[:, col0:]
        s, p, alpha = [None] * N_HEADS, [None] * N_HEADS, [None] * N_HEADS

        def scores(h):
            lanes = slice((h // 2) * LANES, (h // 2 + 1) * LANES)
            return _nt(k_ref[0, pl.ds(kstart, nkeys), lanes], q2[h][col0:])

        def update(h):
            pv = _dot(vt[h * vrows:(h + 1) * vrows, pl.ds(kstart, nkeys)], p[h])
            acc_scr[h, :, col0:] = alpha[h] * acc_scr[h, :, col0:] + pv

        s[0] = scores(0)
        for h in range(N_HEADS):
            if h + 1 < N_HEADS:
                s[h + 1] = scores(h + 1)
            sh = jnp.where(kpos <= t_c, s[h], NEG_INF) if causal else s[h]
            m_scr[h, :, col0:], p[h], alpha[h] = _online_softmax(sh, m_scr[h, :, col0:])
            if h >= 1:
                update(h - 1)
        update(N_HEADS - 1)

    m_scr[...] = jnp.full(m_scr.shape, NEG_INF, F32)
    acc_scr[...] = jnp.zeros(acc_scr.shape, F32)

    def full_chunk(c, carry):
        step(c * tk, tk, 0, False)
        return carry

    lax.fori_loop(0, q0 // tk, full_chunk, 0)
    step(q0, tk // 2, 0, True)
    step(q0 + tk // 2, tk // 2, cols // 2, True)

    ng = ng_ref[...]
    outs = []
    for h in range(N_HEADS):
        acc = acc_scr[h]
        o_all = acc[:HEAD_DIM] * (1.0 / acc[HEAD_DIM:HEAD_DIM + 1])
        map0 = jnp.concatenate([o_all[:, 0:hq], o_all[:, 2 * hq:3 * hq]], axis=1)
        map1 = jnp.concatenate([o_all[:, hq:2 * hq], o_all[:, 3 * hq:4 * hq]], axis=1)
        o = map0 - lam_full * map1
        ms = jnp.mean(o * o, axis=0, keepdims=True)
        outs.append(o * lax.rsqrt(ms + EPS) * ng * (1.0 - lambda_init))
    y = jnp.concatenate(
        [jnp.concatenate(outs[0:2], axis=0).T, jnp.concatenate(outs[2:4], axis=0).T], axis=1)
    o_ref[0] = (y * _silu(z_ref[0].astype(F32))).astype(o_ref.dtype)


def _diff_attention(proj, lam, norm_g, layer_idx, *, tq=512):
    bsz, seq, _ = proj.shape
    tk = tq
    lambda_init = 0.8 - 0.6 * math.exp(-0.3 * layer_idx)
    ng = jnp.broadcast_to(norm_g.reshape(HEAD_DIM, 1), (HEAD_DIM, tq)).astype(F32)
    kern = functools.partial(_diff_kernel, seq=seq, tq=tq, tk=tk, lambda_init=lambda_init)
    return pl.pallas_call(
        kern,
        grid=(bsz, seq // tq),
        in_specs=[pl.BlockSpec((1, tq, D_GROUP), lambda b, i: (b, i, PB_B_Q // 2)),
                  pl.BlockSpec((1, seq, D_GROUP), lambda b, i: (b, 0, PB_B_K // 2)),
                  pl.BlockSpec((1, seq, D_GROUP), lambda b, i: (b, 0, PB_B_V // 2)),
                  pl.BlockSpec((1, tq, D_GROUP), lambda b, i: (b, i, PB_B_Z // 2)),
                  pl.BlockSpec((4, DIFF_QK_DIM), lambda b, i: (0, 0)),
                  pl.BlockSpec((HEAD_DIM, tq), lambda b, i: (0, 0))],
        out_specs=pl.BlockSpec((1, tq, D_GROUP), lambda b, i: (b, i, 0)),
        out_shape=jax.ShapeDtypeStruct((bsz, seq, D_GROUP), BF16),
        scratch_shapes=[pltpu.VMEM((N_HEADS * (HEAD_DIM + SUM_ROWS), seq), BF16),
                        pltpu.VMEM((N_HEADS, 1, 2 * tq), F32),
                        pltpu.VMEM((N_HEADS, HEAD_DIM + SUM_ROWS, 2 * tq), F32)],
        compiler_params=_cparams(("arbitrary", "arbitrary")),
        name="diff_attention",
    )(proj, proj, proj, proj, lam, ng)


def _causal_conv_silu(x_raw, xext, cw_ref, cb_ref):
    n = x_raw.shape[0]
    xext[CONV_PAD:CONV_PAD + n, :] = x_raw
    cw = cw_ref[...]
    acc = cb_ref[...] + cw[0:1] * xext[pl.ds(CONV_PAD - 3, n), :]
    for k in range(1, CONV_WIDTH):
        acc = acc + cw[k:k + 1] * xext[pl.ds(CONV_PAD - 3 + k, n), :]
    xext[0:CONV_PAD, :] = x_raw[n - CONV_PAD:, :]
    return _silu(acc)


def _interleave(chunks):
    live = list(chunks)
    while live:
        nxt = []
        for g in live:
            try:
                next(g)
                nxt.append(g)
            except StopIteration:
                pass
        live = nxt


def _rows_to_lanes(rows, first):
    n = rows.shape[1]
    parts = [rows, jnp.zeros((LANES - first - rows.shape[0], n), F32)]
    if first:
        parts.insert(0, jnp.zeros((first, n), F32))
    return jnp.concatenate(parts, axis=0).T


def _tri_upper(n):
    return jnp.where(_iota((n, n), 0) <= _iota((n, n), 1), 1.0, 0.0).astype(BF16)


def _expand_mat(first_lane, width):
    r = _iota((LANES, width), 0)
    c = _iota((LANES, width), 1)
    return jnp.where(r - first_lane == c // HEAD_DIM, 1.0, 0.0).astype(BF16)


def _group_sum_mat(width, group):
    r = _iota((width, width), 0)
    c = _iota((width, width), 1)
    return jnp.where(r // group == c // group, 1.0, 0.0).astype(BF16)


def _ssd_kernel(z_ref, xbc_ref, sm_ref, cw_ref, cb_ref, dtb_ref, alogc_ref, dsk_ref, ng_ref, o_ref,
                xext, st, *, nb):
    @pl.when(pl.program_id(1) == 0)
    def _():
        st[...] = jnp.zeros(st.shape, F32)
        xext[:, 0:CONV_PAD, :] = jnp.zeros((nb, CONV_PAD, SSM_XBC), F32)

    _interleave(_ssd_chunk(z_ref.at[bb], xbc_ref.at[bb], sm_ref.at[bb], cw_ref, cb_ref, dtb_ref, alogc_ref,
                           dsk_ref, ng_ref, o_ref.at[bb], xext.at[bb], st.at[bb]) for bb in range(nb))


def _ssd_chunk(z_ref, xbc_ref, sm_ref, cw_ref, cb_ref, dtb_ref, alogc_ref, dsk_ref, ng_ref, o_ref, xext, st):
    n = CHUNK
    xc = _causal_conv_silu(xbc_ref[...].astype(F32), xext, cw_ref, cb_ref)
    xs = xc[:, 0:D_GROUP]
    bm = xc[:, D_GROUP:D_GROUP + LANES]
    cm = xc[:, D_GROUP + LANES:D_GROUP + 2 * LANES]
    yield

    dt_c = _softplus(sm_ref[...] + dtb_ref[...])
    da_t = (dt_c * -jnp.exp(alogc_ref[...])).T[SM_DT - N_HEADS:SM_DT + N_HEADS]
    acs_t = _dot_lhs_f32(da_t, _tri_upper(n), terms=2)
    acs_rows = acs_t[N_HEADS:]
    dtx = _dot_lhs_f32(dt_c, _expand_mat(SM_DT, D_GROUP), terms=2)
    acs = _dot_lhs_f32(_rows_to_lanes(acs_t, SM_DT - N_HEADS), _expand_mat(SM_DT, D_GROUP), terms=2)
    acs_last = acs[n - 1:n, :]
    yield

    xdt = xs * dtx
    xdt_b = xdt.astype(BF16)
    bm_b = bm.astype(BF16)
    cm_b = cm.astype(BF16)
    lane = _iota((n, LANES), 1)
    causal = _iota((n, n), 0) >= _iota((n, n), 1)

    y_pairs = []
    for g in range(SSM_GROUPS):
        in_group = (lane // SSM_STATE) == g
        gmat = _nt(jnp.where(in_group, cm, 0.0).astype(BF16), bm_b)
        pair = []
        for h in (2 * g, 2 * g + 1):
            col = acs[:, HEAD_DIM * h:HEAD_DIM * h + 1]
            row = acs_rows[h:h + 1, :]
            decay = jnp.exp(jnp.where(causal, col - row, NEG_INF))
            pair.append(_dot((gmat * decay).astype(BF16), xdt_b[:, g * LANES:(g + 1) * LANES]))
        y_pairs.append(jnp.where(lane < HEAD_DIM, pair[0], pair[1]))
        yield
    y_diag = jnp.concatenate(y_pairs, axis=1)

    state = st[...]
    y_off = jnp.exp(acs) * _dot(cm_b, state.astype(BF16))
    decay_end = jnp.exp(acs_last - acs)
    upd = _dot(bm.T.astype(BF16), (decay_end * xdt).astype(BF16))
    own = (_iota(st.shape, 0) // SSM_STATE) == (_iota(st.shape, 1) // LANES)
    st[...] = jnp.where(own, jnp.exp(acs_last) * state + upd, 0.0)
    yield

    y = (y_diag + y_off + dsk_ref[...] * xs) * _silu(z_ref[...].astype(F32))
    ms = _dot_lhs_f32(y * y, _group_sum_mat(D_GROUP, LANES), terms=2) * (1.0 / LANES)
    o_ref[...] = (y * lax.rsqrt(ms + EPS) * ng_ref[...]).astype(o_ref.dtype)


def _ssd_mixer(proj, small, conv_w, conv_b, dt_bias, a_log, d_skip, norm_g):
    bsz, seq, _ = proj.shape
    n = CHUNK
    dtb = jnp.zeros((1, LANES), F32).at[0, SM_DT:SM_DT + N_HEADS].set(dt_bias)
    alog_c = jnp.zeros((1, LANES), F32).at[0, SM_DT:SM_DT + N_HEADS].set(a_log)
    dsk_x = jnp.repeat(d_skip, HEAD_DIM).reshape(1, D_GROUP)
    full = lambda b, c: (0, 0)
    nb = _seqs_per_step(bsz)
    return pl.pallas_call(
        functools.partial(_ssd_kernel, nb=nb),
        grid=(bsz // nb, seq // n),
        in_specs=[pl.BlockSpec((nb, n, D_GROUP), lambda b, c: (b, c, PB_C_Z // 2)),
                  pl.BlockSpec((nb, n, SSM_XBC), lambda b, c: (b, c, PB_C_XBC * LANES // SSM_XBC)),
                  pl.BlockSpec((nb, n, LANES), lambda b, c: (b, c, 0)),
                  pl.BlockSpec((CONV_WIDTH, SSM_XBC), full),
                  pl.BlockSpec((1, SSM_XBC), full),
                  pl.BlockSpec((1, LANES), full),
                  pl.BlockSpec((1, LANES), full),
                  pl.BlockSpec((1, D_GROUP), full),
                  pl.BlockSpec((1, D_GROUP), full)],
        out_specs=pl.BlockSpec((nb, n, D_GROUP), lambda b, c: (b, c, 0)),
        out_shape=jax.ShapeDtypeStruct((bsz, seq, D_GROUP), BF16),
        scratch_shapes=[pltpu.VMEM((nb, CONV_PAD + n, SSM_XBC), F32), pltpu.VMEM((nb, LANES, D_GROUP), F32)],
        compiler_params=_cparams(("arbitrary", "arbitrary")),
        name="ssd_mixer",
    )(proj, proj, small, conv_w, conv_b.reshape(1, SSM_XBC), dtb, alog_c, dsk_x, norm_g.reshape(1, D_GROUP))


def _mlstm_kernel(qk_ref, v_ref, sm_ref, og_ref, z_ref, cw_ref, cb_ref, ifb_ref, ng_ref, o_ref,
                  xext, c_st, n_st, m_st, *, nb):
    @pl.when(pl.program_id(1) == 0)
    def _():
        c_st[...] = jnp.zeros(c_st.shape, F32)
        n_st[...] = jnp.zeros(n_st.shape, F32)
        m_st[...] = jnp.zeros(m_st.shape, F32)
        xext[:, 0:CONV_PAD, :] = jnp.zeros((nb, CONV_PAD, 2 * D_GROUP), F32)

    _interleave(_mlstm_chunk(qk_ref.at[bb], v_ref.at[bb], sm_ref.at[bb], og_ref.at[bb], z_ref.at[bb], cw_ref,
                             cb_ref, ifb_ref, ng_ref, o_ref.at[bb], xext.at[bb], c_st.at[bb], n_st.at[bb],
                             m_st.at[bb]) for bb in range(nb))


def _mlstm_chunk(qk_ref, v_ref, sm_ref, og_ref, z_ref, cw_ref, cb_ref, ifb_ref, ng_ref, o_ref,
                 xext, c_st, n_st, m_st):
    n = CHUNK
    qk = _causal_conv_silu(qk_ref[...].astype(F32), xext, cw_ref, cb_ref)
    q = qk[:, 0:D_GROUP]
    k = qk[:, D_GROUP:] * (HEAD_DIM ** -0.5)
    q_b = q.astype(BF16)
    k_b = k.astype(BF16)
    v_b = v_ref[...]
    yield

    pre = sm_ref[...] + ifb_ref[...]
    logf = -_softplus(-pre)
    ig = _dot_lhs_f32(pre, _expand_mat(SM_I, D_GROUP), terms=2)
    gates_t = jnp.where(_iota((n, LANES), 1) >= SM_F, logf, pre).T[SM_I:SM_I + 2 * N_HEADS]
    cum_t = _dot_lhs_f32(gates_t, _tri_upper(n), terms=2)
    u_rows = gates_t[0:N_HEADS] - cum_t[N_HEADS:2 * N_HEADS]
    b = _dot_lhs_f32(_rows_to_lanes(cum_t, SM_I), _expand_mat(SM_F, D_GROUP), terms=2)
    b_last = b[n - 1:n, :]
    yield

    m_prev = m_st[...]
    c_prev = c_st[...]
    n_prev = n_st[...]

    lane = _iota((n, LANES), 1)
    causal = _iota((n, n), 0) >= _iota((n, n), 1)
    ones_b = jnp.ones((n, LANES), BF16)
    num_pairs, den_pairs, mt_pairs, wi_pairs = [], [], [], []
    for pr in range(N_HEADS // 2):
        lanes = slice(pr * LANES, (pr + 1) * LANES)
        qp = q[:, lanes]
        kp_b = k_b[:, lanes]
        rhs = jnp.concatenate([v_b[:, lanes], ones_b], axis=1)
        res, mts, wis = [], [], []
        for hh in range(2):
            h = 2 * pr + hh
            in_head = (lane // HEAD_DIM) == hh
            bcol = b[:, HEAD_DIM * h:HEAD_DIM * h + 1]
            dlog = jnp.where(causal, bcol + u_rows[h:h + 1, :], NEG_INF)
            inter = bcol + m_prev[:, HEAD_DIM * h:HEAD_DIM * h + 1]
            m_t = jnp.maximum(inter, jnp.max(dlog, axis=1, keepdims=True))
            s_qk = _nt(jnp.where(in_head, qp, 0.0).astype(BF16), kp_b) * jnp.exp(dlog - m_t)
            res.append(_dot(s_qk.astype(BF16), rhs))
            mts.append(jnp.broadcast_to(m_t, (n, LANES)))
            wis.append(jnp.broadcast_to(jnp.exp(inter - m_t), (n, LANES)))
            yield
        first = lane < HEAD_DIM
        num_pairs.append(jnp.where(first, res[0][:, :LANES], res[1][:, :LANES]))
        den_pairs.append(jnp.where(first, res[0][:, LANES:], res[1][:, LANES:]))
        mt_pairs.append(jnp.where(first, mts[0], mts[1]))
        wi_pairs.append(jnp.where(first, wis[0], wis[1]))
    num_intra = jnp.concatenate(num_pairs, axis=1)
    den_intra = jnp.concatenate(den_pairs, axis=1)
    m_t = jnp.concatenate(mt_pairs, axis=1)
    w_inter = jnp.concatenate(wi_pairs, axis=1)

    head_sum = _group_sum_mat(D_GROUP, HEAD_DIM)
    num = num_intra + w_inter * _dot(q_b, c_prev.astype(BF16))
    den = den_intra + w_inter * _dot_lhs_f32(q * n_prev, head_sum, terms=2)
    hid = num / jnp.maximum(jnp.abs(den), jnp.exp(-m_t))
    yield

    g_end = b_last - b + ig
    m_loc = jnp.max(g_end, axis=0, keepdims=True)
    m_new = jnp.maximum(b_last + m_prev, m_loc)
    a_prev = jnp.exp(b_last + m_prev - m_new)
    a_loc = jnp.exp(m_loc - m_new)
    kw = k * (jnp.exp(g_end - m_loc) * a_loc)
    kw_t = jnp.concatenate([kw[:, :LANES].T, kw[:, LANES:].T], axis=0)
    upd = _dot(kw_t.astype(BF16), v_b)
    own = (_iota(c_st.shape, 0) // HEAD_DIM) == (_iota(c_st.shape, 1) // HEAD_DIM)
    c_st[...] = jnp.where(own, a_prev * c_prev + upd, 0.0)
    n_st[...] = a_prev * n_prev + jnp.sum(kw, axis=0, keepdims=True)
    m_st[...] = m_new
    yield

    hm = _sigmoid(og_ref[...].astype(F32)) * hid
    ms = _dot_lhs_f32(hm * hm, head_sum, terms=2) * (1.0 / HEAD_DIM)
    o_ref[...] = (hm * lax.rsqrt(ms + EPS) * ng_ref[...] * _silu(z_ref[...].astype(F32))).astype(o_ref.dtype)


def _mlstm_mixer(proj, small, conv_w, conv_b, if_b, norm_g):
    bsz, seq, _ = proj.shape
    n = CHUNK
    ifb = jnp.zeros((1, LANES), F32).at[0, SM_I:SM_I + 2 * N_HEADS].set(if_b)
    ng = jnp.tile(norm_g, N_HEADS).reshape(1, D_GROUP)
    full = lambda b, c: (0, 0)
    nb = _seqs_per_step(bsz)
    return pl.pallas_call(
        functools.partial(_mlstm_kernel, nb=nb),
        grid=(bsz // nb, seq // n),
        in_specs=[pl.BlockSpec((nb, n, 2 * D_GROUP), lambda b, c: (b, c, PB_D_QK * LANES // (2 * D_GROUP))),
                  pl.BlockSpec((nb, n, D_GROUP), lambda b, c: (b, c, PB_D_V // 2)),
                  pl.BlockSpec((nb, n, LANES), lambda b, c: (b, c, 0)),
                  pl.BlockSpec((nb, n, D_GROUP), lambda b, c: (b, c, PB_D_O // 2)),
                  pl.BlockSpec((nb, n, D_GROUP), lambda b, c: (b, c, PB_D_Z // 2)),
                  pl.BlockSpec((CONV_WIDTH, 2 * D_GROUP), full),
                  pl.BlockSpec((1, 2 * D_GROUP), full),
                  pl.BlockSpec((1, LANES), full),
                  pl.BlockSpec((1, D_GROUP), full)],
        out_specs=pl.BlockSpec((nb, n, D_GROUP), lambda b, c: (b, c, 0)),
        out_shape=jax.ShapeDtypeStruct((bsz, seq, D_GROUP), BF16),
        scratch_shapes=[pltpu.VMEM((nb, CONV_PAD + n, 2 * D_GROUP), F32),
                        pltpu.VMEM((nb, D_GROUP, D_GROUP), F32),
                        pltpu.VMEM((nb, 1, D_GROUP), F32),
                        pltpu.VMEM((nb, 1, D_GROUP), F32)],
        compiler_params=_cparams(("arbitrary", "arbitrary")),
        name="mlstm_mixer",
    )(proj, proj, small, proj, proj, conv_w, conv_b.reshape(1, 2 * D_GROUP), ifb, ng)


def _outproj_kernel(ya_ref, yb_ref, yc_ref, yd_ref, x_ref, gate_ref, w_ref, fg_ref, o_ref, *, final):
    acc = _dot(ya_ref[0], w_ref[0:D_GROUP, :])
    acc = acc + _dot(yb_ref[0], w_ref[D_GROUP:2 * D_GROUP, :])
    acc = acc + _dot(yc_ref[0], w_ref[2 * D_GROUP:3 * D_GROUP, :])
    acc = acc + _dot(yd_ref[0], w_ref[3 * D_GROUP:4 * D_GROUP, :])
    out = x_ref[0] + gate_ref[0] * acc
    if final:
        ms = jnp.mean(out * out, axis=-1, keepdims=True)
        out = out * lax.rsqrt(ms + EPS) * fg_ref[...]
    o_ref[0] = out


def _out_projection(ys, x, gate, w_bf, final_g, final):
    bsz, seq, d = x.shape
    tm = min(1024, seq)
    yspec = pl.BlockSpec((1, tm, D_GROUP), lambda b, i: (b, i, 0))
    return pl.pallas_call(
        functools.partial(_outproj_kernel, final=final),
        grid=(bsz, seq // tm),
        in_specs=[yspec, yspec, yspec, yspec,
                  pl.BlockSpec((1, tm, d), lambda b, i: (b, i, 0)),
                  pl.BlockSpec((1, 1, d), lambda b, i: (b, 0, 0)),
                  pl.BlockSpec((N_HEADS * D_GROUP, d), lambda b, i: (0, 0)),
                  pl.BlockSpec((1, d), lambda b, i: (0, 0))],
        out_specs=pl.BlockSpec((1, tm, d), lambda b, i: (b, i, 0)),
        out_shape=jax.ShapeDtypeStruct((bsz, seq, d), F32),
        compiler_params=_cparams(("arbitrary", "arbitrary")),
        name="out_projection",
    )(*ys, x, gate, w_bf, final_g.reshape(1, d))


def _cmp_to_sel_t(seq):
    n_cmp = (seq - NSA_CMP_BLOCK) // NSA_CMP_STRIDE + 1
    n_sel = seq // NSA_SEL_BLOCK
    start = np.arange(n_cmp)[:, None] * NSA_CMP_STRIDE
    sel_start = np.arange(n_sel)[None, :] * NSA_SEL_BLOCK
    overlap = np.clip(np.minimum(start + NSA_CMP_BLOCK, sel_start + NSA_SEL_BLOCK)
                      - np.maximum(start, sel_start), 0, None)
    out = np.zeros((n_sel, seq // NSA_CMP_STRIDE), np.float32)
    out[:, :n_cmp] = (overlap / NSA_CMP_BLOCK).T
    return jnp.asarray(out, BF16)


def _mixers(proj, small, kcvc, layer_idx, p):
    seq = proj.shape[1]
    pos = p['nsa_cmp_pos'].reshape(1, NSA_CMP_BLOCK * HEAD_DIM)
    ckv, cvt = _nsa_compress(kcvc, pos, p['nsa_ck_w1'], p['nsa_ck_w2'], p['nsa_cv_w1'], p['nsa_cv_w2'])
    y_a = _nsa_attention(proj, small, ckv, cvt, _cmp_to_sel_t(seq), p['nsa_norm_g'])
    y_b = _diff_attention(proj, p['diff_lam'], p['diff_norm_g'], layer_idx)
    y_c = _ssd_mixer(proj, small, p['ssm_conv_w'], p['ssm_conv_b'], p['ssm_dt_bias'], p['ssm_a_log'],
                     p['ssm_d'], p['ssm_norm_g'])
    y_d = _mlstm_mixer(proj, small, p['ml_conv_w'], p['ml_conv_b'], p['ml_if_b'], p['ml_norm_g'])
    return y_a, y_b, y_c, y_d


_LAYER_PARAMS = ('nsa_cmp_pos', 'nsa_ck_w1', 'nsa_ck_w2', 'nsa_cv_w1', 'nsa_cv_w2', 'nsa_norm_g',
                 'diff_lam', 'diff_norm_g', 'ssm_conv_w', 'ssm_conv_b', 'ssm_dt_bias', 'ssm_a_log',
                 'ssm_d', 'ssm_norm_g', 'ml_conv_w', 'ml_conv_b', 'ml_if_b', 'ml_norm_g')


def kernel(x, c, norm_g, ada_w, ada_b, w_in, w_out, nsa_cmp_pos, nsa_ck_w1, nsa_ck_w2, nsa_cv_w1, nsa_cv_w2, nsa_norm_g, diff_lam, diff_norm_g, ssm_conv_w, ssm_conv_b, ssm_dt_bias, ssm_a_log, ssm_d, ssm_norm_g, ml_conv_w, ml_conv_b, ml_if_b, ml_norm_g, final_g):
    stacked = dict(nsa_cmp_pos=nsa_cmp_pos, nsa_ck_w1=nsa_ck_w1, nsa_ck_w2=nsa_ck_w2, nsa_cv_w1=nsa_cv_w1,
                   nsa_cv_w2=nsa_cv_w2, nsa_norm_g=nsa_norm_g, diff_lam=diff_lam, diff_norm_g=diff_norm_g,
                   ssm_conv_w=ssm_conv_w, ssm_conv_b=ssm_conv_b, ssm_dt_bias=ssm_dt_bias,
                   ssm_a_log=ssm_a_log, ssm_d=ssm_d, ssm_norm_g=ssm_norm_g, ml_conv_w=ml_conv_w,
                   ml_conv_b=ml_conv_b, ml_if_b=ml_if_b, ml_norm_g=ml_norm_g)
    depth = w_in.shape[0]
    bsz, seq, d = x.shape
    mod = _ada_modulation(c, ada_w, ada_b)
    for l in range(depth):
        p = {name: stacked[name][l] for name in _LAYER_PARAMS}
        shift = mod[l, :, 0:d].reshape(bsz, 1, d)
        scale = mod[l, :, d:2 * d].reshape(bsz, 1, d)
        gate = mod[l, :, 2 * d:3 * d].reshape(bsz, 1, d)
        w_bf = _relayout_w_in(w_in[l]).astype(BF16)
        proj, small, kcvc = _in_projection(x, norm_g[l], scale, shift, w_bf)
        ys = _mixers(proj, small, kcvc, l, p)
        x = _out_projection(ys, x, gate, w_out[l].astype(BF16), final_g, final=(l == depth - 1))
    return x
```

```python
import functools
import math

import numpy as np
import jax
import jax.numpy as jnp
from jax import lax
from jax.experimental import pallas as pl
from jax.experimental.pallas import tpu as pltpu

F32 = jnp.float32
BF16 = jnp.bfloat16

D_MODEL = 1024
N_HEADS = 4
HEAD_DIM = 64
D_GROUP = N_HEADS * HEAD_DIM
NEG_INF = -1e30
EPS = 1e-6

NSA_CMP_BLOCK = 32
NSA_CMP_STRIDE = 16
NSA_SEL_BLOCK = 64
NSA_TOP_N = 16
NSA_WINDOW = 512
NSA_CMP_HIDDEN = 128
NSA_FORCED_SCORE = 1e4

DIFF_QK_DIM = HEAD_DIM // 2
SSM_STATE = 64
SSM_GROUPS = 2
SSM_XBC = D_GROUP + 2 * SSM_GROUPS * SSM_STATE
CONV_WIDTH = 4
CHUNK = 128
CONV_PAD = 8

LANES = 128
D_PROJ = 4096

_SPLITS = (
    D_GROUP, HEAD_DIM, HEAD_DIM, HEAD_DIM, HEAD_DIM, HEAD_DIM, HEAD_DIM, 3 * N_HEADS, D_GROUP,
    D_GROUP, D_GROUP, D_GROUP, D_GROUP,
    D_GROUP, SSM_XBC, N_HEADS,
    2 * D_GROUP, D_GROUP, 2 * N_HEADS, D_GROUP, D_GROUP,
)
_OFFS = [0] + [int(o) for o in np.cumsum(_SPLITS)]
(_A_Q, _A_KC, _A_VC, _A_KS, _A_VS, _A_KW, _A_VW, _A_G, _A_Z,
 _B_Q, _B_K, _B_V, _B_Z, _C_Z, _C_XBC, _C_DT, _D_QK, _D_V, _D_IF, _D_O, _D_Z) = range(21)

SM_GATE = 0
SM_DT = 12
SM_I = 16
SM_F = 20

PB_A_Q = 0
PB_A_KCVC = 2
PB_A_KSVS = 3
PB_A_KWVW = 4
PB_SMALL = 5
PB_A_Z = 6
PB_B_Q = 8
PB_B_K = 10
PB_B_V = 12
PB_B_Z = 14
PB_C_XBC = 16
PB_C_Z = 20
PB_D_V = 22
PB_D_QK = 24
PB_D_O = 28
PB_D_Z = 30

VMEM_LIMIT = 56 * 1024 * 1024
LOG2E = 1.4426950408889634
SUM_ROWS = 16
WINDOW_SUB = 256


def _cparams(sem):
    return pltpu.CompilerParams(dimension_semantics=sem, vmem_limit_bytes=VMEM_LIMIT)


def _relayout_w_in(w_in):
    def col(i):
        return w_in[:, _OFFS[i]:_OFFS[i + 1]]
    d = w_in.shape[0]
    small = jnp.concatenate(
        [col(_A_G), col(_C_DT), col(_D_IF), jnp.zeros((d, LANES - 24), w_in.dtype)], axis=1)
    parts = [col(_A_Q), col(_A_KC), col(_A_VC), col(_A_KS), col(_A_VS), col(_A_KW), col(_A_VW),
             small, col(_A_Z),
             col(_B_Q), col(_B_K), col(_B_V), col(_B_Z),
             col(_C_XBC), col(_C_Z),
             col(_D_V), col(_D_QK), col(_D_O), col(_D_Z)]
    out = jnp.concatenate(parts, axis=1)
    assert out.shape[1] == D_PROJ
    return out


def _dot(a, b):
    return jnp.dot(a, b, preferred_element_type=F32)


def _nt(a, b):
    return lax.dot_general(a, b, (((1,), (1,)), ((), ())), preferred_element_type=F32)


def _split3(a):
    hi = a.astype(BF16)
    r1 = a - hi.astype(F32)
    mid = r1.astype(BF16)
    lo = (r1 - mid.astype(F32)).astype(BF16)
    return hi, mid, lo


def _dot_lhs_f32(a, b_exact, terms=3):
    hi, mid, lo = _split3(a)
    out = _dot(hi, b_exact) + _dot(mid, b_exact)
    return out + _dot(lo, b_exact) if terms == 3 else out


def _seqs_per_step(bsz, most):
    return math.gcd(bsz, most)


def _sigmoid(x):
    return 1.0 / (1.0 + jnp.exp(-x))


def _silu(x):
    return x * _sigmoid(x)


def _softplus(x):
    return jnp.maximum(x, 0.0) + jnp.log(1.0 + jnp.exp(-jnp.abs(x)))


def _iota(shape, dim):
    return lax.broadcasted_iota(jnp.int32, shape, dim)


def _ada_kernel(c_ref, w_ref, b_ref, o_ref):
    ca = _silu(c_ref[...])
    w = w_ref[0]
    c_hi, c_mid, _ = _split3(ca)
    w_hi, w_mid, _ = _split3(w)
    acc = _dot(c_hi, w_hi) + _dot(c_hi, w_mid) + _dot(c_mid, w_hi)
    o_ref[0] = acc + b_ref[0]


def _ada_modulation(c, ada_w, ada_b):
    depth, d, d3 = ada_w.shape
    bsz = c.shape[0]
    nb = d3 // d
    return pl.pallas_call(
        _ada_kernel,
        grid=(depth, nb),
        in_specs=[pl.BlockSpec((bsz, d), lambda l, j: (0, 0)),
                  pl.BlockSpec((1, d, d), lambda l, j: (l, 0, j)),
                  pl.BlockSpec((1, 1, d), lambda l, j: (l, 0, j))],
        out_specs=pl.BlockSpec((1, bsz, d), lambda l, j: (l, 0, j)),
        out_shape=jax.ShapeDtypeStruct((depth, bsz, d3), F32),
        compiler_params=_cparams(("arbitrary", "arbitrary")),
        name="ada_modulation",
    )(c, ada_w, ada_b.reshape(depth, 1, d3))


def _inproj_kernel(x_ref, g_ref, sc_ref, sh_ref, w_ref, o_ref, small_ref, kcvc_ref, *, tn):
    x = x_ref[0]
    ms = jnp.mean(x * x, axis=-1, keepdims=True)
    y = x * lax.rsqrt(ms + EPS) * g_ref[...]
    h = (y * (1.0 + sc_ref[0]) + sh_ref[0]).astype(BF16)
    for j in range(D_PROJ // tn):
        acc = _dot(h, w_ref[:, j * tn:(j + 1) * tn])
        o_ref[0, :, j * tn:(j + 1) * tn] = acc.astype(o_ref.dtype)
        if j == (PB_SMALL * LANES) // tn:
            off = PB_SMALL * LANES - j * tn
            small_ref[0] = acc[:, off:off + LANES]
        if j == (PB_A_KCVC * LANES) // tn:
            off = PB_A_KCVC * LANES - j * tn
            kcvc_ref[0] = acc[:, off:off + LANES]


def _in_projection(x, norm_g, scale, shift, w_bf):
    bsz, seq, d = x.shape
    tm = min(1024, seq)
    tn = 1024
    return pl.pallas_call(
        functools.partial(_inproj_kernel, tn=tn),
        grid=(bsz, seq // tm),
        in_specs=[pl.BlockSpec((1, tm, d), lambda b, i: (b, i, 0)),
                  pl.BlockSpec((1, d), lambda b, i: (0, 0)),
                  pl.BlockSpec((1, 1, d), lambda b, i: (b, 0, 0)),
                  pl.BlockSpec((1, 1, d), lambda b, i: (b, 0, 0)),
                  pl.BlockSpec((d, D_PROJ), lambda b, i: (0, 0))],
        out_specs=[pl.BlockSpec((1, tm, D_PROJ), lambda b, i: (b, i, 0)),
                   pl.BlockSpec((1, tm, LANES), lambda b, i: (b, i, 0)),
                   pl.BlockSpec((1, tm, LANES), lambda b, i: (b, i, 0))],
        out_shape=[jax.ShapeDtypeStruct((bsz, seq, D_PROJ), BF16),
                   jax.ShapeDtypeStruct((bsz, seq, LANES), F32),
                   jax.ShapeDtypeStruct((bsz, seq, LANES), F32)],
        compiler_params=_cparams(("arbitrary", "arbitrary")),
        name="in_projection",
    )(x, norm_g.reshape(1, d), scale, shift, w_bf)


def _compress_kernel(x_ref, pos_ref, wall_ref, kw1_ref, kw2_ref, vw1_ref, vw2_ref, kv_ref, vt_ref):
    n = x_ref.shape[1] // NSA_CMP_STRIDE
    acc = jnp.zeros((n, 4 * NSA_CMP_HIDDEN), F32)
    for r in range(NSA_CMP_STRIDE):
        x_r = x_ref[0, pl.ds(r, n, stride=NSA_CMP_STRIDE), :].astype(BF16)
        acc = acc + _dot(x_r, wall_ref[r])
    pos = jnp.broadcast_to(pos_ref[...], (8, NSA_CMP_BLOCK * HEAD_DIM)).astype(BF16)

    def one(first, second, w1_ref, w2_ref):
        second = pltpu.roll(second, n - 1, 0)
        bias = _dot(pos, w1_ref[...].astype(BF16))[0:1]
        hid = _silu(first + second + bias)
        return _dot(hid.astype(BF16), w2_ref[...].astype(BF16))

    hw = NSA_CMP_HIDDEN
    kc = one(acc[:, 0:hw], acc[:, hw:2 * hw], kw1_ref, kw2_ref)
    vc = one(acc[:, 2 * hw:3 * hw], acc[:, 3 * hw:4 * hw], vw1_ref, vw2_ref)
    kv = jnp.concatenate([kc, vc], axis=1)
    kv_ref[0] = kv.astype(BF16)
    ones = jnp.ones((SUM_ROWS, kv.shape[0]), BF16)
    vt_ref[0] = jnp.concatenate([kv.T[HEAD_DIM:].astype(BF16), ones], axis=0)


def _compress_slot_weights(kw1, vw1):
    k1 = kw1.reshape(2, NSA_CMP_STRIDE, HEAD_DIM, NSA_CMP_HIDDEN)
    v1 = vw1.reshape(2, NSA_CMP_STRIDE, HEAD_DIM, NSA_CMP_HIDDEN)
    zero = jnp.zeros_like(k1[0])
    top = jnp.concatenate([k1[0], k1[1], zero, zero], axis=-1)
    bottom = jnp.concatenate([zero, zero, v1[0], v1[1]], axis=-1)
    return jnp.concatenate([top, bottom], axis=1).astype(BF16)


def _nsa_compress(kcvc, pos, kw1, kw2, vw1, vw2):
    bsz, seq, _ = kcvc.shape
    n = seq // NSA_CMP_STRIDE
    wall = _compress_slot_weights(kw1, vw1)
    full2 = lambda b: (0, 0)
    return pl.pallas_call(
        _compress_kernel,
        grid=(bsz,),
        in_specs=[pl.BlockSpec((1, seq, LANES), lambda b: (b, 0, 0)),
                  pl.BlockSpec(pos.shape, full2),
                  pl.BlockSpec(wall.shape, lambda b: (0, 0, 0)),
                  pl.BlockSpec(kw1.shape, full2), pl.BlockSpec(kw2.shape, full2),
                  pl.BlockSpec(vw1.shape, full2), pl.BlockSpec(vw2.shape, full2)],
        out_specs=[pl.BlockSpec((1, n, LANES), lambda b: (b, 0, 0)),
                   pl.BlockSpec((1, HEAD_DIM + SUM_ROWS, n), lambda b: (b, 0, 0))],
        out_shape=[jax.ShapeDtypeStruct((bsz, n, LANES), BF16),
                   jax.ShapeDtypeStruct((bsz, HEAD_DIM + SUM_ROWS, n), BF16)],
        compiler_params=_cparams(("arbitrary",)),
        name="nsa_compress",
    )(kcvc, pos, wall, kw1, kw2, vw1, vw2)


def _online_softmax(s, m):
    m_new = jnp.maximum(m, jnp.max(s, axis=0, keepdims=True))
    return m_new, jnp.exp2(s - m_new).astype(BF16), jnp.exp2(m - m_new)


def _nsa_kernel(q_ref, ks_ref, kw_ref, sm_ref, z_ref, ckv_ref, cvt_ref, c2s_ref, ng_ref, o_ref,
                vst, vwt, sel_scr, imp_scr, ms_scr, as_scr, *, seq, tq, tk, top_n):
    i = pl.program_id(1)
    n_cmp = seq // NSA_CMP_STRIDE
    n_sel = seq // NSA_SEL_BLOCK
    sub = min(tq, WINDOW_SUB)
    span = NSA_WINDOW + sub

    @pl.when(i == 0)
    def _():
        ones = jnp.ones((SUM_ROWS, LANES), BF16)
        for blk in range(seq // LANES):
            rows = slice(blk * LANES, (blk + 1) * LANES)
            ks = ks_ref[0, rows, :].astype(F32)
            kw = kw_ref[0, rows, :].astype(F32)
            vst[:, rows] = jnp.concatenate([ks.T[HEAD_DIM:].astype(BF16), ones], axis=0)
            vwt[:, rows] = jnp.concatenate([kw.T[HEAD_DIM:].astype(BF16), ones], axis=0)

    q0 = pl.multiple_of(i * tq, tq)
    q = q_ref[0].astype(F32) * (HEAD_DIM ** -0.5 * LOG2E)
    low = _iota((tq, LANES), 1) < HEAD_DIM
    qh = []
    for pair in range(N_HEADS // 2):
        qp = q[:, pair * LANES:(pair + 1) * LANES]
        qh.append(jnp.where(low, qp, 0.0).astype(BF16))
        qh.append(jnp.where(low, pltpu.roll(qp, HEAD_DIM, 1), 0.0).astype(BF16))

    t_row = q0 + _iota((1, tq), 1)

    ckv = ckv_ref[0]
    cvt = cvt_ref[0]
    cmp_end = _iota((n_cmp, 1), 0) * NSA_CMP_STRIDE + (NSA_CMP_BLOCK - 1)
    valid_c = cmp_end <= t_row
    any_valid = t_row >= NSA_CMP_BLOCK - 1
    heads = range(N_HEADS)
    s_c = [jnp.where(valid_c, _nt(ckv, qh[h]), NEG_INF) for h in heads]
    m_c = [jnp.max(s_c[h], axis=0, keepdims=True) for h in heads]
    p_c = [jnp.exp2(s_c[h] - m_c[h]) for h in heads]
    pv_c = [_dot(cvt, p_c[h].astype(BF16)) for h in heads]
    inv_c = [jnp.where(any_valid, 1.0 / pv_c[h][HEAD_DIM:HEAD_DIM + 1], 0.0) for h in heads]
    o_c = [pv_c[h][:HEAD_DIM] * inv_c[h] for h in heads]
    p_sum = p_c[0] * inv_c[0]
    for h in range(1, N_HEADS):
        p_sum = p_sum + p_c[h] * inv_c[h]

    p_hi = p_sum.astype(BF16)
    p_lo = (p_sum - p_hi.astype(F32)).astype(BF16)
    c2s = c2s_ref[...]
    imp = _dot(c2s, p_hi) + _dot(c2s, p_lo)
    sid = _iota((n_sel, tq), 0)
    cur = t_row // NSA_SEL_BLOCK
    forced = (sid == cur) | (sid == 0)
    imp = jnp.where(forced, NSA_FORCED_SCORE, jnp.where(sid <= cur, imp, -1.0))
    imp_scr[...] = imp
    per_step = tq // NSA_SEL_BLOCK

    def rank_body(jj, rank):
        for u in range(per_step):
            j = jj * per_step + u
            vj = imp_scr[pl.ds(j, 1), :]
            ge = jnp.where(vj >= imp, 1.0, 0.0)
            gt = jnp.where(vj > imp, 1.0, 0.0)
            rank = rank + jnp.where(sid > j, ge, gt)
        return rank

    n_rank = jnp.where((i + 1) * per_step > top_n, i + 1, 0)
    rank = lax.fori_loop(0, n_rank, rank_body, jnp.zeros((n_sel, tq), F32))
    sel_scr[...] = jnp.where(rank < float(top_n), 1.0, 0.0)

    blocks_per_chunk = tk // NSA_SEL_BLOCK

    def sel_step(kstart, nkeys, blk_off, col0, causal):
        kstart = pl.multiple_of(kstart, nkeys)
        k_c = ks_ref[0, pl.ds(kstart, nkeys), :]
        vt_c = vst[:, pl.ds(kstart, nkeys)]
        sb = pl.multiple_of(kstart // NSA_SEL_BLOCK - blk_off, blocks_per_chunk)
        selc = sel_scr[pl.ds(sb, blocks_per_chunk), col0:]
        rows = [jnp.broadcast_to(selc[r:r + 1, :], (NSA_SEL_BLOCK, tq - col0))
                for r in range(blk_off, blk_off + nkeys // NSA_SEL_BLOCK)]
        valid = jnp.concatenate(rows, axis=0) > 0.5
        if causal:
            valid = valid & (kstart + _iota((nkeys, 1), 0) <= t_row[:, col0:])
        s, p, alpha = [None] * N_HEADS, [None] * N_HEADS, [None] * N_HEADS

        def scores(h):
            return _nt(k_c, qh[h][col0:])

        def update(h):
            as_scr[h, :, col0:] = alpha[h] * as_scr[h, :, col0:] + _dot(vt_c, p[h])

        s[0] = scores(0)
        for h in range(N_HEADS):
            if h + 1 < N_HEADS:
                s[h + 1] = scores(h + 1)
            ms_scr[h, :, col0:], p[h], alpha[h] = _online_softmax(jnp.where(valid, s[h], NEG_INF),
                                                                  ms_scr[h, :, col0:])
            if h >= 1:
                update(h - 1)
        update(N_HEADS - 1)

    ms_scr[...] = jnp.full(ms_scr.shape, NEG_INF, F32)
    as_scr[...] = jnp.zeros(as_scr.shape, F32)

    def full_chunk(c, carry):
        sel_step(c * tk, tk, 0, 0, False)
        return carry

    half_blocks = blocks_per_chunk // 2
    lax.fori_loop(0, q0 // tk, full_chunk, 0)
    sel_step(q0, tk // 2, 0, 0, True)
    sel_step(q0 + tk // 2, tk // 2, half_blocks, tq // 2, True)

    streams = []
    for a in range(tq // sub):
        qs = slice(a * sub, (a + 1) * sub)
        wstart = pl.multiple_of(jnp.maximum(q0 + a * sub - NSA_WINDOW, 0), sub)
        kpos_w = wstart + _iota((span, 1), 0)
        t_sub = t_row[:, qs]
        valid_w = (kpos_w <= t_sub) & (kpos_w > t_sub - NSA_WINDOW)
        streams += [(kw_ref[0, pl.ds(wstart, span), :], vwt[:, pl.ds(wstart, span)], valid_w, qh[h][qs])
                    for h in range(N_HEADS)]
    s_w = [jnp.where(valid, _nt(k_w, q_w), NEG_INF) for k_w, _, valid, q_w in streams]
    m_w = [jnp.max(sw, axis=0, keepdims=True) for sw in s_w]
    p_w = [jnp.exp2(sw - mw).astype(BF16) for sw, mw in zip(s_w, m_w)]
    pv_w = [_dot(st[1], pw) for st, pw in zip(streams, p_w)]
    o_w_parts = [pv[:HEAD_DIM] * (1.0 / pv[HEAD_DIM:HEAD_DIM + 1]) for pv in pv_w]
    o_w = [jnp.concatenate(o_w_parts[h::N_HEADS], axis=1) for h in range(N_HEADS)]

    gates = _sigmoid(sm_ref[0].T[0:16])
    ng = ng_ref[...]
    outs = []
    for h in range(N_HEADS):
        g_c = gates[3 * h:3 * h + 1]
        g_s = gates[3 * h + 1:3 * h + 2]
        g_w = gates[3 * h + 2:3 * h + 3]
        acc_s = as_scr[h]
        o_s = acc_s[:HEAD_DIM] * (1.0 / acc_s[HEAD_DIM:HEAD_DIM + 1])
        oh = g_c * o_c[h] + g_s * o_s + g_w * o_w[h]
        ms = jnp.mean(oh * oh, axis=0, keepdims=True)
        outs.append(oh * lax.rsqrt(ms + EPS) * ng)
    y = jnp.concatenate(
        [jnp.concatenate(outs[0:2], axis=0).T, jnp.concatenate(outs[2:4], axis=0).T], axis=1)
    o_ref[0] = (y * _silu(z_ref[0].astype(F32))).astype(o_ref.dtype)


def _nsa_attention(proj, small, ckv, cvt, c2s_t, norm_g, *, tq=512):
    bsz, seq, _ = proj.shape
    tk = tq
    assert tk // NSA_SEL_BLOCK == 8
    n_cmp = seq // NSA_CMP_STRIDE
    n_sel = seq // NSA_SEL_BLOCK
    top_n = min(NSA_TOP_N, n_sel)
    ng = jnp.broadcast_to(norm_g.reshape(HEAD_DIM, 1), (HEAD_DIM, tq)).astype(F32)
    kern = functools.partial(_nsa_kernel, seq=seq, tq=tq, tk=tk, top_n=top_n)
    return pl.pallas_call(
        kern,
        grid=(bsz, seq // tq),
        in_specs=[pl.BlockSpec((1, tq, 2 * LANES), lambda b, i: (b, i, PB_A_Q // 2)),
                  pl.BlockSpec((1, seq, LANES), lambda b, i: (b, 0, PB_A_KSVS)),
                  pl.BlockSpec((1, seq, LANES), lambda b, i: (b, 0, PB_A_KWVW)),
                  pl.BlockSpec((1, tq, LANES), lambda b, i: (b, i, 0)),
                  pl.BlockSpec((1, tq, 2 * LANES), lambda b, i: (b, i, PB_A_Z // 2)),
                  pl.BlockSpec((1, n_cmp, LANES), lambda b, i: (b, 0, 0)),
                  pl.BlockSpec((1, HEAD_DIM + SUM_ROWS, n_cmp), lambda b, i: (b, 0, 0)),
                  pl.BlockSpec((n_sel, n_cmp), lambda b, i: (0, 0)),
                  pl.BlockSpec((HEAD_DIM, tq), lambda b, i: (0, 0))],
        out_specs=pl.BlockSpec((1, tq, D_GROUP), lambda b, i: (b, i, 0)),
        out_shape=jax.ShapeDtypeStruct((bsz, seq, D_GROUP), BF16),
        scratch_shapes=[pltpu.VMEM((HEAD_DIM + SUM_ROWS, seq), BF16),
                        pltpu.VMEM((HEAD_DIM + SUM_ROWS, seq), BF16),
                        pltpu.VMEM((n_sel, tq), F32), pltpu.VMEM((n_sel, tq), F32),
                        pltpu.VMEM((N_HEADS, 1, tq), F32),
                        pltpu.VMEM((N_HEADS, HEAD_DIM + SUM_ROWS, tq), F32)],
        compiler_params=_cparams(("arbitrary", "arbitrary")),
        name="nsa_attention",
    )(proj, proj, proj, small, proj, ckv, cvt, c2s_t, ng)


def _diff_kernel(q_ref, k_ref, v_ref, z_ref, lam_ref, ng_ref, o_ref, vt, m_scr, acc_scr, *, seq, tq, tk,
                 lambda_init):
    i = pl.program_id(1)
    vrows = HEAD_DIM + SUM_ROWS
    cols = 2 * tq
    hq = tq // 2

    @pl.when(i == 0)
    def _():
        ones = jnp.ones((SUM_ROWS, LANES), BF16)
        for blk in range(seq // LANES):
            rows = slice(blk * LANES, (blk + 1) * LANES)
            parts = []
            for pair in range(N_HEADS // 2):
                v_t = v_ref[0, rows, pair * LANES:(pair + 1) * LANES].astype(F32).T.astype(BF16)
                parts += [v_t[:HEAD_DIM], ones, v_t[HEAD_DIM:], ones]
            vt[:, rows] = jnp.concatenate(parts, axis=0)

    lam = lam_ref[...]
    lam_full = (jnp.exp(jnp.sum(lam[0:1] * lam[1:2], axis=1, keepdims=True))
                - jnp.exp(jnp.sum(lam[2:3] * lam[3:4], axis=1, keepdims=True)) + lambda_init)

    q0 = pl.multiple_of(i * tq, tq)
    q = q_ref[0].astype(F32) * (DIFF_QK_DIM ** -0.5 * LOG2E)
    part = _iota((tq, LANES), 1) // DIFF_QK_DIM
    q2 = []
    for h in range(N_HEADS):
        qp = q[:, (h // 2) * LANES:(h // 2 + 1) * LANES]
        m0, m1 = [jnp.where(part == 2 * (h % 2) + c, qp, 0.0).astype(BF16) for c in range(2)]
        q2.append(jnp.concatenate([m0[:hq], m1[:hq], m0[hq:], m1[hq:]], axis=0))
    t_half = q0 + _iota((1, hq), 1)
    t_cols = jnp.concatenate([t_half, t_half, t_half + hq, t_half + hq], axis=1)

    def step(kstart, nkeys, col0, causal):
        kstart = pl.multiple_of(kstart, nkeys)
        kpos = kstart + _iota((nkeys, 1), 0)
        t_c = t_cols[:, col0:]
        s, p, alpha = [None] * N_HEADS, [None] * N_HEADS, [None] * N_HEADS

        def scores(h):
            lanes = slice((h // 2) * LANES, (h // 2 + 1) * LANES)
            return _nt(k_ref[0, pl.ds(kstart, nkeys), lanes], q2[h][col0:])

        def update(h):
            pv = _dot(vt[h * vrows:(h + 1) * vrows, pl.ds(kstart, nkeys)], p[h])
            acc_scr[h, :, col0:] = alpha[h] * acc_scr[h, :, col0:] + pv

        s[0] = scores(0)
        for h in range(N_HEADS):
            if h + 1 < N_HEADS:
                s[h + 1] = scores(h + 1)
            sh = jnp.where(kpos <= t_c, s[h], NEG_INF) if causal else s[h]
            m_scr[h, :, col0:], p[h], alpha[h] = _online_softmax(sh, m_scr[h, :, col0:])
            if h >= 1:
                update(h - 1)
        update(N_HEADS - 1)

    m_scr[...] = jnp.full(m_scr.shape, NEG_INF, F32)
    acc_scr[...] = jnp.zeros(acc_scr.shape, F32)

    def full_chunk(c, carry):
        step(c * tk, tk, 0, False)
        return carry

    lax.fori_loop(0, q0 // tk, full_chunk, 0)
    step(q0, tk // 2, 0, True)
    step(q0 + tk // 2, tk // 2, cols // 2, True)

    ng = ng_ref[...]
    outs = []
    for h in range(N_HEADS):
        acc = acc_scr[h]
        o_all = acc[:HEAD_DIM] * (1.0 / acc[HEAD_DIM:HEAD_DIM + 1])
        map0 = jnp.concatenate([o_all[:, 0:hq], o_all[:, 2 * hq:3 * hq]], axis=1)
        map1 = jnp.concatenate([o_all[:, hq:2 * hq], o_all[:, 3 * hq:4 * hq]], axis=1)
        o = map0 - lam_full * map1
        ms = jnp.mean(o * o, axis=0, keepdims=True)
        outs.append(o * lax.rsqrt(ms + EPS) * ng * (1.0 - lambda_init))
    y = jnp.concatenate(
        [jnp.concatenate(outs[0:2], axis=0).T, jnp.concatenate(outs[2:4], axis=0).T], axis=1)
    o_ref[0] = (y * _silu(z_ref[0].astype(F32))).astype(o_ref.dtype)


def _diff_attention(proj, lam, norm_g, layer_idx, *, tq=512):
    bsz, seq, _ = proj.shape
    tk = tq
    lambda_init = 0.8 - 0.6 * math.exp(-0.3 * layer_idx)
    ng = jnp.broadcast_to(norm_g.reshape(HEAD_DIM, 1), (HEAD_DIM, tq)).astype(F32)
    kern = functools.partial(_diff_kernel, seq=seq, tq=tq, tk=tk, lambda_init=lambda_init)
    return pl.pallas_call(
        kern,
        grid=(bsz, seq // tq),
        in_specs=[pl.BlockSpec((1, tq, D_GROUP), lambda b, i: (b, i, PB_B_Q // 2)),
                  pl.BlockSpec((1, seq, D_GROUP), lambda b, i: (b, 0, PB_B_K // 2)),
                  pl.BlockSpec((1, seq, D_GROUP), lambda b, i: (b, 0, PB_B_V // 2)),
                  pl.BlockSpec((1, tq, D_GROUP), lambda b, i: (b, i, PB_B_Z // 2)),
                  pl.BlockSpec((4, DIFF_QK_DIM), lambda b, i: (0, 0)),
                  pl.BlockSpec((HEAD_DIM, tq), lambda b, i: (0, 0))],
        out_specs=pl.BlockSpec((1, tq, D_GROUP), lambda b, i: (b, i, 0)),
        out_shape=jax.ShapeDtypeStruct((bsz, seq, D_GROUP), BF16),
        scratch_shapes=[pltpu.VMEM((N_HEADS * (HEAD_DIM + SUM_ROWS), seq), BF16),
                        pltpu.VMEM((N_HEADS, 1, 2 * tq), F32),
                        pltpu.VMEM((N_HEADS, HEAD_DIM + SUM_ROWS, 2 * tq), F32)],
        compiler_params=_cparams(("arbitrary", "arbitrary")),
        name="diff_attention",
    )(proj, proj, proj, proj, lam, ng)


def _causal_conv_silu(x_raw, xext, cw_ref, cb_ref):
    n = x_raw.shape[0]
    xext[CONV_PAD:CONV_PAD + n, :] = x_raw
    cw = cw_ref[...]
    xe = xext[...]
    acc = cb_ref[...] + cw[CONV_WIDTH - 1:CONV_WIDTH] * x_raw
    for k in range(CONV_WIDTH - 1):
        shifted = pltpu.roll(xe, CONV_WIDTH - 1 - k, 0)[CONV_PAD:]
        acc = acc + cw[k:k + 1] * shifted
    xext[0:CONV_PAD, :] = x_raw[n - CONV_PAD:, :]
    return _silu(acc)


def _interleave(chunks):
    live = list(chunks)
    while live:
        nxt = []
        for g in live:
            try:
                next(g)
                nxt.append(g)
            except StopIteration:
                pass
        live = nxt


def _rows_to_lanes(rows, first):
    n = rows.shape[1]
    parts = [rows, jnp.zeros((LANES - first - rows.shape[0], n), F32)]
    if first:
        parts.insert(0, jnp.zeros((first, n), F32))
    return jnp.concatenate(parts, axis=0).T


def _tri_upper(n):
    return jnp.where(_iota((n, n), 0) <= _iota((n, n), 1), 1.0, 0.0).astype(BF16)


def _expand_mat(first_lane, width):
    r = _iota((LANES, width), 0)
    c = _iota((LANES, width), 1)
    return jnp.where(r - first_lane == c // HEAD_DIM, 1.0, 0.0).astype(BF16)


def _group_sum_mat(width, group):
    r = _iota((width, width), 0)
    c = _iota((width, width), 1)
    return jnp.where(r // group == c // group, 1.0, 0.0).astype(BF16)


def _ssd_kernel(z_ref, xbc_ref, sm_ref, cw_ref, cb_ref, dtb_ref, alogc_ref, dsk_ref, ng_ref, o_ref,
                xext, st, *, nb):
    @pl.when(pl.program_id(1) == 0)
    def _():
        st[...] = jnp.zeros(st.shape, F32)
        xext[:, 0:CONV_PAD, :] = jnp.zeros((nb, CONV_PAD, SSM_XBC), F32)

    _interleave(_ssd_chunk(z_ref.at[bb], xbc_ref.at[bb], sm_ref.at[bb], cw_ref, cb_ref, dtb_ref, alogc_ref,
                           dsk_ref, ng_ref, o_ref.at[bb], xext.at[bb], st.at[bb]) for bb in range(nb))


def _ssd_chunk(z_ref, xbc_ref, sm_ref, cw_ref, cb_ref, dtb_ref, alogc_ref, dsk_ref, ng_ref, o_ref, xext, st):
    n = CHUNK
    xc = _causal_conv_silu(xbc_ref[...].astype(F32), xext, cw_ref, cb_ref)
    xs = xc[:, 0:D_GROUP]
    bm = xc[:, D_GROUP:D_GROUP + LANES]
    cm = xc[:, D_GROUP + LANES:D_GROUP + 2 * LANES]
    yield

    dt_c = _softplus(sm_ref[...] + dtb_ref[...])
    da_t = (dt_c * -jnp.exp(alogc_ref[...])).T[SM_DT - N_HEADS:SM_DT + N_HEADS]
    acs_t = _dot_lhs_f32(da_t, _tri_upper(n), terms=2)
    acs_rows = acs_t[N_HEADS:]
    dtx = _dot_lhs_f32(dt_c, _expand_mat(SM_DT, D_GROUP), terms=2)
    acs = _dot_lhs_f32(_rows_to_lanes(acs_t, SM_DT - N_HEADS), _expand_mat(SM_DT, D_GROUP), terms=2)
    acs_last = acs[n - 1:n, :]
    yield

    xdt = xs * dtx
    xdt_b = xdt.astype(BF16)
    bm_b = bm.astype(BF16)
    cm_b = cm.astype(BF16)
    lane = _iota((n, LANES), 1)
    causal = _iota((n, n), 0) >= _iota((n, n), 1)

    y_pairs = []
    for g in range(SSM_GROUPS):
        in_group = (lane // SSM_STATE) == g
        gmat = _nt(jnp.where(in_group, cm, 0.0).astype(BF16), bm_b)
        pair = []
        for h in (2 * g, 2 * g + 1):
            col = acs[:, HEAD_DIM * h:HEAD_DIM * h + 1]
            row = acs_rows[h:h + 1, :]
            decay = jnp.exp(jnp.where(causal, col - row, NEG_INF))
            pair.append(_dot((gmat * decay).astype(BF16), xdt_b[:, g * LANES:(g + 1) * LANES]))
        y_pairs.append(jnp.where(lane < HEAD_DIM, pair[0], pair[1]))
        yield
    y_diag = jnp.concatenate(y_pairs, axis=1)

    state = st[...]
    y_off = jnp.exp(acs) * _dot(cm_b, state.astype(BF16))
    decay_end = jnp.exp(acs_last - acs)
    upd = _dot(bm.T.astype(BF16), (decay_end * xdt).astype(BF16))
    own = (_iota(st.shape, 0) // SSM_STATE) == (_iota(st.shape, 1) // LANES)
    st[...] = jnp.where(own, jnp.exp(acs_last) * state + upd, 0.0)
    yield

    y = (y_diag + y_off + dsk_ref[...] * xs) * _silu(z_ref[...].astype(F32))
    ms = _dot_lhs_f32(y * y, _group_sum_mat(D_GROUP, LANES), terms=2) * (1.0 / LANES)
    o_ref[...] = (y * lax.rsqrt(ms + EPS) * ng_ref[...]).astype(o_ref.dtype)


def _ssd_mixer(proj, small, conv_w, conv_b, dt_bias, a_log, d_skip, norm_g):
    bsz, seq, _ = proj.shape
    n = CHUNK
    dtb = jnp.zeros((1, LANES), F32).at[0, SM_DT:SM_DT + N_HEADS].set(dt_bias)
    alog_c = jnp.zeros((1, LANES), F32).at[0, SM_DT:SM_DT + N_HEADS].set(a_log)
    dsk_x = jnp.repeat(d_skip, HEAD_DIM).reshape(1, D_GROUP)
    full = lambda b, c: (0, 0)
    nb = _seqs_per_step(bsz, 8)
    return pl.pallas_call(
        functools.partial(_ssd_kernel, nb=nb),
        grid=(bsz // nb, seq // n),
        in_specs=[pl.BlockSpec((nb, n, D_GROUP), lambda b, c: (b, c, PB_C_Z // 2)),
                  pl.BlockSpec((nb, n, SSM_XBC), lambda b, c: (b, c, PB_C_XBC * LANES // SSM_XBC)),
                  pl.BlockSpec((nb, n, LANES), lambda b, c: (b, c, 0)),
                  pl.BlockSpec((CONV_WIDTH, SSM_XBC), full),
                  pl.BlockSpec((1, SSM_XBC), full),
                  pl.BlockSpec((1, LANES), full),
                  pl.BlockSpec((1, LANES), full),
                  pl.BlockSpec((1, D_GROUP), full),
                  pl.BlockSpec((1, D_GROUP), full)],
        out_specs=pl.BlockSpec((nb, n, D_GROUP), lambda b, c: (b, c, 0)),
        out_shape=jax.ShapeDtypeStruct((bsz, seq, D_GROUP), BF16),
        scratch_shapes=[pltpu.VMEM((nb, CONV_PAD + n, SSM_XBC), F32), pltpu.VMEM((nb, LANES, D_GROUP), F32)],
        compiler_params=_cparams(("arbitrary", "arbitrary")),
        name="ssd_mixer",
    )(proj, proj, small, conv_w, conv_b.reshape(1, SSM_XBC), dtb, alog_c, dsk_x, norm_g.reshape(1, D_GROUP))


def _mlstm_kernel(qk_ref, v_ref, sm_ref, og_ref, z_ref, cw_ref, cb_ref, ifb_ref, ng_ref, o_ref,
                  xext, c_st, n_st, m_st, *, nb):
    @pl.when(pl.program_id(1) == 0)
    def _():
        c_st[...] = jnp.zeros(c_st.shape, F32)
        n_st[...] = jnp.zeros(n_st.shape, F32)
        m_st[...] = jnp.zeros(m_st.shape, F32)
        xext[:, 0:CONV_PAD, :] = jnp.zeros((nb, CONV_PAD, 2 * D_GROUP), F32)

    _interleave(_mlstm_chunk(qk_ref.at[bb], v_ref.at[bb], sm_ref.at[bb], og_ref.at[bb], z_ref.at[bb], cw_ref,
                             cb_ref, ifb_ref, ng_ref, o_ref.at[bb], xext.at[bb], c_st.at[bb], n_st.at[bb],
                             m_st.at[bb]) for bb in range(nb))


def _mlstm_chunk(qk_ref, v_ref, sm_ref, og_ref, z_ref, cw_ref, cb_ref, ifb_ref, ng_ref, o_ref,
                 xext, c_st, n_st, m_st):
    n = CHUNK
    qk = _causal_conv_silu(qk_ref[...].astype(F32), xext, cw_ref, cb_ref)
    q = qk[:, 0:D_GROUP]
    k = qk[:, D_GROUP:] * (HEAD_DIM ** -0.5)
    q_b = q.astype(BF16)
    k_b = k.astype(BF16)
    v_b = v_ref[...]
    yield

    pre = sm_ref[...] + ifb_ref[...]
    logf = -_softplus(-pre)
    ig = _dot_lhs_f32(pre, _expand_mat(SM_I, D_GROUP), terms=2)
    gates_t = jnp.where(_iota((n, LANES), 1) >= SM_F, logf, pre).T[SM_I:SM_I + 2 * N_HEADS]
    cum_t = _dot_lhs_f32(gates_t, _tri_upper(n), terms=2)
    u_rows = gates_t[0:N_HEADS] - cum_t[N_HEADS:2 * N_HEADS]
    b = _dot_lhs_f32(_rows_to_lanes(cum_t, SM_I), _expand_mat(SM_F, D_GROUP), terms=2)
    b_last = b[n - 1:n, :]
    yield

    m_prev = m_st[...]
    c_prev = c_st[...]
    n_prev = n_st[...]

    lane = _iota((n, LANES), 1)
    causal = _iota((n, n), 0) >= _iota((n, n), 1)
    ones_b = jnp.ones((n, LANES), BF16)
    num_pairs, den_pairs, mt_pairs, wi_pairs = [], [], [], []
    for pr in range(N_HEADS // 2):
        lanes = slice(pr * LANES, (pr + 1) * LANES)
        qp = q[:, lanes]
        kp_b = k_b[:, lanes]
        rhs = jnp.concatenate([v_b[:, lanes], ones_b], axis=1)
        res, mts, wis = [], [], []
        for hh in range(2):
            h = 2 * pr + hh
            in_head = (lane // HEAD_DIM) == hh
            bcol = b[:, HEAD_DIM * h:HEAD_DIM * h + 1]
            dlog = jnp.where(causal, bcol + u_rows[h:h + 1, :], NEG_INF)
            inter = bcol + m_prev[:, HEAD_DIM * h:HEAD_DIM * h + 1]
            m_t = jnp.maximum(inter, jnp.max(dlog, axis=1, keepdims=True))
            s_qk = _nt(jnp.where(in_head, qp, 0.0).astype(BF16), kp_b) * jnp.exp(dlog - m_t)
            res.append(_dot(s_qk.astype(BF16), rhs))
            mts.append(jnp.broadcast_to(m_t, (n, LANES)))
            wis.append(jnp.broadcast_to(jnp.exp(inter - m_t), (n, LANES)))
            yield
        first = lane < HEAD_DIM
        num_pairs.append(jnp.where(first, res[0][:, :LANES], res[1][:, :LANES]))
        den_pairs.append(jnp.where(first, res[0][:, LANES:], res[1][:, LANES:]))
        mt_pairs.append(jnp.where(first, mts[0], mts[1]))
        wi_pairs.append(jnp.where(first, wis[0], wis[1]))
    num_intra = jnp.concatenate(num_pairs, axis=1)
    den_intra = jnp.concatenate(den_pairs, axis=1)
    m_t = jnp.concatenate(mt_pairs, axis=1)
    w_inter = jnp.concatenate(wi_pairs, axis=1)

    head_sum = _group_sum_mat(D_GROUP, HEAD_DIM)
    num = num_intra + w_inter * _dot(q_b, c_prev.astype(BF16))
    den = den_intra + w_inter * _dot_lhs_f32(q * n_prev, head_sum, terms=2)
    hid = num / jnp.maximum(jnp.abs(den), jnp.exp(-m_t))
    yield

    g_end = b_last - b + ig
    m_loc = jnp.max(g_end, axis=0, keepdims=True)
    m_new = jnp.maximum(b_last + m_prev, m_loc)
    a_prev = jnp.exp(b_last + m_prev - m_new)
    a_loc = jnp.exp(m_loc - m_new)
    kw = k * (jnp.exp(g_end - m_loc) * a_loc)
    kw_t = jnp.concatenate([kw[:, :LANES].T, kw[:, LANES:].T], axis=0)
    upd = _dot(kw_t.astype(BF16), v_b)
    own = (_iota(c_st.shape, 0) // HEAD_DIM) == (_iota(c_st.shape, 1) // HEAD_DIM)
    c_st[...] = jnp.where(own, a_prev * c_prev + upd, 0.0)
    n_st[...] = a_prev * n_prev + jnp.sum(kw, axis=0, keepdims=True)
    m_st[...] = m_new
    yield

    hm = _sigmoid(og_ref[...].astype(F32)) * hid
    ms = _dot_lhs_f32(hm * hm, head_sum, terms=2) * (1.0 / HEAD_DIM)
    o_ref[...] = (hm * lax.rsqrt(ms + EPS) * ng_ref[...] * _silu(z_ref[...].astype(F32))).astype(o_ref.dtype)


def _mlstm_mixer(proj, small, conv_w, conv_b, if_b, norm_g):
    bsz, seq, _ = proj.shape
    n = CHUNK
    ifb = jnp.zeros((1, LANES), F32).at[0, SM_I:SM_I + 2 * N_HEADS].set(if_b)
    ng = jnp.tile(norm_g, N_HEADS).reshape(1, D_GROUP)
    full = lambda b, c: (0, 0)
    nb = _seqs_per_step(bsz, 4)
    return pl.pallas_call(
        functools.partial(_mlstm_kernel, nb=nb),
        grid=(bsz // nb, seq // n),
        in_specs=[pl.BlockSpec((nb, n, 2 * D_GROUP), lambda b, c: (b, c, PB_D_QK * LANES // (2 * D_GROUP))),
                  pl.BlockSpec((nb, n, D_GROUP), lambda b, c: (b, c, PB_D_V // 2)),
                  pl.BlockSpec((nb, n, LANES), lambda b, c: (b, c, 0)),
                  pl.BlockSpec((nb, n, D_GROUP), lambda b, c: (b, c, PB_D_O // 2)),
                  pl.BlockSpec((nb, n, D_GROUP), lambda b, c: (b, c, PB_D_Z // 2)),
                  pl.BlockSpec((CONV_WIDTH, 2 * D_GROUP), full),
                  pl.BlockSpec((1, 2 * D_GROUP), full),
                  pl.BlockSpec((1, LANES), full),
                  pl.BlockSpec((1, D_GROUP), full)],
        out_specs=pl.BlockSpec((nb, n, D_GROUP), lambda b, c: (b, c, 0)),
        out_shape=jax.ShapeDtypeStruct((bsz, seq, D_GROUP), BF16),
        scratch_shapes=[pltpu.VMEM((nb, CONV_PAD + n, 2 * D_GROUP), F32),
                        pltpu.VMEM((nb, D_GROUP, D_GROUP), F32),
                        pltpu.VMEM((nb, 1, D_GROUP), F32),
                        pltpu.VMEM((nb, 1, D_GROUP), F32)],
        compiler_params=_cparams(("arbitrary", "arbitrary")),
        name="mlstm_mixer",
    )(proj, proj, small, proj, proj, conv_w, conv_b.reshape(1, 2 * D_GROUP), ifb, ng)


def _outproj_kernel(ya_ref, yb_ref, yc_ref, yd_ref, x_ref, gate_ref, w_ref, fg_ref, o_ref, *, final):
    acc = _dot(ya_ref[0], w_ref[0:D_GROUP, :])
    acc = acc + _dot(yb_ref[0], w_ref[D_GROUP:2 * D_GROUP, :])
    acc = acc + _dot(yc_ref[0], w_ref[2 * D_GROUP:3 * D_GROUP, :])
    acc = acc + _dot(yd_ref[0], w_ref[3 * D_GROUP:4 * D_GROUP, :])
    out = x_ref[0] + gate_ref[0] * acc
    if final:
        ms = jnp.mean(out * out, axis=-1, keepdims=True)
        out = out * lax.rsqrt(ms + EPS) * fg_ref[...]
    o_ref[0] = out


def _out_projection(ys, x, gate, w_bf, final_g, final):
    bsz, seq, d = x.shape
    tm = min(1024, seq)
    yspec = pl.BlockSpec((1, tm, D_GROUP), lambda b, i: (b, i, 0))
    return pl.pallas_call(
        functools.partial(_outproj_kernel, final=final),
        grid=(bsz, seq // tm),
        in_specs=[yspec, yspec, yspec, yspec,
                  pl.BlockSpec((1, tm, d), lambda b, i: (b, i, 0)),
                  pl.BlockSpec((1, 1, d), lambda b, i: (b, 0, 0)),
                  pl.BlockSpec((N_HEADS * D_GROUP, d), lambda b, i: (0, 0)),
                  pl.BlockSpec((1, d), lambda b, i: (0, 0))],
        out_specs=pl.BlockSpec((1, tm, d), lambda b, i: (b, i, 0)),
        out_shape=jax.ShapeDtypeStruct((bsz, seq, d), F32),
        compiler_params=_cparams(("arbitrary", "arbitrary")),
        name="out_projection",
    )(*ys, x, gate, w_bf, final_g.reshape(1, d))


def _cmp_to_sel_t(seq):
    n_cmp = (seq - NSA_CMP_BLOCK) // NSA_CMP_STRIDE + 1
    n_sel = seq // NSA_SEL_BLOCK
    start = np.arange(n_cmp)[:, None] * NSA_CMP_STRIDE
    sel_start = np.arange(n_sel)[None, :] * NSA_SEL_BLOCK
    overlap = np.clip(np.minimum(start + NSA_CMP_BLOCK, sel_start + NSA_SEL_BLOCK)
                      - np.maximum(start, sel_start), 0, None)
    out = np.zeros((n_sel, seq // NSA_CMP_STRIDE), np.float32)
    out[:, :n_cmp] = (overlap / NSA_CMP_BLOCK).T
    return jnp.asarray(out, BF16)


def _mixers(proj, small, kcvc, layer_idx, p):
    seq = proj.shape[1]
    pos = p['nsa_cmp_pos'].reshape(1, NSA_CMP_BLOCK * HEAD_DIM)
    ckv, cvt = _nsa_compress(kcvc, pos, p['nsa_ck_w1'], p['nsa_ck_w2'], p['nsa_cv_w1'], p['nsa_cv_w2'])
    y_a = _nsa_attention(proj, small, ckv, cvt, _cmp_to_sel_t(seq), p['nsa_norm_g'])
    y_b = _diff_attention(proj, p['diff_lam'], p['diff_norm_g'], layer_idx)
    y_c = _ssd_mixer(proj, small, p['ssm_conv_w'], p['ssm_conv_b'], p['ssm_dt_bias'], p['ssm_a_log'],
                     p['ssm_d'], p['ssm_norm_g'])
    y_d = _mlstm_mixer(proj, small, p['ml_conv_w'], p['ml_conv_b'], p['ml_if_b'], p['ml_norm_g'])
    return y_a, y_b, y_c, y_d


_LAYER_PARAMS = ('nsa_cmp_pos', 'nsa_ck_w1', 'nsa_ck_w2', 'nsa_cv_w1', 'nsa_cv_w2', 'nsa_norm_g',
                 'diff_lam', 'diff_norm_g', 'ssm_conv_w', 'ssm_conv_b', 'ssm_dt_bias', 'ssm_a_log',
                 'ssm_d', 'ssm_norm_g', 'ml_conv_w', 'ml_conv_b', 'ml_if_b', 'ml_norm_g')


def kernel(x, c, norm_g, ada_w, ada_b, w_in, w_out, nsa_cmp_pos, nsa_ck_w1, nsa_ck_w2, nsa_cv_w1, nsa_cv_w2, nsa_norm_g, diff_lam, diff_norm_g, ssm_conv_w, ssm_conv_b, ssm_dt_bias, ssm_a_log, ssm_d, ssm_norm_g, ml_conv_w, ml_conv_b, ml_if_b, ml_norm_g, final_g):
    stacked = dict(nsa_cmp_pos=nsa_cmp_pos, nsa_ck_w1=nsa_ck_w1, nsa_ck_w2=nsa_ck_w2, nsa_cv_w1=nsa_cv_w1,
                   nsa_cv_w2=nsa_cv_w2, nsa_norm_g=nsa_norm_g, diff_lam=diff_lam, diff_norm_g=diff_norm_g,
                   ssm_conv_w=ssm_conv_w, ssm_conv_b=ssm_conv_b, ssm_dt_bias=ssm_dt_bias,
                   ssm_a_log=ssm_a_log, ssm_d=ssm_d, ssm_norm_g=ssm_norm_g, ml_conv_w=ml_conv_w,
                   ml_conv_b=ml_conv_b, ml_if_b=ml_if_b, ml_norm_g=ml_norm_g)
    depth = w_in.shape[0]
    bsz, seq, d = x.shape
    mod = _ada_modulation(c, ada_w, ada_b)
    for l in range(depth):
        p = {name: stacked[name][l] for name in _LAYER_PARAMS}
        shift = mod[l, :, 0:d].reshape(bsz, 1, d)
        scale = mod[l, :, d:2 * d].reshape(bsz, 1, d)
        gate = mod[l, :, 2 * d:3 * d].reshape(bsz, 1, d)
        w_bf = _relayout_w_in(w_in[l]).astype(BF16)
        proj, small, kcvc = _in_projection(x, norm_g[l], scale, shift, w_bf)
        ys = _mixers(proj, small, kcvc, l, p)
        x = _out_projection(ys, x, gate, w_out[l].astype(BF16), final_g, final=(l == depth - 1))
    return x
```

```python
import functools
import math

import numpy as np
import jax
import jax.numpy as jnp
from jax import lax
from jax.experimental import pallas as pl
from jax.experimental.pallas import tpu as pltpu

F32 = jnp.float32
BF16 = jnp.bfloat16

D_MODEL = 1024
N_HEADS = 4
HEAD_DIM = 64
D_GROUP = N_HEADS * HEAD_DIM
NEG_INF = -1e30
EPS = 1e-6

NSA_CMP_BLOCK = 32
NSA_CMP_STRIDE = 16
NSA_SEL_BLOCK = 64
NSA_TOP_N = 16
NSA_WINDOW = 512
NSA_CMP_HIDDEN = 128
NSA_FORCED_SCORE = 1e4

DIFF_QK_DIM = HEAD_DIM // 2
SSM_STATE = 64
SSM_GROUPS = 2
SSM_XBC = D_GROUP + 2 * SSM_GROUPS * SSM_STATE
CONV_WIDTH = 4
CHUNK = 128
CONV_PAD = 8

LANES = 128
D_PROJ = 4096

_SPLITS = (
    D_GROUP, HEAD_DIM, HEAD_DIM, HEAD_DIM, HEAD_DIM, HEAD_DIM, HEAD_DIM, 3 * N_HEADS, D_GROUP,
    D_GROUP, D_GROUP, D_GROUP, D_GROUP,
    D_GROUP, SSM_XBC, N_HEADS,
    2 * D_GROUP, D_GROUP, 2 * N_HEADS, D_GROUP, D_GROUP,
)
_OFFS = [0] + [int(o) for o in np.cumsum(_SPLITS)]
(_A_Q, _A_KC, _A_VC, _A_KS, _A_VS, _A_KW, _A_VW, _A_G, _A_Z,
 _B_Q, _B_K, _B_V, _B_Z, _C_Z, _C_XBC, _C_DT, _D_QK, _D_V, _D_IF, _D_O, _D_Z) = range(21)

SM_GATE = 0
SM_DT = 12
SM_I = 16
SM_F = 20

PB_A_Q = 0
PB_A_KCVC = 2
PB_A_KSVS = 3
PB_A_KWVW = 4
PB_SMALL = 5
PB_A_Z = 6
PB_B_Q = 8
PB_B_K = 10
PB_B_V = 12
PB_B_Z = 14
PB_C_XBC = 16
PB_C_Z = 20
PB_D_V = 22
PB_D_QK = 24
PB_D_O = 28
PB_D_Z = 30

VMEM_LIMIT = 56 * 1024 * 1024
LOG2E = 1.4426950408889634
SUM_ROWS = 16
WINDOW_SUB = 256


def _cparams(sem):
    return pltpu.CompilerParams(dimension_semantics=sem, vmem_limit_bytes=VMEM_LIMIT)


def _relayout_w_in(w_in):
    def col(i):
        return w_in[:, _OFFS[i]:_OFFS[i + 1]]
    d = w_in.shape[0]
    small = jnp.concatenate(
        [col(_A_G), col(_C_DT), col(_D_IF), jnp.zeros((d, LANES - 24), w_in.dtype)], axis=1)
    parts = [col(_A_Q), col(_A_KC), col(_A_VC), col(_A_KS), col(_A_VS), col(_A_KW), col(_A_VW),
             small, col(_A_Z),
             col(_B_Q), col(_B_K), col(_B_V), col(_B_Z),
             col(_C_XBC), col(_C_Z),
             col(_D_V), col(_D_QK), col(_D_O), col(_D_Z)]
    out = jnp.concatenate(parts, axis=1)
    assert out.shape[1] == D_PROJ
    return out


def _dot(a, b):
    return jnp.dot(a, b, preferred_element_type=F32)


def _nt(a, b):
    return lax.dot_general(a, b, (((1,), (1,)), ((), ())), preferred_element_type=F32)


def _split3(a):
    hi = a.astype(BF16)
    r1 = a - hi.astype(F32)
    mid = r1.astype(BF16)
    lo = (r1 - mid.astype(F32)).astype(BF16)
    return hi, mid, lo


def _dot_lhs_f32(a, b_exact, terms=3):
    hi, mid, lo = _split3(a)
    out = _dot(hi, b_exact) + _dot(mid, b_exact)
    return out + _dot(lo, b_exact) if terms == 3 else out


def _seqs_per_step(bsz, most):
    return math.gcd(bsz, most)


def _sigmoid(x):
    return 1.0 / (1.0 + jnp.exp(-x))


def _silu(x):
    return x * _sigmoid(x)


def _softplus(x):
    return jnp.maximum(x, 0.0) + jnp.log(1.0 + jnp.exp(-jnp.abs(x)))


def _iota(shape, dim):
    return lax.broadcasted_iota(jnp.int32, shape, dim)


def _ada_kernel(c_ref, w_ref, b_ref, o_ref):
    ca = _silu(c_ref[...])
    w = w_ref[0]
    c_hi, c_mid, _ = _split3(ca)
    w_hi, w_mid, _ = _split3(w)
    acc = _dot(c_hi, w_hi) + _dot(c_hi, w_mid) + _dot(c_mid, w_hi)
    o_ref[0] = acc + b_ref[0]


def _ada_modulation(c, ada_w, ada_b):
    depth, d, d3 = ada_w.shape
    bsz = c.shape[0]
    nb = d3 // d
    return pl.pallas_call(
        _ada_kernel,
        grid=(depth, nb),
        in_specs=[pl.BlockSpec((bsz, d), lambda l, j: (0, 0)),
                  pl.BlockSpec((1, d, d), lambda l, j: (l, 0, j)),
                  pl.BlockSpec((1, 1, d), lambda l, j: (l, 0, j))],
        out_specs=pl.BlockSpec((1, bsz, d), lambda l, j: (l, 0, j)),
        out_shape=jax.ShapeDtypeStruct((depth, bsz, d3), F32),
        compiler_params=_cparams(("arbitrary", "arbitrary")),
        name="ada_modulation",
    )(c, ada_w, ada_b.reshape(depth, 1, d3))


def _inproj_kernel(x_ref, g_ref, sc_ref, sh_ref, w_ref, o_ref, small_ref, kcvc_ref, *, tn):
    x = x_ref[0]
    ms = jnp.mean(x * x, axis=-1, keepdims=True)
    y = x * lax.rsqrt(ms + EPS) * g_ref[...]
    h = (y * (1.0 + sc_ref[0]) + sh_ref[0]).astype(BF16)
    for j in range(D_PROJ // tn):
        acc = _dot(h, w_ref[:, j * tn:(j + 1) * tn])
        o_ref[0, :, j * tn:(j + 1) * tn] = acc.astype(o_ref.dtype)
        if j == (PB_SMALL * LANES) // tn:
            off = PB_SMALL * LANES - j * tn
            small_ref[0] = acc[:, off:off + LANES]
        if j == (PB_A_KCVC * LANES) // tn:
            off = PB_A_KCVC * LANES - j * tn
            kcvc_ref[0] = acc[:, off:off + LANES]


def _in_projection(x, norm_g, scale, shift, w_bf):
    bsz, seq, d = x.shape
    tm = min(1024, seq)
    tn = 1024
    return pl.pallas_call(
        functools.partial(_inproj_kernel, tn=tn),
        grid=(bsz, seq // tm),
        in_specs=[pl.BlockSpec((1, tm, d), lambda b, i: (b, i, 0)),
                  pl.BlockSpec((1, d), lambda b, i: (0, 0)),
                  pl.BlockSpec((1, 1, d), lambda b, i: (b, 0, 0)),
                  pl.BlockSpec((1, 1, d), lambda b, i: (b, 0, 0)),
                  pl.BlockSpec((d, D_PROJ), lambda b, i: (0, 0))],
        out_specs=[pl.BlockSpec((1, tm, D_PROJ), lambda b, i: (b, i, 0)),
                   pl.BlockSpec((1, tm, LANES), lambda b, i: (b, i, 0)),
                   pl.BlockSpec((1, tm, LANES), lambda b, i: (b, i, 0))],
        out_shape=[jax.ShapeDtypeStruct((bsz, seq, D_PROJ), BF16),
                   jax.ShapeDtypeStruct((bsz, seq, LANES), F32),
                   jax.ShapeDtypeStruct((bsz, seq, LANES), F32)],
        compiler_params=_cparams(("arbitrary", "arbitrary")),
        name="in_projection",
    )(x, norm_g.reshape(1, d), scale, shift, w_bf)


def _compress_kernel(x_ref, pos_ref, wall_ref, kw1_ref, kw2_ref, vw1_ref, vw2_ref, kv_ref, vt_ref):
    n = x_ref.shape[1] // NSA_CMP_STRIDE
    acc = jnp.zeros((n, 4 * NSA_CMP_HIDDEN), F32)
    for r in range(NSA_CMP_STRIDE):
        x_r = x_ref[0, pl.ds(r, n, stride=NSA_CMP_STRIDE), :].astype(BF16)
        acc = acc + _dot(x_r, wall_ref[r])
    pos = jnp.broadcast_to(pos_ref[...], (8, NSA_CMP_BLOCK * HEAD_DIM)).astype(BF16)

    def one(first, second, w1_ref, w2_ref):
        second = pltpu.roll(second, n - 1, 0)
        bias = _dot(pos, w1_ref[...].astype(BF16))[0:1]
        hid = _silu(first + second + bias)
        return _dot(hid.astype(BF16), w2_ref[...].astype(BF16))

    hw = NSA_CMP_HIDDEN
    kc = one(acc[:, 0:hw], acc[:, hw:2 * hw], kw1_ref, kw2_ref)
    vc = one(acc[:, 2 * hw:3 * hw], acc[:, 3 * hw:4 * hw], vw1_ref, vw2_ref)
    kv = jnp.concatenate([kc, vc], axis=1)
    kv_ref[0] = kv.astype(BF16)
    ones = jnp.ones((SUM_ROWS, kv.shape[0]), BF16)
    vt_ref[0] = jnp.concatenate([kv.T[HEAD_DIM:].astype(BF16), ones], axis=0)


def _compress_slot_weights(kw1, vw1):
    k1 = kw1.reshape(2, NSA_CMP_STRIDE, HEAD_DIM, NSA_CMP_HIDDEN)
    v1 = vw1.reshape(2, NSA_CMP_STRIDE, HEAD_DIM, NSA_CMP_HIDDEN)
    zero = jnp.zeros_like(k1[0])
    top = jnp.concatenate([k1[0], k1[1], zero, zero], axis=-1)
    bottom = jnp.concatenate([zero, zero, v1[0], v1[1]], axis=-1)
    return jnp.concatenate([top, bottom], axis=1).astype(BF16)


def _nsa_compress(kcvc, pos, kw1, kw2, vw1, vw2):
    bsz, seq, _ = kcvc.shape
    n = seq // NSA_CMP_STRIDE
    wall = _compress_slot_weights(kw1, vw1)
    full2 = lambda b: (0, 0)
    return pl.pallas_call(
        _compress_kernel,
        grid=(bsz,),
        in_specs=[pl.BlockSpec((1, seq, LANES), lambda b: (b, 0, 0)),
                  pl.BlockSpec(pos.shape, full2),
                  pl.BlockSpec(wall.shape, lambda b: (0, 0, 0)),
                  pl.BlockSpec(kw1.shape, full2), pl.BlockSpec(kw2.shape, full2),
                  pl.BlockSpec(vw1.shape, full2), pl.BlockSpec(vw2.shape, full2)],
        out_specs=[pl.BlockSpec((1, n, LANES), lambda b: (b, 0, 0)),
                   pl.BlockSpec((1, HEAD_DIM + SUM_ROWS, n), lambda b: (b, 0, 0))],
        out_shape=[jax.ShapeDtypeStruct((bsz, n, LANES), BF16),
                   jax.ShapeDtypeStruct((bsz, HEAD_DIM + SUM_ROWS, n), BF16)],
        compiler_params=_cparams(("arbitrary",)),
        name="nsa_compress",
    )(kcvc, pos, wall, kw1, kw2, vw1, vw2)


def _online_softmax(s, m):
    m_new = jnp.maximum(m, jnp.max(s, axis=0, keepdims=True))
    return m_new, jnp.exp2(s - m_new).astype(BF16), jnp.exp2(m - m_new)


def _nsa_kernel(q_ref, ks_ref, kw_ref, sm_ref, z_ref, ckv_ref, cvt_ref, c2s_ref, ng_ref, o_ref,
                vst, vwt, sel_scr, imp_scr, ms_scr, as_scr, *, seq, tq, tk, top_n):
    i = pl.program_id(1)
    n_cmp = seq // NSA_CMP_STRIDE
    n_sel = seq // NSA_SEL_BLOCK
    sub = min(tq, WINDOW_SUB)
    span = NSA_WINDOW + sub

    @pl.when(i == 0)
    def _():
        ones = jnp.ones((SUM_ROWS, LANES), BF16)
        for blk in range(seq // LANES):
            rows = slice(blk * LANES, (blk + 1) * LANES)
            ks = ks_ref[0, rows, :].astype(F32)
            kw = kw_ref[0, rows, :].astype(F32)
            vst[:, rows] = jnp.concatenate([ks.T[HEAD_DIM:].astype(BF16), ones], axis=0)
            vwt[:, rows] = jnp.concatenate([kw.T[HEAD_DIM:].astype(BF16), ones], axis=0)

    q0 = pl.multiple_of(i * tq, tq)
    q = q_ref[0].astype(F32) * (HEAD_DIM ** -0.5 * LOG2E)
    low = _iota((tq, LANES), 1) < HEAD_DIM
    qh = []
    for pair in range(N_HEADS // 2):
        qp = q[:, pair * LANES:(pair + 1) * LANES]
        qh.append(jnp.where(low, qp, 0.0).astype(BF16))
        qh.append(jnp.where(low, pltpu.roll(qp, HEAD_DIM, 1), 0.0).astype(BF16))

    t_row = q0 + _iota((1, tq), 1)

    ckv = ckv_ref[0]
    cvt = cvt_ref[0]
    cmp_end = _iota((n_cmp, 1), 0) * NSA_CMP_STRIDE + (NSA_CMP_BLOCK - 1)
    valid_c = cmp_end <= t_row
    any_valid = t_row >= NSA_CMP_BLOCK - 1
    heads = range(N_HEADS)
    s_c = [jnp.where(valid_c, _nt(ckv, qh[h]), NEG_INF) for h in heads]
    m_c = [jnp.max(s_c[h], axis=0, keepdims=True) for h in heads]
    p_c = [jnp.exp2(s_c[h] - m_c[h]) for h in heads]
    pv_c = [_dot(cvt, p_c[h].astype(BF16)) for h in heads]
    inv_c = [jnp.where(any_valid, 1.0 / pv_c[h][HEAD_DIM:HEAD_DIM + 1], 0.0) for h in heads]
    o_c = [pv_c[h][:HEAD_DIM] * inv_c[h] for h in heads]
    p_sum = p_c[0] * inv_c[0]
    for h in range(1, N_HEADS):
        p_sum = p_sum + p_c[h] * inv_c[h]

    p_hi = p_sum.astype(BF16)
    p_lo = (p_sum - p_hi.astype(F32)).astype(BF16)
    c2s = c2s_ref[...]
    imp = _dot(c2s, p_hi) + _dot(c2s, p_lo)
    sid = _iota((n_sel, tq), 0)
    cur = t_row // NSA_SEL_BLOCK
    forced = (sid == cur) | (sid == 0)
    imp = jnp.where(forced, NSA_FORCED_SCORE, jnp.where(sid <= cur, imp, -1.0))
    imp_scr[...] = imp
    per_step = tq // NSA_SEL_BLOCK

    def rank_body(jj, rank):
        for u in range(per_step):
            j = jj * per_step + u
            vj = imp_scr[pl.ds(j, 1), :]
            ge = jnp.where(vj >= imp, 1.0, 0.0)
            gt = jnp.where(vj > imp, 1.0, 0.0)
            rank = rank + jnp.where(sid > j, ge, gt)
        return rank

    n_rank = jnp.where((i + 1) * per_step > top_n, i + 1, 0)
    rank = lax.fori_loop(0, n_rank, rank_body, jnp.zeros((n_sel, tq), F32))
    sel_scr[...] = jnp.where(rank < float(top_n), 1.0, 0.0)

    blocks_per_chunk = tk // NSA_SEL_BLOCK

    def sel_step(kstart, nkeys, blk_off, col0, causal):
        kstart = pl.multiple_of(kstart, nkeys)
        k_c = ks_ref[0, pl.ds(kstart, nkeys), :]
        vt_c = vst[:, pl.ds(kstart, nkeys)]
        sb = pl.multiple_of(kstart // NSA_SEL_BLOCK - blk_off, blocks_per_chunk)
        selc = sel_scr[pl.ds(sb, blocks_per_chunk), col0:]
        rows = [jnp.broadcast_to(selc[r:r + 1, :], (NSA_SEL_BLOCK, tq - col0))
                for r in range(blk_off, blk_off + nkeys // NSA_SEL_BLOCK)]
        valid = jnp.concatenate(rows, axis=0) > 0.5
        if causal:
            valid = valid & (kstart + _iota((nkeys, 1), 0) <= t_row[:, col0:])
        s, p, alpha = [None] * N_HEADS, [None] * N_HEADS, [None] * N_HEADS

        def scores(h):
            return _nt(k_c, qh[h][col0:])

        def update(h):
            as_scr[h, :, col0:] = alpha[h] * as_scr[h, :, col0:] + _dot(vt_c, p[h])

        s[0] = scores(0)
        for h in range(N_HEADS):
            if h + 1 < N_HEADS:
                s[h + 1] = scores(h + 1)
            ms_scr[h, :, col0:], p[h], alpha[h] = _online_softmax(jnp.where(valid, s[h], NEG_INF),
                                                                  ms_scr[h, :, col0:])
            if h >= 1:
                update(h - 1)
        update(N_HEADS - 1)

    ms_scr[...] = jnp.full(ms_scr.shape, NEG_INF, F32)
    as_scr[...] = jnp.zeros(as_scr.shape, F32)

    def full_chunk(c, carry):
        sel_step(c * tk, tk, 0, 0, False)
        return carry

    half_blocks = blocks_per_chunk // 2
    lax.fori_loop(0, q0 // tk, full_chunk, 0)
    sel_step(q0, tk // 2, 0, 0, True)
    sel_step(q0 + tk // 2, tk // 2, half_blocks, tq // 2, True)

    streams = []
    for a in range(tq // sub):
        qs = slice(a * sub, (a + 1) * sub)
        wstart = pl.multiple_of(jnp.maximum(q0 + a * sub - NSA_WINDOW, 0), sub)
        kpos_w = wstart + _iota((span, 1), 0)
        t_sub = t_row[:, qs]
        valid_w = (kpos_w <= t_sub) & (kpos_w > t_sub - NSA_WINDOW)
        streams += [(kw_ref[0, pl.ds(wstart, span), :], vwt[:, pl.ds(wstart, span)], valid_w, qh[h][qs])
                    for h in range(N_HEADS)]
    s_w = [jnp.where(valid, _nt(k_w, q_w), NEG_INF) for k_w, _, valid, q_w in streams]
    m_w = [jnp.max(sw, axis=0, keepdims=True) for sw in s_w]
    p_w = [jnp.exp2(sw - mw).astype(BF16) for sw, mw in zip(s_w, m_w)]
    pv_w = [_dot(st[1], pw) for st, pw in zip(streams, p_w)]
    o_w_parts = [pv[:HEAD_DIM] * (1.0 / pv[HEAD_DIM:HEAD_DIM + 1]) for pv in pv_w]
    o_w = [jnp.concatenate(o_w_parts[h::N_HEADS], axis=1) for h in range(N_HEADS)]

    gates = _sigmoid(sm_ref[0].T[0:16])
    ng = ng_ref[...]
    outs = []
    for h in range(N_HEADS):
        g_c = gates[3 * h:3 * h + 1]
        g_s = gates[3 * h + 1:3 * h + 2]
        g_w = gates[3 * h + 2:3 * h + 3]
        acc_s = as_scr[h]
        o_s = acc_s[:HEAD_DIM] * (1.0 / acc_s[HEAD_DIM:HEAD_DIM + 1])
        oh = g_c * o_c[h] + g_s * o_s + g_w * o_w[h]
        ms = jnp.mean(oh * oh, axis=0, keepdims=True)
        outs.append(oh * lax.rsqrt(ms + EPS) * ng)
    y = jnp.concatenate(
        [jnp.concatenate(outs[0:2], axis=0).T, jnp.concatenate(outs[2:4], axis=0).T], axis=1)
    o_ref[0] = (y * _silu(z_ref[0].astype(F32))).astype(o_ref.dtype)


def _nsa_attention(proj, small, ckv, cvt, c2s_t, norm_g, *, tq=512):
    bsz, seq, _ = proj.shape
    tk = tq
    assert tk // NSA_SEL_BLOCK == 8
    n_cmp = seq // NSA_CMP_STRIDE
    n_sel = seq // NSA_SEL_BLOCK
    top_n = min(NSA_TOP_N, n_sel)
    ng = jnp.broadcast_to(norm_g.reshape(HEAD_DIM, 1), (HEAD_DIM, tq)).astype(F32)
    kern = functools.partial(_nsa_kernel, seq=seq, tq=tq, tk=tk, top_n=top_n)
    return pl.pallas_call(
        kern,
        grid=(bsz, seq // tq),
        in_specs=[pl.BlockSpec((1, tq, 2 * LANES), lambda b, i: (b, i, PB_A_Q // 2)),
                  pl.BlockSpec((1, seq, LANES), lambda b, i: (b, 0, PB_A_KSVS)),
                  pl.BlockSpec((1, seq, LANES), lambda b, i: (b, 0, PB_A_KWVW)),
                  pl.BlockSpec((1, tq, LANES), lambda b, i: (b, i, 0)),
                  pl.BlockSpec((1, tq, 2 * LANES), lambda b, i: (b, i, PB_A_Z // 2)),
                  pl.BlockSpec((1, n_cmp, LANES), lambda b, i: (b, 0, 0)),
                  pl.BlockSpec((1, HEAD_DIM + SUM_ROWS, n_cmp), lambda b, i: (b, 0, 0)),
                  pl.BlockSpec((n_sel, n_cmp), lambda b, i: (0, 0)),
                  pl.BlockSpec((HEAD_DIM, tq), lambda b, i: (0, 0))],
        out_specs=pl.BlockSpec((1, tq, D_GROUP), lambda b, i: (b, i, 0)),
        out_shape=jax.ShapeDtypeStruct((bsz, seq, D_GROUP), BF16),
        scratch_shapes=[pltpu.VMEM((HEAD_DIM + SUM_ROWS, seq), BF16),
                        pltpu.VMEM((HEAD_DIM + SUM_ROWS, seq), BF16),
                        pltpu.VMEM((n_sel, tq), F32), pltpu.VMEM((n_sel, tq), F32),
                        pltpu.VMEM((N_HEADS, 1, tq), F32),
                        pltpu.VMEM((N_HEADS, HEAD_DIM + SUM_ROWS, tq), F32)],
        compiler_params=_cparams(("arbitrary", "arbitrary")),
        name="nsa_attention",
    )(proj, proj, proj, small, proj, ckv, cvt, c2s_t, ng)


def _diff_kernel(q_ref, k_ref, v_ref, z_ref, lam_ref, ng_ref, o_ref, vt, m_scr, acc_scr, *, seq, tq, tk,
                 lambda_init):
    i = pl.program_id(1)
    vrows = HEAD_DIM + SUM_ROWS
    cols = 2 * tq
    hq = tq // 2

    @pl.when(i == 0)
    def _():
        ones = jnp.ones((SUM_ROWS, LANES), BF16)
        for blk in range(seq // LANES):
            rows = slice(blk * LANES, (blk + 1) * LANES)
            parts = []
            for pair in range(N_HEADS // 2):
                v_t = v_ref[0, rows, pair * LANES:(pair + 1) * LANES].astype(F32).T.astype(BF16)
                parts += [v_t[:HEAD_DIM], ones, v_t[HEAD_DIM:], ones]
            vt[:, rows] = jnp.concatenate(parts, axis=0)

    lam = lam_ref[...]
    lam_full = (jnp.exp(jnp.sum(lam[0:1] * lam[1:2], axis=1, keepdims=True))
                - jnp.exp(jnp.sum(lam[2:3] * lam[3:4], axis=1, keepdims=True)) + lambda_init)

    q0 = pl.multiple_of(i * tq, tq)
    q = q_ref[0].astype(F32) * (DIFF_QK_DIM ** -0.5 * LOG2E)
    part = _iota((tq, LANES), 1) // DIFF_QK_DIM
    q2 = []
    for h in range(N_HEADS):
        qp = q[:, (h // 2) * LANES:(h // 2 + 1) * LANES]
        m0, m1 = [jnp.where(part == 2 * (h % 2) + c, qp, 0.0).astype(BF16) for c in range(2)]
        q2.append(jnp.concatenate([m0[:hq], m1[:hq], m0[hq:], m1[hq:]], axis=0))
    t_half = q0 + _iota((1, hq), 1)
    t_cols = jnp.concatenate([t_half, t_half, t_half + hq, t_half + hq], axis=1)

    def step(kstart, nkeys, col0, causal):
        kstart = pl.multiple_of(kstart, nkeys)
        kpos = kstart + _iota((nkeys, 1), 0)
        t_c = t_cols[:, col0:]
        s, p, alpha = [None] * N_HEADS, [None] * N_HEADS, [None] * N_HEADS

        def scores(h):
            lanes = slice((h // 2) * LANES, (h // 2 + 1) * LANES)
            return _nt(k_ref[0, pl.ds(kstart, nkeys), lanes], q2[h][col0:])

        def update(h):
            pv = _dot(vt[h * vrows:(h + 1) * vrows, pl.ds(kstart, nkeys)], p[h])
            acc_scr[h, :, col0:] = alpha[h] * acc_scr[h, :, col0:] + pv

        s[0] = scores(0)
        for h in range(N_HEADS):
            if h + 1 < N_HEADS:
                s[h + 1] = scores(h + 1)
            sh = jnp.where(kpos <= t_c, s[h], NEG_INF) if causal else s[h]
            m_scr[h, :, col0:], p[h], alpha[h] = _online_softmax(sh, m_scr[h, :, col0:])
            if h >= 1:
                update(h - 1)
        update(N_HEADS - 1)

    m_scr[...] = jnp.full(m_scr.shape, NEG_INF, F32)
    acc_scr[...] = jnp.zeros(acc_scr.shape, F32)

    def full_chunk(c, carry):
        step(c * tk, tk, 0, False)
        return carry

    lax.fori_loop(0, q0 // tk, full_chunk, 0)
    step(q0, tk // 2, 0, True)
    step(q0 + tk // 2, tk // 2, cols // 2, True)

    ng = ng_ref[...]
    outs = []
    for h in range(N_HEADS):
        acc = acc_scr[h]
        o_all = acc[:HEAD_DIM] * (1.0 / acc[HEAD_DIM:HEAD_DIM + 1])
        map0 = jnp.concatenate([o_all[:, 0:hq], o_all[:, 2 * hq:3 * hq]], axis=1)
        map1 = jnp.concatenate([o_all[:, hq:2 * hq], o_all[:, 3 * hq:4 * hq]], axis=1)
        o = map0 - lam_full * map1
        ms = jnp.mean(o * o, axis=0, keepdims=True)
        outs.append(o * lax.rsqrt(ms + EPS) * ng * (1.0 - lambda_init))
    y = jnp.concatenate(
        [jnp.concatenate(outs[0:2], axis=0).T, jnp.concatenate(outs[2:4], axis=0).T], axis=1)
    o_ref[0] = (y * _silu(z_ref[0].astype(F32))).astype(o_ref.dtype)


def _diff_attention(proj, lam, norm_g, layer_idx, *, tq=512):
    bsz, seq, _ = proj.shape
    tk = tq
    lambda_init = 0.8 - 0.6 * math.exp(-0.3 * layer_idx)
    ng = jnp.broadcast_to(norm_g.reshape(HEAD_DIM, 1), (HEAD_DIM, tq)).astype(F32)
    kern = functools.partial(_diff_kernel, seq=seq, tq=tq, tk=tk, lambda_init=lambda_init)
    return pl.pallas_call(
        kern,
        grid=(bsz, seq // tq),
        in_specs=[pl.BlockSpec((1, tq, D_GROUP), lambda b, i: (b, i, PB_B_Q // 2)),
                  pl.BlockSpec((1, seq, D_GROUP), lambda b, i: (b, 0, PB_B_K // 2)),
                  pl.BlockSpec((1, seq, D_GROUP), lambda b, i: (b, 0, PB_B_V // 2)),
                  pl.BlockSpec((1, tq, D_GROUP), lambda b, i: (b, i, PB_B_Z // 2)),
                  pl.BlockSpec((4, DIFF_QK_DIM), lambda b, i: (0, 0)),
                  pl.BlockSpec((HEAD_DIM, tq), lambda b, i: (0, 0))],
        out_specs=pl.BlockSpec((1, tq, D_GROUP), lambda b, i: (b, i, 0)),
        out_shape=jax.ShapeDtypeStruct((bsz, seq, D_GROUP), BF16),
        scratch_shapes=[pltpu.VMEM((N_HEADS * (HEAD_DIM + SUM_ROWS), seq), BF16),
                        pltpu.VMEM((N_HEADS, 1, 2 * tq), F32),
                        pltpu.VMEM((N_HEADS, HEAD_DIM + SUM_ROWS, 2 * tq), F32)],
        compiler_params=_cparams(("arbitrary", "arbitrary")),
        name="diff_attention",
    )(proj, proj, proj, proj, lam, ng)


def _causal_conv_silu(x_raw, xext, cw_ref, cb_ref):
    n = x_raw.shape[0]
    xext[CONV_PAD:CONV_PAD + n, :] = x_raw
    cw = cw_ref[...]
    xe = xext[...]
    acc = cb_ref[...] + cw[CONV_WIDTH - 1:CONV_WIDTH] * x_raw
    for k in range(CONV_WIDTH - 1):
        shifted = pltpu.roll(xe, CONV_WIDTH - 1 - k, 0)[CONV_PAD:]
        acc = acc + cw[k:k + 1] * shifted
    xext[0:CONV_PAD, :] = x_raw[n - CONV_PAD:, :]
    return _silu(acc)


def _interleave(chunks):
    live = list(chunks)
    while live:
        nxt = []
        for g in live:
            try:
                next(g)
                nxt.append(g)
            except StopIteration:
                pass
        live = nxt


def _rows_to_lanes(rows, first):
    n = rows.shape[1]
    parts = [rows, jnp.zeros((LANES - first - rows.shape[0], n), F32)]
    if first:
        parts.insert(0, jnp.zeros((first, n), F32))
    return jnp.concatenate(parts, axis=0).T


def _tri_upper(n):
    return jnp.where(_iota((n, n), 0) <= _iota((n, n), 1), 1.0, 0.0).astype(BF16)


def _expand_mat(first_lane, width):
    r = _iota((LANES, width), 0)
    c = _iota((LANES, width), 1)
    return jnp.where(r - first_lane == c // HEAD_DIM, 1.0, 0.0).astype(BF16)


def _group_sum_mat(width, group):
    r = _iota((width, width), 0)
    c = _iota((width, width), 1)
    return jnp.where(r // group == c // group, 1.0, 0.0).astype(BF16)


def _ssd_kernel(z_ref, xbc_ref, sm_ref, cw_ref, cb_ref, dtb_ref, alogc_ref, dsk_ref, ng_ref, o_ref,
                xext, st, *, nb):
    @pl.when(pl.program_id(1) == 0)
    def _():
        st[...] = jnp.zeros(st.shape, F32)
        xext[:, 0:CONV_PAD, :] = jnp.zeros((nb, CONV_PAD, SSM_XBC), F32)

    _interleave(_ssd_chunk(z_ref.at[bb], xbc_ref.at[bb], sm_ref.at[bb], cw_ref, cb_ref, dtb_ref, alogc_ref,
                           dsk_ref, ng_ref, o_ref.at[bb], xext.at[bb], st.at[bb]) for bb in range(nb))


def _ssd_chunk(z_ref, xbc_ref, sm_ref, cw_ref, cb_ref, dtb_ref, alogc_ref, dsk_ref, ng_ref, o_ref, xext, st):
    n = CHUNK
    xc = _causal_conv_silu(xbc_ref[...].astype(F32), xext, cw_ref, cb_ref)
    xs = xc[:, 0:D_GROUP]
    bm = xc[:, D_GROUP:D_GROUP + LANES]
    cm = xc[:, D_GROUP + LANES:D_GROUP + 2 * LANES]
    yield

    dt_c = _softplus(sm_ref[...] + dtb_ref[...])
    da_t = (dt_c * -jnp.exp(alogc_ref[...])).T[SM_DT - N_HEADS:SM_DT + N_HEADS]
    acs_t = _dot_lhs_f32(da_t, _tri_upper(n), terms=2)
    acs_rows = acs_t[N_HEADS:]
    dtx = _dot_lhs_f32(dt_c, _expand_mat(SM_DT, D_GROUP), terms=2)
    acs = _dot_lhs_f32(_rows_to_lanes(acs_t, SM_DT - N_HEADS), _expand_mat(SM_DT, D_GROUP), terms=2)
    acs_last = acs[n - 1:n, :]
    yield

    xdt = xs * dtx
    xdt_b = xdt.astype(BF16)
    bm_b = bm.astype(BF16)
    cm_b = cm.astype(BF16)
    lane = _iota((n, LANES), 1)
    causal = _iota((n, n), 0) >= _iota((n, n), 1)

    y_pairs = []
    for g in range(SSM_GROUPS):
        in_group = (lane // SSM_STATE) == g
        gmat = _nt(jnp.where(in_group, cm, 0.0).astype(BF16), bm_b)
        pair = []
        for h in (2 * g, 2 * g + 1):
            col = acs[:, HEAD_DIM * h:HEAD_DIM * h + 1]
            row = acs_rows[h:h + 1, :]
            decay = jnp.exp(jnp.where(causal, col - row, NEG_INF))
            pair.append(_dot((gmat * decay).astype(BF16), xdt_b[:, g * LANES:(g + 1) * LANES]))
        y_pairs.append(jnp.where(lane < HEAD_DIM, pair[0], pair[1]))
        yield
    y_diag = jnp.concatenate(y_pairs, axis=1)

    state = st[...]
    y_off = jnp.exp(acs) * _dot(cm_b, state.astype(BF16))
    decay_end = jnp.exp(acs_last - acs)
    upd = _dot(bm.T.astype(BF16), (decay_end * xdt).astype(BF16))
    own = (_iota(st.shape, 0) // SSM_STATE) == (_iota(st.shape, 1) // LANES)
    st[...] = jnp.where(own, jnp.exp(acs_last) * state + upd, 0.0)
    yield

    y = (y_diag + y_off + dsk_ref[...] * xs) * _silu(z_ref[...].astype(F32))
    ms = _dot_lhs_f32(y * y, _group_sum_mat(D_GROUP, LANES), terms=2) * (1.0 / LANES)
    o_ref[...] = (y * lax.rsqrt(ms + EPS) * ng_ref[...]).astype(o_ref.dtype)


def _ssd_mixer(proj, small, conv_w, conv_b, dt_bias, a_log, d_skip, norm_g):
    bsz, seq, _ = proj.shape
    n = CHUNK
    dtb = jnp.zeros((1, LANES), F32).at[0, SM_DT:SM_DT + N_HEADS].set(dt_bias)
    alog_c = jnp.zeros((1, LANES), F32).at[0, SM_DT:SM_DT + N_HEADS].set(a_log)
    dsk_x = jnp.repeat(d_skip, HEAD_DIM).reshape(1, D_GROUP)
    full = lambda b, c: (0, 0)
    nb = _seqs_per_step(bsz, 8)
    return pl.pallas_call(
        functools.partial(_ssd_kernel, nb=nb),
        grid=(bsz // nb, seq // n),
        in_specs=[pl.BlockSpec((nb, n, D_GROUP), lambda b, c: (b, c, PB_C_Z // 2)),
                  pl.BlockSpec((nb, n, SSM_XBC), lambda b, c: (b, c, PB_C_XBC * LANES // SSM_XBC)),
                  pl.BlockSpec((nb, n, LANES), lambda b, c: (b, c, 0)),
                  pl.BlockSpec((CONV_WIDTH, SSM_XBC), full),
                  pl.BlockSpec((1, SSM_XBC), full),
                  pl.BlockSpec((1, LANES), full),
                  pl.BlockSpec((1, LANES), full),
                  pl.BlockSpec((1, D_GROUP), full),
                  pl.BlockSpec((1, D_GROUP), full)],
        out_specs=pl.BlockSpec((nb, n, D_GROUP), lambda b, c: (b, c, 0)),
        out_shape=jax.ShapeDtypeStruct((bsz, seq, D_GROUP), BF16),
        scratch_shapes=[pltpu.VMEM((nb, CONV_PAD + n, SSM_XBC), F32), pltpu.VMEM((nb, LANES, D_GROUP), F32)],
        compiler_params=_cparams(("arbitrary", "arbitrary")),
        name="ssd_mixer",
    )(proj, proj, small, conv_w, conv_b.reshape(1, SSM_XBC), dtb, alog_c, dsk_x, norm_g.reshape(1, D_GROUP))


def _mlstm_kernel(qk_ref, v_ref, sm_ref, og_ref, z_ref, cw_ref, cb_ref, ifb_ref, ng_ref, o_ref,
                  xext, c_st, n_st, m_st, *, nb):
    @pl.when(pl.program_id(1) == 0)
    def _():
        c_st[...] = jnp.zeros(c_st.shape, F32)
        n_st[...] = jnp.zeros(n_st.shape, F32)
        m_st[...] = jnp.zeros(m_st.shape, F32)
        xext[:, 0:CONV_PAD, :] = jnp.zeros((nb, CONV_PAD, 2 * D_GROUP), F32)

    _interleave(_mlstm_chunk(qk_ref.at[bb], v_ref.at[bb], sm_ref.at[bb], og_ref.at[bb], z_ref.at[bb], cw_ref,
                             cb_ref, ifb_ref, ng_ref, o_ref.at[bb], xext.at[bb], c_st.at[bb], n_st.at[bb],
                             m_st.at[bb]) for bb in range(nb))


def _mlstm_chunk(qk_ref, v_ref, sm_ref, og_ref, z_ref, cw_ref, cb_ref, ifb_ref, ng_ref, o_ref,
                 xext, c_st, n_st, m_st):
    n = CHUNK
    qk = _causal_conv_silu(qk_ref[...].astype(F32), xext, cw_ref, cb_ref)
    q = qk[:, 0:D_GROUP]
    k = qk[:, D_GROUP:] * (HEAD_DIM ** -0.5)
    q_b = q.astype(BF16)
    k_b = k.astype(BF16)
    v_b = v_ref[...]
    yield

    pre = sm_ref[...] + ifb_ref[...]
    logf = -_softplus(-pre)
    ig = _dot_lhs_f32(pre, _expand_mat(SM_I, D_GROUP), terms=2)
    gates_t = jnp.where(_iota((n, LANES), 1) >= SM_F, logf, pre).T[SM_I:SM_I + 2 * N_HEADS]
    cum_t = _dot_lhs_f32(gates_t, _tri_upper(n), terms=2)
    u_rows = gates_t[0:N_HEADS] - cum_t[N_HEADS:2 * N_HEADS]
    b = _dot_lhs_f32(_rows_to_lanes(cum_t, SM_I), _expand_mat(SM_F, D_GROUP), terms=2)
    b_last = b[n - 1:n, :]
    yield

    m_prev = m_st[...]
    c_prev = c_st[...]
    n_prev = n_st[...]

    lane = _iota((n, LANES), 1)
    causal = _iota((n, n), 0) >= _iota((n, n), 1)
    ones_b = jnp.ones((n, LANES), BF16)
    num_pairs, den_pairs, mt_pairs, wi_pairs = [], [], [], []
    for pr in range(N_HEADS // 2):
        lanes = slice(pr * LANES, (pr + 1) * LANES)
        qp = q[:, lanes]
        kp_b = k_b[:, lanes]
        rhs = jnp.concatenate([v_b[:, lanes], ones_b], axis=1)
        res, mts, wis = [], [], []
        for hh in range(2):
            h = 2 * pr + hh
            in_head = (lane // HEAD_DIM) == hh
            bcol = b[:, HEAD_DIM * h:HEAD_DIM * h + 1]
            dlog = jnp.where(causal, bcol + u_rows[h:h + 1, :], NEG_INF)
            inter = bcol + m_prev[:, HEAD_DIM * h:HEAD_DIM * h + 1]
            m_t = jnp.maximum(inter, jnp.max(dlog, axis=1, keepdims=True))
            s_qk = _nt(jnp.where(in_head, qp, 0.0).astype(BF16), kp_b) * jnp.exp(dlog - m_t)
            res.append(_dot(s_qk.astype(BF16), rhs))
            mts.append(jnp.broadcast_to(m_t, (n, LANES)))
            wis.append(jnp.broadcast_to(jnp.exp(inter - m_t), (n, LANES)))
        first = lane < HEAD_DIM
        num_pairs.append(jnp.where(first, res[0][:, :LANES], res[1][:, :LANES]))
        den_pairs.append(jnp.where(first, res[0][:, LANES:], res[1][:, LANES:]))
        mt_pairs.append(jnp.where(first, mts[0], mts[1]))
        wi_pairs.append(jnp.where(first, wis[0], wis[1]))
        yield
    num_intra = jnp.concatenate(num_pairs, axis=1)
    den_intra = jnp.concatenate(den_pairs, axis=1)
    m_t = jnp.concatenate(mt_pairs, axis=1)
    w_inter = jnp.concatenate(wi_pairs, axis=1)

    head_sum = _group_sum_mat(D_GROUP, HEAD_DIM)
    num = num_intra + w_inter * _dot(q_b, c_prev.astype(BF16))
    den = den_intra + w_inter * _dot_lhs_f32(q * n_prev, head_sum, terms=2)
    hid = num / jnp.maximum(jnp.abs(den), jnp.exp(-m_t))
    yield

    g_end = b_last - b + ig
    m_loc = jnp.max(g_end, axis=0, keepdims=True)
    m_new = jnp.maximum(b_last + m_prev, m_loc)
    a_prev = jnp.exp(b_last + m_prev - m_new)
    a_loc = jnp.exp(m_loc - m_new)
    kw = k * (jnp.exp(g_end - m_loc) * a_loc)
    kw_t = jnp.concatenate([kw[:, :LANES].T, kw[:, LANES:].T], axis=0)
    upd = _dot(kw_t.astype(BF16), v_b)
    own = (_iota(c_st.shape, 0) // HEAD_DIM) == (_iota(c_st.shape, 1) // HEAD_DIM)
    c_st[...] = jnp.where(own, a_prev * c_prev + upd, 0.0)
    n_st[...] = a_prev * n_prev + jnp.sum(kw, axis=0, keepdims=True)
    m_st[...] = m_new
    yield

    hm = _sigmoid(og_ref[...].astype(F32)) * hid
    ms = _dot_lhs_f32(hm * hm, head_sum, terms=2) * (1.0 / HEAD_DIM)
    o_ref[...] = (hm * lax.rsqrt(ms + EPS) * ng_ref[...] * _silu(z_ref[...].astype(F32))).astype(o_ref.dtype)


def _mlstm_mixer(proj, small, conv_w, conv_b, if_b, norm_g):
    bsz, seq, _ = proj.shape
    n = CHUNK
    ifb = jnp.zeros((1, LANES), F32).at[0, SM_I:SM_I + 2 * N_HEADS].set(if_b)
    ng = jnp.tile(norm_g, N_HEADS).reshape(1, D_GROUP)
    full = lambda b, c: (0, 0)
    nb = _seqs_per_step(bsz, 4)
    return pl.pallas_call(
        functools.partial(_mlstm_kernel, nb=nb),
        grid=(bsz // nb, seq // n),
        in_specs=[pl.BlockSpec((nb, n, 2 * D_GROUP), lambda b, c: (b, c, PB_D_QK * LANES // (2 * D_GROUP))),
                  pl.BlockSpec((nb, n, D_GROUP), lambda b, c: (b, c, PB_D_V // 2)),
                  pl.BlockSpec((nb, n, LANES), lambda b, c: (b, c, 0)),
                  pl.BlockSpec((nb, n, D_GROUP), lambda b, c: (b, c, PB_D_O // 2)),
                  pl.BlockSpec((nb, n, D_GROUP), lambda b, c: (b, c, PB_D_Z // 2)),
                  pl.BlockSpec((CONV_WIDTH, 2 * D_GROUP), full),
                  pl.BlockSpec((1, 2 * D_GROUP), full),
                  pl.BlockSpec((1, LANES), full),
                  pl.BlockSpec((1, D_GROUP), full)],
        out_specs=pl.BlockSpec((nb, n, D_GROUP), lambda b, c: (b, c, 0)),
        out_shape=jax.ShapeDtypeStruct((bsz, seq, D_GROUP), BF16),
        scratch_shapes=[pltpu.VMEM((nb, CONV_PAD + n, 2 * D_GROUP), F32),
                        pltpu.VMEM((nb, D_GROUP, D_GROUP), F32),
                        pltpu.VMEM((nb, 1, D_GROUP), F32),
                        pltpu.VMEM((nb, 1, D_GROUP), F32)],
        compiler_params=_cparams(("arbitrary", "arbitrary")),
        name="mlstm_mixer",
    )(proj, proj, small, proj, proj, conv_w, conv_b.reshape(1, 2 * D_GROUP), ifb, ng)


def _outproj_kernel(ya_ref, yb_ref, yc_ref, yd_ref, x_ref, gate_ref, w_ref, fg_ref, o_ref, *, final):
    acc = _dot(ya_ref[0], w_ref[0:D_GROUP, :])
    acc = acc + _dot(yb_ref[0], w_ref[D_GROUP:2 * D_GROUP, :])
    acc = acc + _dot(yc_ref[0], w_ref[2 * D_GROUP:3 * D_GROUP, :])
    acc = acc + _dot(yd_ref[0], w_ref[3 * D_GROUP:4 * D_GROUP, :])
    out = x_ref[0] + gate_ref[0] * acc
    if final:
        ms = jnp.mean(out * out, axis=-1, keepdims=True)
        out = out * lax.rsqrt(ms + EPS) * fg_ref[...]
    o_ref[0] = out


def _out_projection(ys, x, gate, w_bf, final_g, final):
    bsz, seq, d = x.shape
    tm = min(1024, seq)
    yspec = pl.BlockSpec((1, tm, D_GROUP), lambda b, i: (b, i, 0))
    return pl.pallas_call(
        functools.partial(_outproj_kernel, final=final),
        grid=(bsz, seq // tm),
        in_specs=[yspec, yspec, yspec, yspec,
                  pl.BlockSpec((1, tm, d), lambda b, i: (b, i, 0)),
                  pl.BlockSpec((1, 1, d), lambda b, i: (b, 0, 0)),
                  pl.BlockSpec((N_HEADS * D_GROUP, d), lambda b, i: (0, 0)),
                  pl.BlockSpec((1, d), lambda b, i: (0, 0))],
        out_specs=pl.BlockSpec((1, tm, d), lambda b, i: (b, i, 0)),
        out_shape=jax.ShapeDtypeStruct((bsz, seq, d), F32),
        compiler_params=_cparams(("arbitrary", "arbitrary")),
        name="out_projection",
    )(*ys, x, gate, w_bf, final_g.reshape(1, d))


def _cmp_to_sel_t(seq):
    n_cmp = (seq - NSA_CMP_BLOCK) // NSA_CMP_STRIDE + 1
    n_sel = seq // NSA_SEL_BLOCK
    start = np.arange(n_cmp)[:, None] * NSA_CMP_STRIDE
    sel_start = np.arange(n_sel)[None, :] * NSA_SEL_BLOCK
    overlap = np.clip(np.minimum(start + NSA_CMP_BLOCK, sel_start + NSA_SEL_BLOCK)
                      - np.maximum(start, sel_start), 0, None)
    out = np.zeros((n_sel, seq // NSA_CMP_STRIDE), np.float32)
    out[:, :n_cmp] = (overlap / NSA_CMP_BLOCK).T
    return jnp.asarray(out, BF16)


def _mixers(proj, small, kcvc, layer_idx, p):
    seq = proj.shape[1]
    pos = p['nsa_cmp_pos'].reshape(1, NSA_CMP_BLOCK * HEAD_DIM)
    ckv, cvt = _nsa_compress(kcvc, pos, p['nsa_ck_w1'], p['nsa_ck_w2'], p['nsa_cv_w1'], p['nsa_cv_w2'])
    y_a = _nsa_attention(proj, small, ckv, cvt, _cmp_to_sel_t(seq), p['nsa_norm_g'])
    y_b = _diff_attention(proj, p['diff_lam'], p['diff_norm_g'], layer_idx)
    y_c = _ssd_mixer(proj, small, p['ssm_conv_w'], p['ssm_conv_b'], p['ssm_dt_bias'], p['ssm_a_log'],
                     p['ssm_d'], p['ssm_norm_g'])
    y_d = _mlstm_mixer(proj, small, p['ml_conv_w'], p['ml_conv_b'], p['ml_if_b'], p['ml_norm_g'])
    return y_a, y_b, y_c, y_d


_LAYER_PARAMS = ('nsa_cmp_pos', 'nsa_ck_w1', 'nsa_ck_w2', 'nsa_cv_w1', 'nsa_cv_w2', 'nsa_norm_g',
                 'diff_lam', 'diff_norm_g', 'ssm_conv_w', 'ssm_conv_b', 'ssm_dt_bias', 'ssm_a_log',
                 'ssm_d', 'ssm_norm_g', 'ml_conv_w', 'ml_conv_b', 'ml_if_b', 'ml_norm_g')


def kernel(x, c, norm_g, ada_w, ada_b, w_in, w_out, nsa_cmp_pos, nsa_ck_w1, nsa_ck_w2, nsa_cv_w1, nsa_cv_w2, nsa_norm_g, diff_lam, diff_norm_g, ssm_conv_w, ssm_conv_b, ssm_dt_bias, ssm_a_log, ssm_d, ssm_norm_g, ml_conv_w, ml_conv_b, ml_if_b, ml_norm_g, final_g):
    stacked = dict(nsa_cmp_pos=nsa_cmp_pos, nsa_ck_w1=nsa_ck_w1, nsa_ck_w2=nsa_ck_w2, nsa_cv_w1=nsa_cv_w1,
                   nsa_cv_w2=nsa_cv_w2, nsa_norm_g=nsa_norm_g, diff_lam=diff_lam, diff_norm_g=diff_norm_g,
                   ssm_conv_w=ssm_conv_w, ssm_conv_b=ssm_conv_b, ssm_dt_bias=ssm_dt_bias,
                   ssm_a_log=ssm_a_log, ssm_d=ssm_d, ssm_norm_g=ssm_norm_g, ml_conv_w=ml_conv_w,
                   ml_conv_b=ml_conv_b, ml_if_b=ml_if_b, ml_norm_g=ml_norm_g)
    depth = w_in.shape[0]
    bsz, seq, d = x.shape
    mod = _ada_modulation(c, ada_w, ada_b)
    for l in range(depth):
        p = {name: stacked[name][l] for name in _LAYER_PARAMS}
        shift = mod[l, :, 0:d].reshape(bsz, 1, d)
        scale = mod[l, :, d:2 * d].reshape(bsz, 1, d)
        gate = mod[l, :, 2 * d:3 * d].reshape(bsz, 1, d)
        w_bf = _relayout_w_in(w_in[l].astype(BF16))
        proj, small, kcvc = _in_projection(x, norm_g[l], scale, shift, w_bf)
        ys = _mixers(proj, small, kcvc, l, p)
        x = _out_projection(ys, x, gate, w_out[l].astype(BF16), final_g, final=(l == depth - 1))
    return x
```
